```python
import math
import jax, jax.numpy as jnp
from jax import lax
import numpy as np

D_MODEL = 1024
BATCH = 8
SEQ = 4096
DEPTH = 1

CHUNK = 64
D_MIX = D_MODEL
D_RWKV = D_MIX // 2
D_CONV = D_MIX - D_RWKV
HEAD_SIZE = 64
N_HEADS = D_RWKV // HEAD_SIZE
LORA_W = 64
LORA_A = 64
LORA_G = 128
CONV_WIDTH = 31
D_FF = 2816
D_SHIFT = 3 * D_RWKV + LORA_W + LORA_A + LORA_G
D_IN = D_SHIFT + 2 * D_CONV
RMS_EPS = 1e-6
GN_EPS = 64e-5
LN_EPS = 1e-5
DECAY_SCALE = math.exp(-0.5)

kernel_name = "hybrid_rwkv7_conformer_conv_macaron_block"


def rmsnorm(x, g):
    xf = x.astype(jnp.float32)
    y = xf * lax.rsqrt(jnp.mean(xf * xf, axis=-1, keepdims=True) + RMS_EPS)
    return (y * g.astype(jnp.float32)).astype(x.dtype)


def layernorm(x, g, b):
    xf = x.astype(jnp.float32)
    mu = jnp.mean(xf, axis=-1, keepdims=True)
    var = jnp.mean(jnp.square(xf - mu), axis=-1, keepdims=True)
    y = (xf - mu) * lax.rsqrt(var + LN_EPS)
    return (y * g.astype(jnp.float32) + b.astype(jnp.float32)).astype(x.dtype)


def swiglu(x, w_gu, w_down):
    gu = x @ w_gu
    gate, up = gu[..., :D_FF], gu[..., D_FF:]
    return (jax.nn.silu(gate) * up) @ w_down


def token_shift(y):
    return jnp.pad(y[:, :-1], ((0, 0), (1, 0), (0, 0)))


def rwkv7_recurrence(r, w, k, v, z, b):
    bsz, seq, nh, n = r.shape
    n_chunks = seq // CHUNK

    def to_chunks(t):
        return jnp.transpose(t, (1, 0, 2, 3)).reshape(n_chunks, CHUNK, bsz, nh, n)

    xs = tuple(to_chunks(t) for t in (r, w, k, v, z, b))

    def frame_step(state, inp):
        r_t, w_t, k_t, v_t, z_t, b_t = inp
        sz = jnp.einsum('bhij,bhj->bhi', state, z_t)
        state = (state * w_t[:, :, None, :]
                 + sz[..., None] * b_t[:, :, None, :]
                 + v_t[..., None] * k_t[:, :, None, :])
        y_t = jnp.einsum('bhij,bhj->bhi', state, r_t)
        return state, y_t

    def chunk_step(state, chunk_inp):
        return lax.scan(frame_step, state, chunk_inp)

    state0 = jnp.zeros((bsz, nh, n, n), jnp.float32)
    _, ys = lax.scan(chunk_step, state0, xs)
    ys = ys.reshape(seq, bsz, nh, n)
    return jnp.transpose(ys, (1, 0, 2, 3))


def hybrid_mixer(h, w_in, shift_mu, w_up, w0, a_up, a0, g_up, k_k, k_a, r_k,
                 gn_w, gn_b, conv_dw, conv_b, conv_ln_w, conv_ln_b, w_out):
    bsz, seq, _ = h.shape
    p = h @ w_in
    ps, pc = p[..., :D_SHIFT], p[..., D_SHIFT:]

    ps = ps + (token_shift(ps) - ps) * shift_mu
    o1, o2, o3 = D_RWKV, 2 * D_RWKV, 3 * D_RWKV
    o4, o5 = o3 + LORA_W, o3 + LORA_W + LORA_A
    r, k, v = ps[..., :o1], ps[..., o1:o2], ps[..., o2:o3]
    xw, xa, xg = ps[..., o3:o4], ps[..., o4:o5], ps[..., o5:]

    d = (w0 + jnp.tanh(xw) @ w_up).astype(jnp.float32)
    decay = jnp.exp(-DECAY_SCALE * jax.nn.sigmoid(d))
    a = jax.nn.sigmoid(a0 + xa @ a_up)
    g = jax.nn.sigmoid(xg) @ g_up

    heads = lambda t: t.reshape(bsz, seq, N_HEADS, HEAD_SIZE).astype(jnp.float32)
    kk = heads(k * k_k)
    kk = kk * lax.rsqrt(jnp.maximum(jnp.sum(kk * kk, axis=-1, keepdims=True), 1e-12))
    k = k * (1.0 + (a - 1.0) * k_a)
    rh, kh, vh, ah, wh = heads(r), heads(k), heads(v), heads(a), heads(decay)

    y = rwkv7_recurrence(rh, wh, kh, vh, -kk, kk * ah)
    mu = jnp.mean(y, axis=-1, keepdims=True)
    var = jnp.mean(jnp.square(y - mu), axis=-1, keepdims=True)
    y = (y - mu) * lax.rsqrt(var + GN_EPS)
    y = y * gn_w.astype(jnp.float32).reshape(N_HEADS, HEAD_SIZE) + gn_b.astype(jnp.float32).reshape(N_HEADS, HEAD_SIZE)
    bonus = jnp.sum(rh * kh * r_k.astype(jnp.float32), axis=-1, keepdims=True) * vh
    y = (y + bonus).reshape(bsz, seq, D_RWKV).astype(h.dtype)
    out_a = y * g

    glu = pc[..., :D_CONV] * jax.nn.sigmoid(pc[..., D_CONV:])
    c = lax.conv_general_dilated(
        glu, conv_dw[:, None, :], window_strides=(1,),
        padding=[(CONV_WIDTH - 1, 0)],
        dimension_numbers=('NWC', 'WIO', 'NWC'),
        feature_group_count=D_CONV) + conv_b
    out_b = jax.nn.silu(layernorm(c, conv_ln_w, conv_ln_b))

    return jnp.concatenate([out_a, out_b], axis=-1) @ w_out


def _fwd_setup_inputs(seed: int = 0) -> dict:
    key = jax.random.key(seed)
    ks = iter(jax.random.split(key, 40))
    f32 = jnp.float32

    def nrm(shape, scale):
        return jax.random.normal(next(ks), shape, f32) * scale

    def gain(shape):
        return 1.0 + nrm(shape, 0.02)

    L = DEPTH
    return {
        "x": nrm((BATCH, SEQ, D_MODEL), 1.0),
        "ffn1_norm_pre": gain((L, D_MODEL)),
        "ffn1_norm_post": gain((L, D_MODEL)),
        "ffn1_w_gu": nrm((L, D_MODEL, 2 * D_FF), D_MODEL ** -0.5),
        "ffn1_w_down": nrm((L, D_FF, D_MODEL), D_FF ** -0.5),
        "mix_norm_pre": gain((L, D_MODEL)),
        "mix_norm_post": gain((L, D_MODEL)),
        "w_in": nrm((L, D_MODEL, D_IN), D_MODEL ** -0.5),
        "shift_mu": jax.random.uniform(next(ks), (L, D_SHIFT), f32, 0.1, 0.9),
        "w_up": nrm((L, LORA_W, D_RWKV), 0.3 * LORA_W ** -0.5),
        "w0": nrm((L, D_RWKV), 0.5),
        "a_up": nrm((L, LORA_A, D_RWKV), 0.3 * LORA_A ** -0.5),
        "a0": nrm((L, D_RWKV), 0.1),
        "g_up": nrm((L, LORA_G, D_RWKV), LORA_G ** -0.5),
        "k_k": 0.85 + nrm((L, D_RWKV), 0.02),
        "k_a": 1.0 + nrm((L, D_RWKV), 0.02),
        "r_k": nrm((L, N_HEADS, HEAD_SIZE), 0.1),
        "gn_w": gain((L, D_RWKV)),
        "gn_b": nrm((L, D_RWKV), 0.01),
        "conv_dw": nrm((L, CONV_WIDTH, D_CONV), CONV_WIDTH ** -0.5),
        "conv_b": nrm((L, D_CONV), 0.01),
        "conv_ln_w": gain((L, D_CONV)),
        "conv_ln_b": nrm((L, D_CONV), 0.01),
        "w_out": nrm((L, D_MIX, D_MODEL), D_MIX ** -0.5),
        "ffn2_norm_pre": gain((L, D_MODEL)),
        "ffn2_norm_post": gain((L, D_MODEL)),
        "ffn2_w_gu": nrm((L, D_MODEL, 2 * D_FF), D_MODEL ** -0.5),
        "ffn2_w_down": nrm((L, D_FF, D_MODEL), D_FF ** -0.5),
    }


def _fwd_reference(x, ffn1_norm_pre, ffn1_norm_post, ffn1_w_gu, ffn1_w_down,
              mix_norm_pre, mix_norm_post, w_in, shift_mu, w_up, w0, a_up, a0,
              g_up, k_k, k_a, r_k, gn_w, gn_b, conv_dw, conv_b, conv_ln_w,
              conv_ln_b, w_out, ffn2_norm_pre, ffn2_norm_post, ffn2_w_gu,
              ffn2_w_down):
    for l in range(DEPTH):
        f = swiglu(rmsnorm(x, ffn1_norm_pre[l]), ffn1_w_gu[l], ffn1_w_down[l])
        x = x + 0.5 * rmsnorm(f, ffn1_norm_post[l])
        m = hybrid_mixer(rmsnorm(x, mix_norm_pre[l]), w_in[l], shift_mu[l], w_up[l],
                         w0[l], a_up[l], a0[l], g_up[l], k_k[l], k_a[l], r_k[l],
                         gn_w[l], gn_b[l], conv_dw[l], conv_b[l], conv_ln_w[l],
                         conv_ln_b[l], w_out[l])
        x = x + rmsnorm(m, mix_norm_post[l])
        f = swiglu(rmsnorm(x, ffn2_norm_pre[l]), ffn2_w_gu[l], ffn2_w_down[l])
        x = x + 0.5 * rmsnorm(f, ffn2_norm_post[l])
    return x


import jax as _jax
import jax.numpy as _jnp

TWIN_FORMAT = 'train_step'
FWD_PARAMS = ['x', 'ffn1_norm_pre', 'ffn1_norm_post', 'ffn1_w_gu', 'ffn1_w_down', 'mix_norm_pre', 'mix_norm_post', 'w_in', 'shift_mu', 'w_up', 'w0', 'a_up', 'a0', 'g_up', 'k_k', 'k_a', 'r_k', 'gn_w', 'gn_b', 'conv_dw', 'conv_b', 'conv_ln_w', 'conv_ln_b', 'w_out', 'ffn2_norm_pre', 'ffn2_norm_post', 'ffn2_w_gu', 'ffn2_w_down']
TWIN_WEIGHTS = ['ffn1_norm_pre', 'ffn1_norm_post', 'ffn1_w_gu', 'ffn1_w_down', 'mix_norm_pre', 'mix_norm_post', 'w_in', 'shift_mu', 'w_up', 'w0', 'a_up', 'a0', 'g_up', 'k_k', 'k_a', 'r_k', 'gn_w', 'gn_b', 'conv_dw', 'conv_b', 'conv_ln_w', 'conv_ln_b', 'w_out', 'ffn2_norm_pre', 'ffn2_norm_post', 'ffn2_w_gu', 'ffn2_w_down']
TWIN_DIFF_INPUT = 'x'
TWIN_INPUTS = ['x', 'ffn1_norm_pre', 'ffn1_norm_post', 'ffn1_w_gu', 'ffn1_w_down', 'mix_norm_pre', 'mix_norm_post', 'w_in', 'shift_mu', 'w_up', 'w0', 'a_up', 'a0', 'g_up', 'k_k', 'k_a', 'r_k', 'gn_w', 'gn_b', 'conv_dw', 'conv_b', 'conv_ln_w', 'conv_ln_b', 'w_out', 'ffn2_norm_pre', 'ffn2_norm_post', 'ffn2_w_gu', 'ffn2_w_down', 'loss_target', 'm_ffn1_norm_pre', 'm_ffn1_norm_post', 'm_ffn1_w_gu', 'm_ffn1_w_down', 'm_mix_norm_pre', 'm_mix_norm_post', 'm_w_in', 'm_shift_mu', 'm_w_up', 'm_w0', 'm_a_up', 'm_a0', 'm_g_up', 'm_k_k', 'm_k_a', 'm_r_k', 'm_gn_w', 'm_gn_b', 'm_conv_dw', 'm_conv_b', 'm_conv_ln_w', 'm_conv_ln_b', 'm_w_out', 'm_ffn2_norm_pre', 'm_ffn2_norm_post', 'm_ffn2_w_gu', 'm_ffn2_w_down', 'v_ffn1_norm_pre', 'v_ffn1_norm_post', 'v_ffn1_w_gu', 'v_ffn1_w_down', 'v_mix_norm_pre', 'v_mix_norm_post', 'v_w_in', 'v_shift_mu', 'v_w_up', 'v_w0', 'v_a_up', 'v_a0', 'v_g_up', 'v_k_k', 'v_k_a', 'v_r_k', 'v_gn_w', 'v_gn_b', 'v_conv_dw', 'v_conv_b', 'v_conv_ln_w', 'v_conv_ln_b', 'v_w_out', 'v_ffn2_norm_pre', 'v_ffn2_norm_post', 'v_ffn2_w_gu', 'v_ffn2_w_down']
TWIN_OUTPUTS = ['loss', 'grad_x', 'grad_ffn1_norm_pre', 'grad_ffn1_norm_post', 'grad_ffn1_w_gu', 'grad_ffn1_w_down', 'grad_mix_norm_pre', 'grad_mix_norm_post', 'grad_w_in', 'grad_shift_mu', 'grad_w_up', 'grad_w0', 'grad_a_up', 'grad_a0', 'grad_g_up', 'grad_k_k', 'grad_k_a', 'grad_r_k', 'grad_gn_w', 'grad_gn_b', 'grad_conv_dw', 'grad_conv_b', 'grad_conv_ln_w', 'grad_conv_ln_b', 'grad_w_out', 'grad_ffn2_norm_pre', 'grad_ffn2_norm_post', 'grad_ffn2_w_gu', 'grad_ffn2_w_down', 'delta_ffn1_norm_pre', 'delta_ffn1_norm_post', 'delta_ffn1_w_gu', 'delta_ffn1_w_down', 'delta_mix_norm_pre', 'delta_mix_norm_post', 'delta_w_in', 'delta_shift_mu', 'delta_w_up', 'delta_w0', 'delta_a_up', 'delta_a0', 'delta_g_up', 'delta_k_k', 'delta_k_a', 'delta_r_k', 'delta_gn_w', 'delta_gn_b', 'delta_conv_dw', 'delta_conv_b', 'delta_conv_ln_w', 'delta_conv_ln_b', 'delta_w_out', 'delta_ffn2_norm_pre', 'delta_ffn2_norm_post', 'delta_ffn2_w_gu', 'delta_ffn2_w_down', 'new_m_ffn1_norm_pre', 'new_m_ffn1_norm_post', 'new_m_ffn1_w_gu', 'new_m_ffn1_w_down', 'new_m_mix_norm_pre', 'new_m_mix_norm_post', 'new_m_w_in', 'new_m_shift_mu', 'new_m_w_up', 'new_m_w0', 'new_m_a_up', 'new_m_a0', 'new_m_g_up', 'new_m_k_k', 'new_m_k_a', 'new_m_r_k', 'new_m_gn_w', 'new_m_gn_b', 'new_m_conv_dw', 'new_m_conv_b', 'new_m_conv_ln_w', 'new_m_conv_ln_b', 'new_m_w_out', 'new_m_ffn2_norm_pre', 'new_m_ffn2_norm_post', 'new_m_ffn2_w_gu', 'new_m_ffn2_w_down', 'new_v_ffn1_norm_pre', 'new_v_ffn1_norm_post', 'new_v_ffn1_w_gu', 'new_v_ffn1_w_down', 'new_v_mix_norm_pre', 'new_v_mix_norm_post', 'new_v_w_in', 'new_v_shift_mu', 'new_v_w_up', 'new_v_w0', 'new_v_a_up', 'new_v_a0', 'new_v_g_up', 'new_v_k_k', 'new_v_k_a', 'new_v_r_k', 'new_v_gn_w', 'new_v_gn_b', 'new_v_conv_dw', 'new_v_conv_b', 'new_v_conv_ln_w', 'new_v_conv_ln_b', 'new_v_w_out', 'new_v_ffn2_norm_pre', 'new_v_ffn2_norm_post', 'new_v_ffn2_w_gu', 'new_v_ffn2_w_down']
TWIN_LEAF_KINDS = {'loss': 'loss', 'grad_x': 'grad_x', 'grad_ffn1_norm_pre': 'grad_w', 'grad_ffn1_norm_post': 'grad_w', 'grad_ffn1_w_gu': 'grad_w', 'grad_ffn1_w_down': 'grad_w', 'grad_mix_norm_pre': 'grad_w', 'grad_mix_norm_post': 'grad_w', 'grad_w_in': 'grad_w', 'grad_shift_mu': 'grad_w', 'grad_w_up': 'grad_w', 'grad_w0': 'grad_w', 'grad_a_up': 'grad_w', 'grad_a0': 'grad_w', 'grad_g_up': 'grad_w', 'grad_k_k': 'grad_w', 'grad_k_a': 'grad_w', 'grad_r_k': 'grad_w', 'grad_gn_w': 'grad_w', 'grad_gn_b': 'grad_w', 'grad_conv_dw': 'grad_w', 'grad_conv_b': 'grad_w', 'grad_conv_ln_w': 'grad_w', 'grad_conv_ln_b': 'grad_w', 'grad_w_out': 'grad_w', 'grad_ffn2_norm_pre': 'grad_w', 'grad_ffn2_norm_post': 'grad_w', 'grad_ffn2_w_gu': 'grad_w', 'grad_ffn2_w_down': 'grad_w', 'delta_ffn1_norm_pre': 'delta_w', 'delta_ffn1_norm_post': 'delta_w', 'delta_ffn1_w_gu': 'delta_w', 'delta_ffn1_w_down': 'delta_w', 'delta_mix_norm_pre': 'delta_w', 'delta_mix_norm_post': 'delta_w', 'delta_w_in': 'delta_w', 'delta_shift_mu': 'delta_w', 'delta_w_up': 'delta_w', 'delta_w0': 'delta_w', 'delta_a_up': 'delta_w', 'delta_a0': 'delta_w', 'delta_g_up': 'delta_w', 'delta_k_k': 'delta_w', 'delta_k_a': 'delta_w', 'delta_r_k': 'delta_w', 'delta_gn_w': 'delta_w', 'delta_gn_b': 'delta_w', 'delta_conv_dw': 'delta_w', 'delta_conv_b': 'delta_w', 'delta_conv_ln_w': 'delta_w', 'delta_conv_ln_b': 'delta_w', 'delta_w_out': 'delta_w', 'delta_ffn2_norm_pre': 'delta_w', 'delta_ffn2_norm_post': 'delta_w', 'delta_ffn2_w_gu': 'delta_w', 'delta_ffn2_w_down': 'delta_w', 'new_m_ffn1_norm_pre': 'new_m', 'new_m_ffn1_norm_post': 'new_m', 'new_m_ffn1_w_gu': 'new_m', 'new_m_ffn1_w_down': 'new_m', 'new_m_mix_norm_pre': 'new_m', 'new_m_mix_norm_post': 'new_m', 'new_m_w_in': 'new_m', 'new_m_shift_mu': 'new_m', 'new_m_w_up': 'new_m', 'new_m_w0': 'new_m', 'new_m_a_up': 'new_m', 'new_m_a0': 'new_m', 'new_m_g_up': 'new_m', 'new_m_k_k': 'new_m', 'new_m_k_a': 'new_m', 'new_m_r_k': 'new_m', 'new_m_gn_w': 'new_m', 'new_m_gn_b': 'new_m', 'new_m_conv_dw': 'new_m', 'new_m_conv_b': 'new_m', 'new_m_conv_ln_w': 'new_m', 'new_m_conv_ln_b': 'new_m', 'new_m_w_out': 'new_m', 'new_m_ffn2_norm_pre': 'new_m', 'new_m_ffn2_norm_post': 'new_m', 'new_m_ffn2_w_gu': 'new_m', 'new_m_ffn2_w_down': 'new_m', 'new_v_ffn1_norm_pre': 'new_v', 'new_v_ffn1_norm_post': 'new_v', 'new_v_ffn1_w_gu': 'new_v', 'new_v_ffn1_w_down': 'new_v', 'new_v_mix_norm_pre': 'new_v', 'new_v_mix_norm_post': 'new_v', 'new_v_w_in': 'new_v', 'new_v_shift_mu': 'new_v', 'new_v_w_up': 'new_v', 'new_v_w0': 'new_v', 'new_v_a_up': 'new_v', 'new_v_a0': 'new_v', 'new_v_g_up': 'new_v', 'new_v_k_k': 'new_v', 'new_v_k_a': 'new_v', 'new_v_r_k': 'new_v', 'new_v_gn_w': 'new_v', 'new_v_gn_b': 'new_v', 'new_v_conv_dw': 'new_v', 'new_v_conv_b': 'new_v', 'new_v_conv_ln_w': 'new_v', 'new_v_conv_ln_b': 'new_v', 'new_v_w_out': 'new_v', 'new_v_ffn2_norm_pre': 'new_v', 'new_v_ffn2_norm_post': 'new_v', 'new_v_ffn2_w_gu': 'new_v', 'new_v_ffn2_w_down': 'new_v'}


def _forward(args):
    return _fwd_reference(*[args[k] for k in FWD_PARAMS])


def _output_shape():
    out = _jax.eval_shape(lambda: _forward(_fwd_setup_inputs(0)))
    return out.shape, out.dtype

N_MICROBATCH = 1
ADAM_LR = 0.001
ADAM_B1 = 0.9
ADAM_B2 = 0.999
ADAM_EPS = 1e-08
ADAM_WD = 0.01
ADAM_STEP = 10
PER_EXAMPLE_BATCH_AXIS = {'x': 0, 'loss_target': 0}
SHARED_INPUTS = []
_WEIGHT_DTYPES = {'ffn1_norm_pre': _jnp.float32, 'ffn1_norm_post': _jnp.float32, 'ffn1_w_gu': _jnp.float32, 'ffn1_w_down': _jnp.float32, 'mix_norm_pre': _jnp.float32, 'mix_norm_post': _jnp.float32, 'w_in': _jnp.float32, 'shift_mu': _jnp.float32, 'w_up': _jnp.float32, 'w0': _jnp.float32, 'a_up': _jnp.float32, 'a0': _jnp.float32, 'g_up': _jnp.float32, 'k_k': _jnp.float32, 'k_a': _jnp.float32, 'r_k': _jnp.float32, 'gn_w': _jnp.float32, 'gn_b': _jnp.float32, 'conv_dw': _jnp.float32, 'conv_b': _jnp.float32, 'conv_ln_w': _jnp.float32, 'conv_ln_b': _jnp.float32, 'w_out': _jnp.float32, 'ffn2_norm_pre': _jnp.float32, 'ffn2_norm_post': _jnp.float32, 'ffn2_w_gu': _jnp.float32, 'ffn2_w_down': _jnp.float32}
MOMENT_SCALE = {'ffn1_norm_pre': 4.290416e-01, 'ffn1_norm_post': 7.894914e+00, 'ffn1_w_gu': 1.834647e-01, 'ffn1_w_down': 3.077663e-01, 'mix_norm_pre': 5.020552e-01, 'mix_norm_post': 3.210657e+01, 'w_in': 3.053677e-01, 'shift_mu': 5.728187e-01, 'w_up': 3.970342e-02, 'w0': 1.264453e-01, 'a_up': 1.318090e-01, 'a0': 2.027401e-01, 'g_up': 3.540856e-01, 'k_k': 9.038760e-02, 'k_a': 4.411355e-01, 'r_k': 6.280579e-01, 'gn_w': 4.310969e-01, 'gn_b': 1.535712e+00, 'conv_dw': 3.722389e-01, 'conv_b': 2.123599e+00, 'conv_ln_w': 9.213612e-01, 'conv_ln_b': 1.246152e+00, 'w_out': 4.545856e-01, 'ffn2_norm_pre': 3.868495e-01, 'ffn2_norm_post': 7.945261e+00, 'ffn2_w_gu': 1.675295e-01, 'ffn2_w_down': 3.356286e-01}


def _to_microbatches(a, axis):
    t = _jnp.moveaxis(a, axis, 0)
    t = t.reshape((N_MICROBATCH, t.shape[0] // N_MICROBATCH) + t.shape[1:])
    return _jnp.moveaxis(t, 1, axis + 1)


def setup_inputs(seed: int = 0) -> dict:
    inp = _fwd_setup_inputs(seed)
    key = _jax.random.fold_in(_jax.random.key(seed), 7919)
    shape, _ = _output_shape()
    out = dict(inp)
    out["loss_target"] = _jax.random.normal(_jax.random.fold_in(key, 0), shape, _jnp.float32)
    for i, name in enumerate(TWIN_WEIGHTS):
        w = inp[name].astype(_jnp.float32)
        if MOMENT_SCALE is None:
            s = _jnp.sqrt(_jnp.mean(_jnp.square(w)) + 1e-30)
        else:
            s = MOMENT_SCALE[name]
        km, kv = _jax.random.split(_jax.random.fold_in(key, i + 1))
        out[name] = w
        out["m_" + name] = s * _jax.random.normal(km, w.shape, _jnp.float32)
        out["v_" + name] = (s * s) * _jax.random.uniform(kv, w.shape, _jnp.float32, 0.5, 1.5)
    if N_MICROBATCH > 1:
        for name, axis in PER_EXAMPLE_BATCH_AXIS.items():
            out[name] = _to_microbatches(out[name], axis)
    return {'x': out['x'], 'ffn1_norm_pre': out['ffn1_norm_pre'], 'ffn1_norm_post': out['ffn1_norm_post'], 'ffn1_w_gu': out['ffn1_w_gu'], 'ffn1_w_down': out['ffn1_w_down'], 'mix_norm_pre': out['mix_norm_pre'], 'mix_norm_post': out['mix_norm_post'], 'w_in': out['w_in'], 'shift_mu': out['shift_mu'], 'w_up': out['w_up'], 'w0': out['w0'], 'a_up': out['a_up'], 'a0': out['a0'], 'g_up': out['g_up'], 'k_k': out['k_k'], 'k_a': out['k_a'], 'r_k': out['r_k'], 'gn_w': out['gn_w'], 'gn_b': out['gn_b'], 'conv_dw': out['conv_dw'], 'conv_b': out['conv_b'], 'conv_ln_w': out['conv_ln_w'], 'conv_ln_b': out['conv_ln_b'], 'w_out': out['w_out'], 'ffn2_norm_pre': out['ffn2_norm_pre'], 'ffn2_norm_post': out['ffn2_norm_post'], 'ffn2_w_gu': out['ffn2_w_gu'], 'ffn2_w_down': out['ffn2_w_down'], 'loss_target': out['loss_target'], 'm_ffn1_norm_pre': out['m_ffn1_norm_pre'], 'm_ffn1_norm_post': out['m_ffn1_norm_post'], 'm_ffn1_w_gu': out['m_ffn1_w_gu'], 'm_ffn1_w_down': out['m_ffn1_w_down'], 'm_mix_norm_pre': out['m_mix_norm_pre'], 'm_mix_norm_post': out['m_mix_norm_post'], 'm_w_in': out['m_w_in'], 'm_shift_mu': out['m_shift_mu'], 'm_w_up': out['m_w_up'], 'm_w0': out['m_w0'], 'm_a_up': out['m_a_up'], 'm_a0': out['m_a0'], 'm_g_up': out['m_g_up'], 'm_k_k': out['m_k_k'], 'm_k_a': out['m_k_a'], 'm_r_k': out['m_r_k'], 'm_gn_w': out['m_gn_w'], 'm_gn_b': out['m_gn_b'], 'm_conv_dw': out['m_conv_dw'], 'm_conv_b': out['m_conv_b'], 'm_conv_ln_w': out['m_conv_ln_w'], 'm_conv_ln_b': out['m_conv_ln_b'], 'm_w_out': out['m_w_out'], 'm_ffn2_norm_pre': out['m_ffn2_norm_pre'], 'm_ffn2_norm_post': out['m_ffn2_norm_post'], 'm_ffn2_w_gu': out['m_ffn2_w_gu'], 'm_ffn2_w_down': out['m_ffn2_w_down'], 'v_ffn1_norm_pre': out['v_ffn1_norm_pre'], 'v_ffn1_norm_post': out['v_ffn1_norm_post'], 'v_ffn1_w_gu': out['v_ffn1_w_gu'], 'v_ffn1_w_down': out['v_ffn1_w_down'], 'v_mix_norm_pre': out['v_mix_norm_pre'], 'v_mix_norm_post': out['v_mix_norm_post'], 'v_w_in': out['v_w_in'], 'v_shift_mu': out['v_shift_mu'], 'v_w_up': out['v_w_up'], 'v_w0': out['v_w0'], 'v_a_up': out['v_a_up'], 'v_a0': out['v_a0'], 'v_g_up': out['v_g_up'], 'v_k_k': out['v_k_k'], 'v_k_a': out['v_k_a'], 'v_r_k': out['v_r_k'], 'v_gn_w': out['v_gn_w'], 'v_gn_b': out['v_gn_b'], 'v_conv_dw': out['v_conv_dw'], 'v_conv_b': out['v_conv_b'], 'v_conv_ln_w': out['v_conv_ln_w'], 'v_conv_ln_b': out['v_conv_ln_b'], 'v_w_out': out['v_w_out'], 'v_ffn2_norm_pre': out['v_ffn2_norm_pre'], 'v_ffn2_norm_post': out['v_ffn2_norm_post'], 'v_ffn2_w_gu': out['v_ffn2_w_gu'], 'v_ffn2_w_down': out['v_ffn2_w_down']}


def _loss(weights, diff, rest, loss_target):
    with _jax.named_scope("forward"):
        args = {**rest, TWIN_DIFF_INPUT: diff, **{k: w.astype(_WEIGHT_DTYPES[k]) for k, w in weights.items()}}
        y = _forward(args)
    with _jax.named_scope("loss_head"):
        err = _jnp.square(y.astype(_jnp.float32) - loss_target)
        return 0.5 * _jnp.sum(_jnp.mean(err, axis=-1)) if err.ndim else 0.5 * err


def _adamw(w, g, m, v):
    m = ADAM_B1 * m + (1.0 - ADAM_B1) * g
    v = ADAM_B2 * v + (1.0 - ADAM_B2) * _jnp.square(g)
    m_hat = m / (1.0 - ADAM_B1 ** ADAM_STEP)
    v_hat = v / (1.0 - ADAM_B2 ** ADAM_STEP)
    delta = -ADAM_LR * (m_hat / (_jnp.sqrt(v_hat) + ADAM_EPS) + ADAM_WD * w)
    return delta, m, v


def reference(x, ffn1_norm_pre, ffn1_norm_post, ffn1_w_gu, ffn1_w_down, mix_norm_pre, mix_norm_post, w_in, shift_mu, w_up, w0, a_up, a0, g_up, k_k, k_a, r_k, gn_w, gn_b, conv_dw, conv_b, conv_ln_w, conv_ln_b, w_out, ffn2_norm_pre, ffn2_norm_post, ffn2_w_gu, ffn2_w_down, loss_target, m_ffn1_norm_pre, m_ffn1_norm_post, m_ffn1_w_gu, m_ffn1_w_down, m_mix_norm_pre, m_mix_norm_post, m_w_in, m_shift_mu, m_w_up, m_w0, m_a_up, m_a0, m_g_up, m_k_k, m_k_a, m_r_k, m_gn_w, m_gn_b, m_conv_dw, m_conv_b, m_conv_ln_w, m_conv_ln_b, m_w_out, m_ffn2_norm_pre, m_ffn2_norm_post, m_ffn2_w_gu, m_ffn2_w_down, v_ffn1_norm_pre, v_ffn1_norm_post, v_ffn1_w_gu, v_ffn1_w_down, v_mix_norm_pre, v_mix_norm_post, v_w_in, v_shift_mu, v_w_up, v_w0, v_a_up, v_a0, v_g_up, v_k_k, v_k_a, v_r_k, v_gn_w, v_gn_b, v_conv_dw, v_conv_b, v_conv_ln_w, v_conv_ln_b, v_w_out, v_ffn2_norm_pre, v_ffn2_norm_post, v_ffn2_w_gu, v_ffn2_w_down):
    given = dict(x=x, ffn1_norm_pre=ffn1_norm_pre, ffn1_norm_post=ffn1_norm_post, ffn1_w_gu=ffn1_w_gu, ffn1_w_down=ffn1_w_down, mix_norm_pre=mix_norm_pre, mix_norm_post=mix_norm_post, w_in=w_in, shift_mu=shift_mu, w_up=w_up, w0=w0, a_up=a_up, a0=a0, g_up=g_up, k_k=k_k, k_a=k_a, r_k=r_k, gn_w=gn_w, gn_b=gn_b, conv_dw=conv_dw, conv_b=conv_b, conv_ln_w=conv_ln_w, conv_ln_b=conv_ln_b, w_out=w_out, ffn2_norm_pre=ffn2_norm_pre, ffn2_norm_post=ffn2_norm_post, ffn2_w_gu=ffn2_w_gu, ffn2_w_down=ffn2_w_down, loss_target=loss_target, m_ffn1_norm_pre=m_ffn1_norm_pre, m_ffn1_norm_post=m_ffn1_norm_post, m_ffn1_w_gu=m_ffn1_w_gu, m_ffn1_w_down=m_ffn1_w_down, m_mix_norm_pre=m_mix_norm_pre, m_mix_norm_post=m_mix_norm_post, m_w_in=m_w_in, m_shift_mu=m_shift_mu, m_w_up=m_w_up, m_w0=m_w0, m_a_up=m_a_up, m_a0=m_a0, m_g_up=m_g_up, m_k_k=m_k_k, m_k_a=m_k_a, m_r_k=m_r_k, m_gn_w=m_gn_w, m_gn_b=m_gn_b, m_conv_dw=m_conv_dw, m_conv_b=m_conv_b, m_conv_ln_w=m_conv_ln_w, m_conv_ln_b=m_conv_ln_b, m_w_out=m_w_out, m_ffn2_norm_pre=m_ffn2_norm_pre, m_ffn2_norm_post=m_ffn2_norm_post, m_ffn2_w_gu=m_ffn2_w_gu, m_ffn2_w_down=m_ffn2_w_down, v_ffn1_norm_pre=v_ffn1_norm_pre, v_ffn1_norm_post=v_ffn1_norm_post, v_ffn1_w_gu=v_ffn1_w_gu, v_ffn1_w_down=v_ffn1_w_down, v_mix_norm_pre=v_mix_norm_pre, v_mix_norm_post=v_mix_norm_post, v_w_in=v_w_in, v_shift_mu=v_shift_mu, v_w_up=v_w_up, v_w0=v_w0, v_a_up=v_a_up, v_a0=v_a0, v_g_up=v_g_up, v_k_k=v_k_k, v_k_a=v_k_a, v_r_k=v_r_k, v_gn_w=v_gn_w, v_gn_b=v_gn_b, v_conv_dw=v_conv_dw, v_conv_b=v_conv_b, v_conv_ln_w=v_conv_ln_w, v_conv_ln_b=v_conv_ln_b, v_w_out=v_w_out, v_ffn2_norm_pre=v_ffn2_norm_pre, v_ffn2_norm_post=v_ffn2_norm_post, v_ffn2_w_gu=v_ffn2_w_gu, v_ffn2_w_down=v_ffn2_w_down)
    weights = {n: given[n] for n in TWIN_WEIGHTS}
    shared = {n: given[n] for n in SHARED_INPUTS}
    per_example = {n: given[n] for n in ['x']}
    grad_fn = _jax.value_and_grad(_loss, argnums=(0, 1))

    def one_microbatch(ex, loss_target):
        ex = dict(ex)
        diff = ex.pop(TWIN_DIFF_INPUT)
        return grad_fn(weights, diff, {**shared, **ex}, loss_target)

    if N_MICROBATCH == 1:
        loss, (grad_w, grad_x) = one_microbatch(per_example, given["loss_target"])
    else:
        def body(carry, xs):
            loss_sum, grad_sum = carry
            l_k, (gw_k, gx_k) = one_microbatch(xs[0], xs[1])
            with _jax.named_scope("update"):
                return (loss_sum + l_k, _jax.tree.map(_jnp.add, grad_sum, gw_k)), gx_k

        init = (_jnp.zeros((), _jnp.float32), _jax.tree.map(_jnp.zeros_like, weights))
        (loss, grad_w), grad_x = _jax.lax.scan(body, init, (per_example, given["loss_target"]))
    with _jax.named_scope("update"):
        delta_w, new_m, new_v = {}, {}, {}
        for n in TWIN_WEIGHTS:
            delta_w[n], new_m[n], new_v[n] = _adamw(weights[n], grad_w[n], given["m_" + n], given["v_" + n])
    return (loss, grad_x, *[grad_w[n] for n in TWIN_WEIGHTS], *[delta_w[n] for n in TWIN_WEIGHTS],
            *[new_m[n] for n in TWIN_WEIGHTS], *[new_v[n] for n in TWIN_WEIGHTS])
```

```python
import functools
import math

import jax
import jax.numpy as jnp
from jax import lax
from jax.experimental import pallas as pl
from jax.experimental.pallas import tpu as pltpu

F32 = jnp.float32
BF16 = jnp.bfloat16

D = 1024
DFF = 2816
DR = 512
HS = 64
D_SHIFT = 1792
D_IN = 2816
CW = 31
RMS_EPS = 1e-6
GN_EPS = 64e-5
LN_EPS = 1e-5
DECAY_SCALE = math.exp(-0.5)
ADAM_LR, ADAM_B1, ADAM_B2, ADAM_EPS, ADAM_WD, ADAM_STEP = 0.001, 0.9, 0.999, 1e-8, 0.01, 10

REC_TILE = 128
VMEM_LIMIT = 56 * 1024 * 1024
MESH = pl.DeviceIdType.MESH
ANY = pl.BlockSpec(memory_space=pl.ANY)

BIG = (("ffn1_w_gu", 2 * DFF, True), ("ffn1_w_down", DFF, False), ("w_in", D_IN, True),
       ("w_out", D, False), ("ffn2_w_gu", 2 * DFF, True), ("ffn2_w_down", DFF, False))
EARLY = ("ffn1_w_gu", "ffn1_w_down", "w_in")
LATE = ("w_out", "ffn2_w_gu", "ffn2_w_down")
PACK_W = 384
SMALL = (("ffn1_norm_pre", D), ("ffn1_norm_post", D), ("mix_norm_pre", D), ("mix_norm_post", D),
         ("shift_mu", D_SHIFT), ("w0", DR), ("a0", DR), ("k_k", DR), ("k_a", DR), ("r_k", DR),
         ("gn_w", DR), ("gn_b", DR), ("conv_b", DR), ("conv_ln_w", DR), ("conv_ln_b", DR),
         ("ffn2_norm_pre", D), ("ffn2_norm_post", D))
SMALL_N = sum(n for _, n in SMALL)
SMALL_PAD = 8 * 1664


def _params(sem):
    return pltpu.CompilerParams(dimension_semantics=sem, vmem_limit_bytes=VMEM_LIMIT)


def _tile(n, prefs):
    for p in prefs:
        if n % p == 0:
            return p
    return n


def _rows_call(name, fn, rows, consts, row_outs, acc_outs, tm):
    specs, arrs = [], []
    for r in rows:
        if isinstance(r, tuple):
            a, bs, im = r
            specs.append(pl.BlockSpec(bs, im))
        else:
            a = r
            specs.append(pl.BlockSpec((tm, a.shape[1]), lambda i: (i, 0)))
        arrs.append(a)
    t_rows = arrs[0].shape[0]
    for c in consts:
        specs.append(pl.BlockSpec(c.shape, functools.partial(lambda i, n: (0,) * n, n=c.ndim)))
        arrs.append(c)
    n_in, n_o, n_a = len(arrs), len(row_outs), len(acc_outs)

    def kern(*refs):
        i = pl.program_id(0)
        vals = [r[...] for r in refs[:n_in]]
        ro, ao = fn(i, *vals)
        outs = refs[n_in:]
        for k in range(n_o):
            outs[k][...] = ro[k].astype(outs[k].dtype)
        if n_a:
            @pl.when(i == 0)
            def _():
                for k in range(n_a):
                    outs[n_o + k][...] = jnp.zeros(outs[n_o + k].shape, F32)
            for k in range(n_a):
                outs[n_o + k][...] += ao[k]

    out_specs = [pl.BlockSpec((tm, w), lambda i: (i, 0)) for (w, _) in row_outs]
    out_specs += [pl.BlockSpec(s, functools.partial(lambda i, n: (0,) * n, n=len(s))) for s in acc_outs]
    out_shape = [jax.ShapeDtypeStruct((t_rows, w), dt) for (w, dt) in row_outs]
    out_shape += [jax.ShapeDtypeStruct(s, F32) for s in acc_outs]
    return pl.pallas_call(kern, grid=(t_rows // tm,), in_specs=specs, out_specs=out_specs, out_shape=out_shape,
                          name=name, compiler_params=_params(("arbitrary",)))(*arrs)


_DIMS = {"nn": (((1,), (0,)), ((), ())), "nt": (((1,), (1,)), ((), ())), "tn": (((0,), (0,)), ((), ()))}
_LANE_TILES = (1408, 1024, 512, 384, 256, 128)


def _matmul(name, a, b, mode, out_dtype=F32):
    if mode == "nn":
        (m, k), (_, n) = a.shape, b.shape
    elif mode == "nt":
        (m, k), (n, _) = a.shape, b.shape
    else:
        (k, m), (_, n) = a.shape, b.shape
    if mode == "tn":
        tm, tk = _tile(m, _LANE_TILES), _tile(k, (512, 256, 128))
    else:
        tm, tk = _tile(m, (1024, 512, 256, 128)), _tile(k, _LANE_TILES)
    tn = _tile(n, _LANE_TILES)
    nk = k // tk
    assert out_dtype == F32 or nk == 1

    def kern(a_ref, b_ref, o_ref):
        part = lax.dot_general(a_ref[...].astype(BF16), b_ref[...].astype(BF16), _DIMS[mode], preferred_element_type=F32)
        if nk == 1:
            o_ref[...] = part.astype(o_ref.dtype)
        else:
            kk = pl.program_id(2)

            @pl.when(kk == 0)
            def _():
                o_ref[...] = part

            @pl.when(kk > 0)
            def _():
                o_ref[...] += part

    a_spec = pl.BlockSpec((tk, tm), lambda i, j, q: (q, i)) if mode == "tn" else pl.BlockSpec((tm, tk), lambda i, j, q: (i, q))
    b_spec = pl.BlockSpec((tn, tk), lambda i, j, q: (j, q)) if mode == "nt" else pl.BlockSpec((tk, tn), lambda i, j, q: (q, j))
    return pl.pallas_call(
        kern, grid=(m // tm, n // tn, nk), in_specs=[a_spec, b_spec],
        out_specs=pl.BlockSpec((tm, tn), lambda i, j, q: (i, j)),
        out_shape=jax.ShapeDtypeStruct((m, n), out_dtype), name=name,
        compiler_params=_params(("arbitrary", "arbitrary", "arbitrary")))(a, b)


def _rms(x, g):
    return x * lax.rsqrt(jnp.mean(x * x, axis=-1, keepdims=True) + RMS_EPS) * g


def _silu(x):
    return x * jax.nn.sigmoid(x)


def _bones(n):
    r = lax.broadcasted_iota(jnp.int32, (n, n), 0) // HS
    c = lax.broadcasted_iota(jnp.int32, (n, n), 1) // HS
    return (r == c).astype(BF16)


@jax.custom_vjp
def _segsum(x):
    bones = _bones(x.shape[1])
    hi = x.astype(BF16)
    lo = (x - hi.astype(F32)).astype(BF16)
    return jnp.dot(hi, bones, preferred_element_type=F32) + jnp.dot(lo, bones, preferred_element_type=F32)


_segsum.defvjp(lambda x: (_segsum(x), None), lambda _, ct: (_segsum(ct),))


@jax.custom_vjp
def _dot_nt(x, w):
    return lax.dot_general(x.astype(BF16), w.astype(BF16), _DIMS["nt"], preferred_element_type=F32)


def _dot_nt_bwd(res, ct):
    x, w = res
    ctb = ct.astype(BF16)
    dx = lax.dot_general(ctb, w.astype(BF16), _DIMS["nn"], preferred_element_type=F32)
    dw = lax.dot_general(ctb, x.astype(BF16), _DIMS["tn"], preferred_element_type=F32)
    return dx, dw


_dot_nt.defvjp(lambda x, w: (_dot_nt(x, w), (x, w)), _dot_nt_bwd)


def _swiglu(gu):
    return _silu(gu[:, :DFF]) * gu[:, DFF:]


def _prep(ps, w0, a0, k_k, k_a, p01, p2):
    r, k, v = ps[:, :DR], ps[:, DR:2 * DR], ps[:, 2 * DR:3 * DR]
    wa, xg = ps[:, 3 * DR:3 * DR + 128], ps[:, 3 * DR + 128:]
    first = lax.broadcasted_iota(jnp.int32, (1, 128), 1) < 64
    d = w0 + _dot_nt(jnp.where(first, jnp.tanh(wa), 0.0), p01)
    decay = jnp.exp(-DECAY_SCALE * jax.nn.sigmoid(d))
    a = jax.nn.sigmoid(a0 + _dot_nt(jnp.where(first, 0.0, wa), p01))
    g = _dot_nt(jax.nn.sigmoid(xg), p2)
    kk = k * k_k
    kk = kk * lax.rsqrt(jnp.maximum(_segsum(kk * kk), 1e-12))
    k2 = k * (1.0 + (a - 1.0) * k_a)
    return r, decay, k2, v, -kk, kk * a, g


def _post(y, r, k, v, g, gn_w, gn_b, r_k):
    mu = _segsum(y) * (1.0 / HS)
    yc = y - mu
    var = _segsum(yc * yc) * (1.0 / HS)
    yn = yc * lax.rsqrt(var + GN_EPS) * gn_w + gn_b
    return (yn + _segsum(r * k * r_k) * v) * g


def _ln_silu(c, w, b):
    mu = jnp.mean(c, axis=-1, keepdims=True)
    var = jnp.mean(jnp.square(c - mu), axis=-1, keepdims=True)
    return _silu((c - mu) * lax.rsqrt(var + LN_EPS) * w + b)


def _colsum(x):
    return jnp.sum(x, axis=0, keepdims=True)


def _ffn_fwd(tag, x, pre, w_gu_t, w_down, tm):
    (h,) = _rows_call(tag + "_norm", lambda i, xv, g: ((_rms(xv, g),), ()), [x], [pre], [(D, BF16)], [], tm)
    gu = _matmul(tag + "_gu", h, w_gu_t, "nt")
    (act,) = _rows_call(tag + "_act", lambda i, v: ((_swiglu(v),), ()), [gu], [], [(DFF, BF16)], [], min(tm, 256))
    f = _matmul(tag + "_down", act, w_down, "nn")
    return h, gu, act, f


def _ffn_bwd(tag, x, pre, post, scale, h, gu, act, f, dxo, w_gu_t, w_down, tm):
    def resid_b(i, fv, dv, g):
        _, vjp = jax.vjp(lambda a, b: scale * _rms(a, b), fv, g)
        df, dg = vjp(dv)
        return (df,), (dg,)
    df, dpost = _rows_call(tag + "_resid_b", resid_b, [f, dxo], [post], [(D, BF16)], [(1, D)], tm)
    dact = _matmul(tag + "_dact", df, w_down, "nt")
    dw_down = _matmul(tag + "_dwdown", act, df, "tn")

    def act_b(i, guv, dv):
        _, vjp = jax.vjp(_swiglu, guv)
        return (vjp(dv)[0],), ()
    (dgu,) = _rows_call(tag + "_act_b", act_b, [gu, dact], [], [(2 * DFF, BF16)], [], min(tm, 256))
    dh = _matmul(tag + "_dh", dgu, w_gu_t, "nn")
    dw_gu_t = _matmul(tag + "_dwgu", dgu, h, "tn")

    def norm_b(i, xv, dhv, dv, g):
        _, vjp = jax.vjp(_rms, xv, g)
        dx, dg = vjp(dhv)
        return (dx + dv,), (dg,)
    dx, dpre = _rows_call(tag + "_norm_b", norm_b, [x, dh, dxo], [pre], [(D, F32)], [(1, D)], tm)
    return dx, dpre, dpost, dw_gu_t, dw_down


def _pair_bcast(cols, first):
    return jnp.concatenate([jnp.where(first, cols[2 * p], cols[2 * p + 1]) for p in range(4)], axis=1)


def _head_sums(x, first):
    cols = []
    for p in range(4):
        xp = x[:, 128 * p:128 * (p + 1)]
        cols.append(jnp.sum(jnp.where(first, xp, 0.0), axis=1, keepdims=True))
        cols.append(jnp.sum(jnp.where(first, 0.0, xp), axis=1, keepdims=True))
    return cols


def _split16(x8):
    hi = x8.astype(BF16).astype(F32)
    return jnp.concatenate([hi, x8 - hi], axis=0).astype(BF16)


def _cols8(x8, e16):
    return lax.dot_general(_split16(x8), e16, _DIMS["tn"], preferred_element_type=F32)


def _pair_rows(c, first):
    return jnp.concatenate([jnp.where(first, c[128 * p:128 * p + HS], c[128 * p + HS:128 * (p + 1)]) for p in range(4)], axis=1)


def _rows8(prod, bones, dmask):
    x = prod.astype(BF16)
    full = jnp.concatenate([jnp.dot(x[:, 256 * q:256 * (q + 1)], bones, preferred_element_type=F32) for q in range(2)], axis=1)
    return jnp.concatenate([_colsum(full[HS * j:HS * (j + 1)] * dmask) for j in range(8)], axis=0)


def _rec_step(s, wr, zr, br, kr, vc, first):
    u = _pair_bcast(_head_sums(s * zr, first), first)
    return s * wr + u * br + vc * kr, u


def _rec_consts():
    e16 = (lax.broadcasted_iota(jnp.int32, (16, 1024), 0) % 8 == lax.broadcasted_iota(jnp.int32, (16, 1024), 1) // 128)
    dmask = lax.broadcasted_iota(jnp.int32, (HS, DR), 0) == lax.broadcasted_iota(jnp.int32, (HS, DR), 1) % HS
    return e16.astype(BF16), _bones(256), dmask.astype(F32)


def _const_spec(a):
    return pl.BlockSpec(a.shape, functools.partial(lambda i, n: (0,) * n, n=a.ndim))


def _rec_fwd(r, w, k, z, b, v, shards):
    t_len = r.shape[0]
    nt = t_len // REC_TILE
    ns = len(shards)
    consts = _rec_consts()

    def kern(r_ref, w_ref, k_ref, z_ref, b_ref, v_ref, e_ref, bones_ref, dm_ref, *rest):
        y_ref, ck_ref, states, u_ref = rest[ns:ns + 4]
        s_ref, prod = rest[2 * ns + 4:2 * ns + 6]
        start, middle, finish = _gather_plan(rest[:ns], rest[ns + 4:2 * ns + 4], *rest[2 * ns + 6:])
        i = pl.program_id(0)

        @pl.when(i == 0)
        def _():
            s_ref[...] = jnp.zeros(s_ref.shape, F32)
            start()

        pl.when(i == nt // 2)(middle)
        ck_ref[0] = s_ref[...]
        first = lax.broadcasted_iota(jnp.int32, (1, 128), 1) < HS

        def group(g8, s):
            base = pl.multiple_of(g8 * 8, 8)
            r8, w8, k8, z8, b8, v8 = (ref[pl.ds(base, 8), :] for ref in (r_ref, w_ref, k_ref, z_ref, b_ref, v_ref))
            vcols = _cols8(v8, e_ref[...])
            urows = []
            for j in range(8):
                vc = _pair_rows(vcols[:, 128 * j:128 * (j + 1)], first)
                s, u = _rec_step(s, w8[j:j + 1], z8[j:j + 1], b8[j:j + 1], k8[j:j + 1], vc, first)
                states[base + j] = s
                urows.append(_colsum(u * dm_ref[...]))
                prod[HS * j:HS * (j + 1), :] = s * r8[j:j + 1]
            y_ref[pl.ds(base, 8), :] = _rows8(prod[...], bones_ref[...], dm_ref[...])
            u_ref[pl.ds(base, 8), :] = jnp.concatenate(urows, axis=0)
            return s

        s_ref[...] = lax.fori_loop(0, REC_TILE // 8, group, s_ref[...])
        pl.when(i == nt - 1)(finish)

    row = pl.BlockSpec((REC_TILE, DR), lambda i: (i, 0))
    y, ck, states, u, *got = pl.pallas_call(
        kern, grid=(nt,), in_specs=[row] * 6 + [_const_spec(c) for c in consts] + [ANY] * ns,
        out_specs=[row, pl.BlockSpec((1, HS, DR), lambda i: (i, 0, 0)), pl.BlockSpec((REC_TILE, HS, DR), lambda i: (i, 0, 0)), row]
        + [ANY] * ns,
        out_shape=[jax.ShapeDtypeStruct((t_len, DR), F32), jax.ShapeDtypeStruct((nt, HS, DR), F32),
                   jax.ShapeDtypeStruct((t_len, HS, DR), F32), jax.ShapeDtypeStruct((t_len, DR), F32)] + _gathered_shapes(shards),
        scratch_shapes=[pltpu.VMEM((HS, DR), F32), pltpu.VMEM((8 * HS, DR), F32)] + _gather_sems(ns), name="rec_fwd",
        compiler_params=_params(("arbitrary",)))(r, w, k, z, b, v, *consts, *shards)
    return y, (ck, states, u), _fill_own(got, shards)


def _rec_bwd(r, w, k, z, b, v, dy, saved, sums):
    t_len = r.shape[0]
    nt = t_len // REC_TILE
    ns = len(sums)
    consts = _rec_consts()

    def kern(r_ref, w_ref, k_ref, z_ref, b_ref, v_ref, dy_ref, u_ref, ck_ref, states, e_ref, bones_ref, dm_ref, *rest):
        dr_ref, dw_ref, dk_ref, dz_ref, db_ref, dv_ref = rest[ns:ns + 6]
        ds_ref, prod = rest[2 * ns + 6:2 * ns + 8]
        start, finish = _owners_plan(rest[:ns], rest[ns + 6:2 * ns + 6], *rest[2 * ns + 8:])
        i = pl.program_id(0)

        @pl.when(i == 0)
        def _():
            ds_ref[...] = jnp.zeros(ds_ref.shape, F32)
            start()

        first = lax.broadcasted_iota(jnp.int32, (1, 128), 1) < HS

        def bgroup(gg, ds):
            base = pl.multiple_of((REC_TILE // 8 - 1 - gg) * 8, 8)
            r8, w8, k8, z8, b8, v8, dy8, u8 = (ref[pl.ds(base, 8), :]
                                               for ref in (r_ref, w_ref, k_ref, z_ref, b_ref, v_ref, dy_ref, u_ref))
            vcols = _cols8(v8, e_ref[...])
            dycols = _cols8(dy8, e_ref[...])
            ucols = _cols8(u8, e_ref[...])
            before = jnp.where(base == 0, ck_ref[0], states[jnp.maximum(base - 1, 0)])
            rows = {n: [None] * 8 for n in ("dr", "dw", "dk", "dz", "db")}
            for j in range(7, -1, -1):
                t = base + j
                rr, wr, kr, zr, br = (x[j:j + 1] for x in (r8, w8, k8, z8, b8))
                s_prev, s_t = (states[t - 1] if j else before), states[t]
                dyc = _pair_rows(dycols[:, 128 * j:128 * (j + 1)], first)
                vc = _pair_rows(vcols[:, 128 * j:128 * (j + 1)], first)
                ds = ds + dyc * rr
                rows["dr"][j] = _colsum(s_t * dyc)
                rows["dw"][j] = _colsum(ds * s_prev)
                du = _pair_bcast(_head_sums(ds * br, first), first)
                rows["db"][j] = _colsum(ds * _pair_rows(ucols[:, 128 * j:128 * (j + 1)], first))
                rows["dk"][j] = _colsum(ds * vc)
                prod[HS * j:HS * (j + 1), :] = ds * kr
                rows["dz"][j] = _colsum(s_prev * du)
                ds = ds * wr + du * zr
            for n, ref in (("dr", dr_ref), ("dw", dw_ref), ("dk", dk_ref), ("dz", dz_ref), ("db", db_ref)):
                ref[pl.ds(base, 8), :] = jnp.concatenate(rows[n], axis=0)
            dv_ref[pl.ds(base, 8), :] = _rows8(prod[...], bones_ref[...], dm_ref[...])
            return ds

        ds_ref[...] = lax.fori_loop(0, REC_TILE // 8, bgroup, ds_ref[...])
        pl.when(i == nt - 1)(finish)

    ck, states, u = saved
    row = pl.BlockSpec((REC_TILE, DR), lambda i: (nt - 1 - i, 0))
    outs = pl.pallas_call(
        kern, grid=(nt,),
        in_specs=[row] * 8 + [pl.BlockSpec((1, HS, DR), lambda i: (nt - 1 - i, 0, 0)),
                              pl.BlockSpec((REC_TILE, HS, DR), lambda i: (nt - 1 - i, 0, 0))]
        + [_const_spec(c) for c in consts] + [ANY] * ns,
        out_specs=[row] * 6 + [ANY] * ns, out_shape=[jax.ShapeDtypeStruct((t_len, DR), F32)] * 6 + _owner_shapes(sums),
        scratch_shapes=[pltpu.VMEM((HS, DR), F32), pltpu.VMEM((8 * HS, DR), F32)] + _owner_sems(ns), name="rec_bwd",
        compiler_params=_params(("arbitrary",)))(r, w, k, z, b, v, dy, u, ck, states, *consts, *sums)
    return outs[:6], outs[6:]


def _prev_rows(a, tm, n):
    return (a, (n, a.shape[1]), lambda i: (jnp.maximum(i * (tm // n) - 1, 0), 0))


def _next_rows(a, tm, n):
    last = a.shape[0] // n - 1
    return (a, (n, a.shape[1]), lambda i: (jnp.minimum((i + 1) * (tm // n), last), 0))


def _shifted(x, prev8, i):
    rowid = lax.broadcasted_iota(jnp.int32, x.shape, 0)
    before = jnp.where(i == 0, 0.0, prev8[7:8, :])
    return jnp.where(rowid == 0, before, pltpu.roll(x, 1, 0))


def _shift_fwd(p, mu, tm):
    def fn(i, pv, prev8, muv):
        x = pv[:, :D_SHIFT]
        return (x + (_shifted(x, prev8[:, :D_SHIFT], i) - x) * muv,), ()
    return _rows_call("shift", fn, [p, _prev_rows(p, tm, 8)], [mu], [(D_SHIFT, F32)], [], tm)[0]


def _shift_bwd(dps, p, dpc, mu, tm):
    n_tiles = p.shape[0] // tm

    def fn(i, dv, next8, pv, prev8, dpcv, muv):
        x = pv[:, :D_SHIFT]
        xs = _shifted(x, prev8[:, :D_SHIFT], i)
        rowid = lax.broadcasted_iota(jnp.int32, dv.shape, 0)
        after = jnp.where(i == n_tiles - 1, 0.0, next8[0:1, :])
        dnext = jnp.where(rowid == tm - 1, after, pltpu.roll(dv, tm - 1, 0))
        dp_s = dv * (1.0 - muv) + dnext * muv
        return (jnp.concatenate([dp_s.astype(BF16), dpcv], axis=1),), (_colsum(dv * (xs - x)),)
    return _rows_call("shift_b", fn, [dps, _next_rows(dps, tm, 8), p, _prev_rows(p, tm, 8), dpc], [mu],
                      [(D_IN, BF16)], [(1, D_SHIFT)], tm)


def _glu(pc):
    return pc[:, :DR] * jax.nn.sigmoid(pc[:, DR:])


def _conv_fwd(p, dw32, cb, lw, lb, tm):
    t_len = p.shape[0]

    def kern(p_ref, ph_ref, dw_ref, cb_ref, lw_ref, lb_ref, glu_ref, c_ref, ob_ref, ext):
        i = pl.program_id(0)
        glu = _glu(p_ref[:, D_SHIFT:])
        ext[0:32, :] = jnp.where(i == 0, 0.0, _glu(ph_ref[:, D_SHIFT:]))
        ext[32:, :] = glu
        acc = jnp.zeros((tm, DR), F32)
        for j in range(CW):
            acc = acc + ext[2 + j:2 + j + tm, :] * dw_ref[j:j + 1, :]
        c = acc + cb_ref[...]
        glu_ref[...] = glu
        c_ref[...] = c
        ob_ref[...] = _ln_silu(c, lw_ref[...], lb_ref[...]).astype(BF16)

    tile = lambda w: pl.BlockSpec((tm, w), lambda i: (i, 0))
    const = lambda a: pl.BlockSpec(a.shape, lambda i: (0, 0))
    halo = pl.BlockSpec((32, D_IN), lambda i: (jnp.maximum(i * (tm // 32) - 1, 0), 0))
    return pl.pallas_call(
        kern, grid=(t_len // tm,), in_specs=[tile(D_IN), halo, const(dw32), const(cb), const(lw), const(lb)],
        out_specs=[tile(DR)] * 3,
        out_shape=[jax.ShapeDtypeStruct((t_len, DR), F32)] * 2 + [jax.ShapeDtypeStruct((t_len, DR), BF16)],
        scratch_shapes=[pltpu.VMEM((tm + 32, DR), F32)], name="conv_fwd",
        compiler_params=_params(("arbitrary",)))(p, p, dw32, cb, lw, lb)


def _conv_bwd(dc, glu, p, dw32, tm):
    t_len = p.shape[0]
    n_tiles = t_len // tm

    def kern(dc_ref, dcn_ref, glu_ref, gluh_ref, p_ref, dw_ref, dpc_ref, ddw_ref, ext_d, ext_g):
        i = pl.program_id(0)

        @pl.when(i == 0)
        def _():
            ddw_ref[...] = jnp.zeros(ddw_ref.shape, F32)

        dcv = dc_ref[...]
        ext_d[0:tm, :] = dcv
        ext_d[tm:, :] = jnp.where(i == n_tiles - 1, 0.0, dcn_ref[...])
        ext_g[0:32, :] = jnp.where(i == 0, 0.0, gluh_ref[...])
        ext_g[32:, :] = glu_ref[...]
        dglu = jnp.zeros((tm, DR), F32)
        for j in range(CW):
            dglu = dglu + ext_d[30 - j:30 - j + tm, :] * dw_ref[j:j + 1, :]
            ddw_ref[j:j + 1, :] += _colsum(dcv * ext_g[2 + j:2 + j + tm, :])
        pc = p_ref[:, D_SHIFT:]
        sg = jax.nn.sigmoid(pc[:, DR:])
        dpc_ref[...] = jnp.concatenate([dglu * sg, dglu * pc[:, :DR] * sg * (1.0 - sg)], axis=1).astype(BF16)

    tile = lambda w: pl.BlockSpec((tm, w), lambda i: (i, 0))
    nxt = pl.BlockSpec((32, DR), lambda i: (jnp.minimum((i + 1) * (tm // 32), t_len // 32 - 1), 0))
    prv = pl.BlockSpec((32, DR), lambda i: (jnp.maximum(i * (tm // 32) - 1, 0), 0))
    return pl.pallas_call(
        kern, grid=(n_tiles,), in_specs=[tile(DR), nxt, tile(DR), prv, tile(D_IN), pl.BlockSpec((32, DR), lambda i: (0, 0))],
        out_specs=[tile(D), pl.BlockSpec((32, DR), lambda i: (0, 0))],
        out_shape=[jax.ShapeDtypeStruct((t_len, D), BF16), jax.ShapeDtypeStruct((32, DR), F32)],
        scratch_shapes=[pltpu.VMEM((tm + 32, DR), F32)] * 2, name="conv_bwd",
        compiler_params=_params(("arbitrary",)))(dc, dc, glu, glu, p, dw32)


def _place():
    x, y, c = lax.axis_index("x"), lax.axis_index("y"), lax.axis_index("c")
    chips = [(1 - x, y), (x, 1 - y), (1 - x, 1 - y)]
    return x, y, c, chips


def _gather_weights(shards):
    n = len(shards)

    def body(*refs):
        start, middle, finish = _gather_plan(refs[:n], refs[n:2 * n], *refs[2 * n:])
        start()
        middle()
        finish()

    got = pl.pallas_call(body, in_specs=[ANY] * n, out_specs=[ANY] * n, out_shape=_gathered_shapes(shards),
                         scratch_shapes=_gather_sems(n), name="gather_weights")(*shards)
    return _fill_own(got, shards)


def _gathered_shapes(shards):
    return [jax.ShapeDtypeStruct((4,) + s.shape, s.dtype) for s in shards]


def _gather_sems(n):
    return [pltpu.SemaphoreType.DMA((6 * n,)), pltpu.SemaphoreType.DMA((6 * n,))]


def _fill_own(got, shards):
    me = 2 * lax.axis_index("x") + lax.axis_index("y")
    return [lax.dynamic_update_slice(g, s[None], (me, 0, 0, 0)) for g, s in zip(got, shards)]


def _gather_plan(src, dst, send, recv):
    n = len(src)
    x, y, c, chips = _place()
    me, sib = 2 * x + y, (x, y, 1 - c)

    def rcopy(k, sem, s_ref, d_ref, to):
        return pltpu.make_async_remote_copy(src_ref=s_ref, dst_ref=d_ref, send_sem=send.at[6 * k + sem],
                                            recv_sem=recv.at[6 * k + sem], device_id=to, device_id_type=MESH)

    def landed(k, m, half):
        return dst[k].at[2 * chips[m][0] + chips[m][1], half]

    first = [rcopy(k, m, src[k].at[c], dst[k].at[me, c], (*chips[m], c)) for k in range(n) for m in range(3)]
    passed = [rcopy(k, 3 + m, landed(k, m, c), landed(k, m, c), sib) for k in range(n) for m in range(3)]

    def start():
        for cp in first:
            cp.start()

    def middle():
        for k in range(n):
            for m in range(3):
                rcopy(k, m, landed(k, m, c), landed(k, m, c), sib).wait_recv()
                passed[3 * k + m].start()

    def finish():
        for k in range(n):
            for m in range(3):
                rcopy(k, 3 + m, landed(k, m, 1 - c), landed(k, m, 1 - c), sib).wait_recv()
        for cp in first + passed:
            cp.wait_send()

    return start, middle, finish


def _swap_halves(name, give):
    n = len(give)

    def body(*refs):
        src, got = refs[:n], refs[n:2 * n]
        send, recv = refs[2 * n:]
        x, y, c, _ = _place()
        copies = []
        for k in range(n):
            for j in range(4):
                copies.append(pltpu.make_async_remote_copy(
                    src_ref=src[k].at[j], dst_ref=got[k].at[j], send_sem=send.at[4 * k + j], recv_sem=recv.at[4 * k + j],
                    device_id=(x, y, 1 - c), device_id_type=MESH))
                copies[-1].start()
        for cp in copies:
            cp.wait()

    out_shape = [jax.ShapeDtypeStruct(g.shape, g.dtype) for g in give]
    return pl.pallas_call(body, in_specs=[ANY] * n, out_specs=[ANY] * n, out_shape=out_shape,
                          scratch_shapes=[pltpu.SemaphoreType.DMA((4 * n,)), pltpu.SemaphoreType.DMA((4 * n,))],
                          name=name)(*give)


def _to_owners(sums):
    n = len(sums)

    def body(*refs):
        start, finish = _owners_plan(refs[:n], refs[n:2 * n], *refs[2 * n:])
        start()
        finish()

    return pl.pallas_call(body, in_specs=[ANY] * n, out_specs=[ANY] * n, out_shape=_owner_shapes(sums),
                          scratch_shapes=_owner_sems(n), name="to_owners")(*sums)


def _owner_shapes(sums):
    return [jax.ShapeDtypeStruct((3,) + s.shape[1:], s.dtype) for s in sums]


def _owner_sems(n):
    return [pltpu.SemaphoreType.DMA((3 * n,)), pltpu.SemaphoreType.DMA((3 * n,))]


def _owners_plan(src, dst, send, recv):
    x, y, c, chips = _place()
    copies = [pltpu.make_async_remote_copy(
        src_ref=src[k].at[2 * chip[0] + chip[1]], dst_ref=dst[k].at[m], send_sem=send.at[3 * k + m],
        recv_sem=recv.at[3 * k + m], device_id=(*chip, c), device_id_type=MESH)
        for k in range(len(src)) for m, chip in enumerate(chips)]

    def start():
        for cp in copies:
            cp.start()

    def finish():
        for cp in copies:
            cp.wait()

    return start, finish


def _join_halves(halves):
    n = len(halves)

    def body(*refs):
        src, dst = refs[:n], refs[n:2 * n]
        send, recv = refs[2 * n:]
        x, y, c, _ = _place()
        copies = []
        for k in range(n):
            copies.append(pltpu.make_async_remote_copy(src_ref=src[k], dst_ref=dst[k], send_sem=send.at[k],
                                                       recv_sem=recv.at[k], device_id=(x, y, 1 - c), device_id_type=MESH))
            copies[-1].start()
        for cp in copies:
            cp.wait()

    out_shape = [jax.ShapeDtypeStruct(h.shape, h.dtype) for h in halves]
    return pl.pallas_call(body, in_specs=[ANY] * n, out_specs=[ANY] * n, out_shape=out_shape,
                          scratch_shapes=[pltpu.SemaphoreType.DMA((n,)), pltpu.SemaphoreType.DMA((n,))],
                          name="join_halves")(*halves)


def _allreduce_small(v):
    rows, n = v.shape

    def body(v_ref, o_ref, buf, send, recv):
        x, y, c, _ = _place()
        me = 4 * x + 2 * y + c
        buf[me] = v_ref[...]
        copies = []
        for d in range(1, 8):
            peer = (x ^ (d >> 2), y ^ ((d >> 1) & 1), c ^ (d & 1))
            cp = pltpu.make_async_remote_copy(src_ref=v_ref, dst_ref=buf.at[me], send_sem=send.at[d], recv_sem=recv.at[d],
                                              device_id=peer, device_id_type=MESH)
            cp.start()
            copies.append(cp)
        for d in range(1, 8):
            peer = 4 * (x ^ (d >> 2)) + 2 * (y ^ ((d >> 1) & 1)) + (c ^ (d & 1))
            pltpu.make_async_remote_copy(src_ref=v_ref, dst_ref=buf.at[peer], send_sem=send.at[d], recv_sem=recv.at[d],
                                         device_id=(x, y, c), device_id_type=MESH).wait_recv()
        for cp in copies:
            cp.wait_send()
        acc = buf[0]
        for d in range(1, 8):
            acc = acc + buf[d]
        o_ref[...] = acc

    vm = pl.BlockSpec(memory_space=pltpu.VMEM)
    return pl.pallas_call(body, in_specs=[vm], out_specs=vm, out_shape=jax.ShapeDtypeStruct((rows, n), F32),
                          scratch_shapes=[pltpu.VMEM((8, rows, n), F32), pltpu.SemaphoreType.DMA((8,)),
                                          pltpu.SemaphoreType.DMA((8,))], name="allreduce_small")(v)


def _add_call(name, parts, tm, out_dtype=F32):
    def fn(i, *vals):
        acc = vals[0].astype(F32)
        for v in vals[1:]:
            acc = acc + v.astype(F32)
        return (acc,), ()
    return _rows_call(name, fn, list(parts), [], [(parts[0].shape[1], out_dtype)], [], tm)[0]


def _adamw(name, w, g, m, v):
    c1 = 1.0 / (1.0 - ADAM_B1 ** ADAM_STEP)
    c2 = 1.0 / (1.0 - ADAM_B2 ** ADAM_STEP)

    def fn(i, wv, gv, mv, vv):
        m2 = ADAM_B1 * mv + (1.0 - ADAM_B1) * gv
        v2 = ADAM_B2 * vv + (1.0 - ADAM_B2) * jnp.square(gv)
        delta = -ADAM_LR * ((m2 * c1) / (jnp.sqrt(v2 * c2) + ADAM_EPS) + ADAM_WD * wv)
        return (delta, m2, v2), ()
    cols = w.shape[1]
    tm = _tile(w.shape[0], (256, 176, 128, 64, 8))
    return _rows_call(name, fn, [w, g, m, v], [], [(cols, F32)] * 3, [], tm)


def _canon(name, a, transposed):
    return a[0].T if transposed else a[0]


def _pack_sharded(w_up, a_up, g_up, conv_dw):
    parts = [w_up[0].T, a_up[0].T, g_up[0].T, conv_dw[0].T]
    used = sum(p.shape[1] for p in parts)
    return jnp.concatenate(parts + [jnp.zeros((parts[0].shape[0], PACK_W - used), F32)], axis=1)


def _unpack_sharded(a):
    return [a[:, 0:64].T[None], a[:, 64:128].T[None], a[:, 128:256].T[None], a[:, 256:256 + CW].T[None]]


def _pack_small(vals):
    flat = jnp.concatenate([v.reshape(-1) for v in vals] + [jnp.zeros((SMALL_PAD - SMALL_N,), F32)])
    return flat.reshape(8, SMALL_PAD // 8)


def _unpack_small(a, like):
    flat, out, off = a.reshape(-1), [], 0
    for (_, n), ref in zip(SMALL, like):
        out.append(flat[off:off + n].reshape(ref.shape))
        off += n
    return out


def kernel(x, ffn1_norm_pre, ffn1_norm_post, ffn1_w_gu, ffn1_w_down, mix_norm_pre, mix_norm_post, w_in, shift_mu, w_up, w0, a_up, a0, g_up, k_k, k_a, r_k, gn_w, gn_b, conv_dw, conv_b, conv_ln_w, conv_ln_b, w_out, ffn2_norm_pre, ffn2_norm_post, ffn2_w_gu, ffn2_w_down, loss_target, m_ffn1_norm_pre, m_ffn1_norm_post, m_ffn1_w_gu, m_ffn1_w_down, m_mix_norm_pre, m_mix_norm_post, m_w_in, m_shift_mu, m_w_up, m_w0, m_a_up, m_a0, m_g_up, m_k_k, m_k_a, m_r_k, m_gn_w, m_gn_b, m_conv_dw, m_conv_b, m_conv_ln_w, m_conv_ln_b, m_w_out, m_ffn2_norm_pre, m_ffn2_norm_post, m_ffn2_w_gu, m_ffn2_w_down, v_ffn1_norm_pre, v_ffn1_norm_post, v_ffn1_w_gu, v_ffn1_w_down, v_mix_norm_pre, v_mix_norm_post, v_w_in, v_shift_mu, v_w_up, v_w0, v_a_up, v_a0, v_g_up, v_k_k, v_k_a, v_r_k, v_gn_w, v_gn_b, v_conv_dw, v_conv_b, v_conv_ln_w, v_conv_ln_b, v_w_out, v_ffn2_norm_pre, v_ffn2_norm_post, v_ffn2_w_gu, v_ffn2_w_down):
    w = dict(ffn1_norm_pre=ffn1_norm_pre, ffn1_norm_post=ffn1_norm_post, ffn1_w_gu=ffn1_w_gu, ffn1_w_down=ffn1_w_down, mix_norm_pre=mix_norm_pre, mix_norm_post=mix_norm_post, w_in=w_in, shift_mu=shift_mu, w_up=w_up, w0=w0, a_up=a_up, a0=a0, g_up=g_up, k_k=k_k, k_a=k_a, r_k=r_k, gn_w=gn_w, gn_b=gn_b, conv_dw=conv_dw, conv_b=conv_b, conv_ln_w=conv_ln_w, conv_ln_b=conv_ln_b, w_out=w_out, ffn2_norm_pre=ffn2_norm_pre, ffn2_norm_post=ffn2_norm_post, ffn2_w_gu=ffn2_w_gu, ffn2_w_down=ffn2_w_down)
    mom = dict(ffn1_norm_pre=m_ffn1_norm_pre, ffn1_norm_post=m_ffn1_norm_post, ffn1_w_gu=m_ffn1_w_gu, ffn1_w_down=m_ffn1_w_down, mix_norm_pre=m_mix_norm_pre, mix_norm_post=m_mix_norm_post, w_in=m_w_in, shift_mu=m_shift_mu, w_up=m_w_up, w0=m_w0, a_up=m_a_up, a0=m_a0, g_up=m_g_up, k_k=m_k_k, k_a=m_k_a, r_k=m_r_k, gn_w=m_gn_w, gn_b=m_gn_b, conv_dw=m_conv_dw, conv_b=m_conv_b, conv_ln_w=m_conv_ln_w, conv_ln_b=m_conv_ln_b, w_out=m_w_out, ffn2_norm_pre=m_ffn2_norm_pre, ffn2_norm_post=m_ffn2_norm_post, ffn2_w_gu=m_ffn2_w_gu, ffn2_w_down=m_ffn2_w_down)
    var = dict(ffn1_norm_pre=v_ffn1_norm_pre, ffn1_norm_post=v_ffn1_norm_post, ffn1_w_gu=v_ffn1_w_gu, ffn1_w_down=v_ffn1_w_down, mix_norm_pre=v_mix_norm_pre, mix_norm_post=v_mix_norm_post, w_in=v_w_in, shift_mu=v_shift_mu, w_up=v_w_up, w0=v_w0, a_up=v_a_up, a0=v_a0, g_up=v_g_up, k_k=v_k_k, k_a=v_k_a, r_k=v_r_k, gn_w=v_gn_w, gn_b=v_gn_b, conv_dw=v_conv_dw, conv_b=v_conv_b, conv_ln_w=v_conv_ln_w, conv_ln_b=v_conv_ln_b, w_out=v_w_out, ffn2_norm_pre=v_ffn2_norm_pre, ffn2_norm_post=v_ffn2_norm_post, ffn2_w_gu=v_ffn2_w_gu, ffn2_w_down=v_ffn2_w_down)
    names = list(w)

    x0 = x[0]
    tgt = loss_target[0]
    t_len = x0.shape[0]
    tm = _tile(t_len, (512, 256, 128))

    half = {n: rows // 8 for n, rows, _ in BIG}
    pack_local = _pack_sharded(w["w_up"], w["a_up"], w["g_up"], w["conv_dw"])
    shard = {n: _canon(n, w[n], tr).astype(BF16).reshape(2, half[n], D) for n, _, tr in BIG}
    rows_of = {n: rows for n, rows, _ in BIG}
    gathered = _gather_weights([shard[n] for n in EARLY] + [pack_local.reshape(2, 64, PACK_W)])
    full = {n: g.reshape(rows_of[n], D) for n, g in zip(EARLY, gathered)}
    pack = gathered[-1].reshape(DR, PACK_W)
    p01, p2 = pack[:, 0:128], pack[:, 128:256]
    dw32 = jnp.concatenate([pack[:, 256:256 + CW].T, jnp.zeros((1, DR), F32)], axis=0)

    h1, gu1, act1, f1 = _ffn_fwd("ffn1", x0, w["ffn1_norm_pre"], full["ffn1_w_gu"], full["ffn1_w_down"], tm)
    (x1,) = _rows_call("ffn1_resid", lambda i, xv, fv, g: ((xv + 0.5 * _rms(fv, g),), ()), [x0, f1],
                       [w["ffn1_norm_post"]], [(D, F32)], [], tm)
    (hm,) = _rows_call("mix_norm", lambda i, xv, g: ((_rms(xv, g),), ()), [x1], [w["mix_norm_pre"]], [(D, BF16)], [], tm)
    p = _matmul("mix_in", hm, full["w_in"], "nt")
    ps = _shift_fwd(p, w["shift_mu"], tm)
    rkc = w["r_k"].reshape(1, DR)
    prep_consts = [w["w0"], w["a0"], w["k_k"], w["k_a"], p01, p2]
    r_, dec, k2, v_, z_, b_, g_ = _rows_call(
        "prep", lambda i, psv, *cs: (_prep(psv, *cs), ()), [ps], prep_consts, [(DR, F32)] * 7, [], min(tm, 256))
    y, ck, late = _rec_fwd(r_, dec, k2, z_, b_, v_, [shard[n] for n in LATE])
    full.update({n: g.reshape(rows_of[n], D) for n, g in zip(LATE, late)})
    glu, cpre, ob = _conv_fwd(p, dw32, w["conv_b"], w["conv_ln_w"], w["conv_ln_b"], tm)
    post_consts = [w["gn_w"], w["gn_b"], rkc]
    (o,) = _rows_call(
        "post", lambda i, yv, rv, kv, vv, gv, obv, *cs: ((jnp.concatenate([_post(yv, rv, kv, vv, gv, *cs).astype(BF16), obv], axis=1),), ()),
        [y, r_, k2, v_, g_, ob], post_consts, [(D, BF16)], [], min(tm, 256))
    mo = _matmul("mix_out", o, full["w_out"], "nn")
    (x2,) = _rows_call("mix_resid", lambda i, xv, fv, g: ((xv + _rms(fv, g),), ()), [x1, mo],
                       [w["mix_norm_post"]], [(D, F32)], [], tm)
    h2, gu2, act2, f2 = _ffn_fwd("ffn2", x2, w["ffn2_norm_pre"], full["ffn2_w_gu"], full["ffn2_w_down"], tm)

    def loss_fn(i, xv, fv, tv, g):
        err = xv + 0.5 * _rms(fv, g) - tv
        part = 0.5 * jnp.sum(jnp.mean(err * err, axis=-1, keepdims=True), axis=0, keepdims=True)
        return (err * (1.0 / D),), (jnp.broadcast_to(part, (8, 128)),)
    dx3, loss_part = _rows_call("loss", loss_fn, [x2, f2, tgt], [w["ffn2_norm_post"]], [(D, F32)], [(8, 128)], tm)
    loss = lax.psum(loss_part[0, 0], ("x", "y", "c"))

    g_small = {}
    dx2, g_small["ffn2_norm_pre"], g_small["ffn2_norm_post"], dgu2_t, dwd2 = _ffn_bwd(
        "ffn2", x2, w["ffn2_norm_pre"], w["ffn2_norm_post"], 0.5, h2, gu2, act2, f2, dx3, full["ffn2_w_gu"], full["ffn2_w_down"], tm)

    def mix_resid_b(i, fv, dv, g):
        _, vjp = jax.vjp(_rms, fv, g)
        df, dg = vjp(dv)
        return (df,), (dg,)
    dmo, g_small["mix_norm_post"] = _rows_call("mix_resid_b", mix_resid_b, [mo, dx2], [w["mix_norm_post"]],
                                               [(D, BF16)], [(1, D)], tm)
    do = _matmul("mix_do", dmo, full["w_out"], "nt")
    dw_out = _matmul("mix_dwout", o, dmo, "tn")

    def post_b(i, yv, rv, kv, vv, gv, dov, *cs):
        _, vjp = jax.vjp(_post, yv, rv, kv, vv, gv, *cs)
        dy, dr, dk, dv, dg, dgw, dgb, drk = vjp(dov[:, :DR])
        return (dy, dr, dk, dv, dg), (_colsum(dgw), _colsum(dgb), _colsum(drk))
    dy, dr1, dk1, dv1, dg, g_small["gn_w"], g_small["gn_b"], g_small["r_k"] = _rows_call(
        "post_b", post_b, [y, r_, k2, v_, g_, do], post_consts, [(DR, F32)] * 5, [(1, DR)] * 3, min(tm, 256))
    my_c = lax.axis_index("c")
    my_chip = 2 * lax.axis_index("x") + lax.axis_index("y")
    g_big = dict(w_out=dw_out, ffn2_w_gu=dgu2_t, ffn2_w_down=dwd2)

    def halves_of(group, which):
        return jnp.concatenate([lax.dynamic_index_in_dim(g_big[n].reshape(4, 2, half[n], D), which, 1, keepdims=False)
                                for n in group], axis=1)

    def pair_sums(tag, group, extra_mine=(), extra_give=()):
        mine = halves_of(group, my_c)
        got, *got_x = _swap_halves("swap_halves_" + tag, [halves_of(group, 1 - my_c)] + list(extra_give))
        rows = mine.shape[1]
        tile = _tile(rows, (352, 592, 16))
        travels = _add_call("add_pair_" + tag, [mine.reshape(4 * rows, D), got.reshape(4 * rows, D)], tile, BF16)
        return mine, got, travels.reshape(4, rows, D), got_x

    def owner_sum(tag, mine, got, parts):
        own = [lax.dynamic_index_in_dim(a, my_chip, 0, keepdims=False) for a in (mine, got)]
        return _add_call("add_chips_" + tag, own + [parts[m] for m in range(3)], _tile(mine.shape[1], (352, 592, 16)))

    mine_l, got_l, sum_l, _ = pair_sums("late", LATE)
    (dr2, ddec, dk2, dz, db, dv2), (parts_l,) = _rec_bwd(r_, dec, k2, z_, b_, v_, dy, ck, [sum_l])

    def prep_b(i, psv, a1, a2, c1, c2, e1, e2, dwv, dzv, dbv, dgv, *cs):
        _, vjp = jax.vjp(_prep, psv, *cs)
        dps, dw0, da0, dkk, dka, dp01, dp2 = vjp((a1 + a2, dwv, c1 + c2, e1 + e2, dzv, dbv, dgv))
        return (dps,), (_colsum(dw0), _colsum(da0), _colsum(dkk), _colsum(dka), dp01, dp2)
    dps, g_small["w0"], g_small["a0"], g_small["k_k"], g_small["k_a"], dp01, dp2 = _rows_call(
        "prep_b", prep_b, [ps, dr1, dr2, dk1, dk2, dv1, dv2, ddec, dz, db, dg], prep_consts, [(D_SHIFT, F32)],
        [(1, DR)] * 4 + [(DR, 128)] * 2, min(tm, 256))

    def convln_b(i, cv, dov, lw, lb):
        _, vjp = jax.vjp(_ln_silu, cv, lw, lb)
        dc, dlw, dlb = vjp(dov[:, DR:])
        return (dc,), (_colsum(dc), _colsum(dlw), _colsum(dlb))
    dc, g_small["conv_b"], g_small["conv_ln_w"], g_small["conv_ln_b"] = _rows_call(
        "convln_b", convln_b, [cpre, do], [w["conv_ln_w"], w["conv_ln_b"]], [(DR, F32)], [(1, DR)] * 3, tm)
    dpc, ddw32 = _conv_bwd(dc, glu, p, dw32, tm)
    dp, g_small["shift_mu"] = _shift_bwd(dps, p, dpc, w["shift_mu"], tm)
    dhm = _matmul("mix_dh", dp, full["w_in"], "nn")
    dw_in_t = _matmul("mix_dwin", dp, hm, "tn")

    def norm_b(i, xv, dhv, dv, g):
        _, vjp = jax.vjp(_rms, xv, g)
        dx, dg_ = vjp(dhv)
        return (dx + dv,), (dg_,)
    dx1, g_small["mix_norm_pre"] = _rows_call("mix_norm_b", norm_b, [x1, dhm, dx2], [w["mix_norm_pre"]],
                                              [(D, F32)], [(1, D)], tm)
    grad_x, g_small["ffn1_norm_pre"], g_small["ffn1_norm_post"], dgu1_t, dwd1 = _ffn_bwd(
        "ffn1", x0, w["ffn1_norm_pre"], w["ffn1_norm_post"], 0.5, h1, gu1, act1, f1, dx1, full["ffn1_w_gu"], full["ffn1_w_down"], tm)

    g_big.update(ffn1_w_gu=dgu1_t, ffn1_w_down=dwd1, w_in=dw_in_t)
    pack_grads = jnp.concatenate([dp01, dp2, ddw32.T, jnp.zeros((DR, PACK_W - 288), F32)], axis=1).reshape(4, 2, 64, PACK_W)
    mine_p, give_p = (lax.dynamic_index_in_dim(pack_grads, which, 1, keepdims=False) for which in (my_c, 1 - my_c))
    mine_e, got_e, sum_e, (got_p,) = pair_sums("early", EARLY, extra_give=[give_p])
    sum_p = _add_call("add_pair_pack", [mine_p.reshape(256, PACK_W), got_p.reshape(256, PACK_W)], 256).reshape(4, 64, PACK_W)
    parts_e, parts_p = _to_owners([sum_e, sum_p])
    fin_e = owner_sum("early", mine_e, got_e, parts_e)
    fin_l = owner_sum("late", mine_l, got_l, parts_l)
    fin_p = _add_call("add_chips_pack", [lax.dynamic_index_in_dim(sum_p, my_chip, 0, keepdims=False)] + [parts_p[m] for m in range(3)], 64)
    fins = [fin_e, fin_l, fin_p]
    red_e, red_l, red_p = [jnp.where(my_c == 0, jnp.stack([f, s]), jnp.stack([s, f])) for f, s in zip(fins, _join_halves(fins))]
    small_sum = _allreduce_small(_pack_small([g_small[n] for n, _ in SMALL]))

    grads, delta, new_m, new_v = {}, {}, {}, {}
    reduced = {}
    for group, red in ((EARLY, red_e), (LATE, red_l)):
        off = 0
        for n in group:
            reduced[n] = red[:, off:off + half[n], :].reshape(rows_of[n] // 4, D)
            off += half[n]
    for n, rows, tr in BIG:
        g = reduced[n]
        g = (g.T if tr else g)[None]
        grads[n] = g
        d_, m_, v2_ = _adamw("adamw_" + n, w[n][0], g[0], mom[n][0], var[n][0])
        delta[n], new_m[n], new_v[n] = d_[None], m_[None], v2_[None]
    sh = ("w_up", "a_up", "g_up", "conv_dw")
    g_pack = red_p.reshape(128, PACK_W)
    d_, m_, v2_ = _adamw("adamw_pack", pack_local, g_pack, _pack_sharded(*[mom[n] for n in sh]), _pack_sharded(*[var[n] for n in sh]))
    for dst, src in ((grads, g_pack), (delta, d_), (new_m, m_), (new_v, v2_)):
        for n, a in zip(sh, _unpack_sharded(src)):
            dst[n] = a
    sm = [n for n, _ in SMALL]
    d_, m_, v2_ = _adamw("adamw_small", _pack_small([w[n] for n in sm]), small_sum, _pack_small([mom[n] for n in sm]),
                         _pack_small([var[n] for n in sm]))
    like = [w[n] for n in sm]
    for dst, src in ((grads, small_sum), (delta, d_), (new_m, m_), (new_v, v2_)):
        for n, a in zip(sm, _unpack_small(src, like)):
            dst[n] = a

    wn = names
    return (loss, grad_x[None], *[grads[n] for n in wn], *[delta[n] for n in wn], *[new_m[n] for n in wn],
            *[new_v[n] for n in wn])
```

```python
import functools
import math

import jax
import jax.numpy as jnp
from jax import lax
from jax.experimental import pallas as pl
from jax.experimental.pallas import tpu as pltpu

F32 = jnp.float32
BF16 = jnp.bfloat16

D = 1024
DFF = 2816
DR = 512
HS = 64
D_SHIFT = 1792
D_IN = 2816
CW = 31
RMS_EPS = 1e-6
GN_EPS = 64e-5
LN_EPS = 1e-5
DECAY_SCALE = math.exp(-0.5)
ADAM_LR, ADAM_B1, ADAM_B2, ADAM_EPS, ADAM_WD, ADAM_STEP = 0.001, 0.9, 0.999, 1e-8, 0.01, 10

REC_TILE = 128
VMEM_LIMIT = 56 * 1024 * 1024
MESH = pl.DeviceIdType.MESH
ANY = pl.BlockSpec(memory_space=pl.ANY)

BIG = (("ffn1_w_gu", 2 * DFF, True), ("ffn1_w_down", DFF, False), ("w_in", D_IN, True),
       ("w_out", D, False), ("ffn2_w_gu", 2 * DFF, True), ("ffn2_w_down", DFF, False))
EARLY = ("ffn1_w_gu", "ffn1_w_down", "w_in")
LATE = ("w_out", "ffn2_w_gu", "ffn2_w_down")
PACK_W = 384
SMALL = (("ffn1_norm_pre", D), ("ffn1_norm_post", D), ("mix_norm_pre", D), ("mix_norm_post", D),
         ("shift_mu", D_SHIFT), ("w0", DR), ("a0", DR), ("k_k", DR), ("k_a", DR), ("r_k", DR),
         ("gn_w", DR), ("gn_b", DR), ("conv_b", DR), ("conv_ln_w", DR), ("conv_ln_b", DR),
         ("ffn2_norm_pre", D), ("ffn2_norm_post", D))
SMALL_N = sum(n for _, n in SMALL)
SMALL_PAD = 8 * 1664


def _params(sem):
    return pltpu.CompilerParams(dimension_semantics=sem, vmem_limit_bytes=VMEM_LIMIT)


def _tile(n, prefs):
    for p in prefs:
        if n % p == 0:
            return p
    return n


def _rows_call(name, fn, rows, consts, row_outs, acc_outs, tm):
    specs, arrs = [], []
    for r in rows:
        if isinstance(r, tuple):
            a, bs, im = r
            specs.append(pl.BlockSpec(bs, im))
        else:
            a = r
            specs.append(pl.BlockSpec((tm, a.shape[1]), lambda i: (i, 0)))
        arrs.append(a)
    t_rows = arrs[0].shape[0]
    for c in consts:
        specs.append(pl.BlockSpec(c.shape, functools.partial(lambda i, n: (0,) * n, n=c.ndim)))
        arrs.append(c)
    n_in, n_o, n_a = len(arrs), len(row_outs), len(acc_outs)

    def kern(*refs):
        i = pl.program_id(0)
        vals = [r[...] for r in refs[:n_in]]
        ro, ao = fn(i, *vals)
        outs = refs[n_in:]
        for k in range(n_o):
            outs[k][...] = ro[k].astype(outs[k].dtype)
        if n_a:
            @pl.when(i == 0)
            def _():
                for k in range(n_a):
                    outs[n_o + k][...] = jnp.zeros(outs[n_o + k].shape, F32)
            for k in range(n_a):
                outs[n_o + k][...] += ao[k]

    out_specs = [pl.BlockSpec((tm, w), lambda i: (i, 0)) for (w, _) in row_outs]
    out_specs += [pl.BlockSpec(s, functools.partial(lambda i, n: (0,) * n, n=len(s))) for s in acc_outs]
    out_shape = [jax.ShapeDtypeStruct((t_rows, w), dt) for (w, dt) in row_outs]
    out_shape += [jax.ShapeDtypeStruct(s, F32) for s in acc_outs]
    return pl.pallas_call(kern, grid=(t_rows // tm,), in_specs=specs, out_specs=out_specs, out_shape=out_shape,
                          name=name, compiler_params=_params(("arbitrary",)))(*arrs)


_DIMS = {"nn": (((1,), (0,)), ((), ())), "nt": (((1,), (1,)), ((), ())), "tn": (((0,), (0,)), ((), ()))}
_LANE_TILES = (1408, 1024, 512, 384, 256, 128)


def _matmul(name, a, b, mode, out_dtype=F32):
    if mode == "nn":
        (m, k), (_, n) = a.shape, b.shape
    elif mode == "nt":
        (m, k), (n, _) = a.shape, b.shape
    else:
        (k, m), (_, n) = a.shape, b.shape
    if mode == "tn":
        tm, tk = _tile(m, _LANE_TILES), _tile(k, (512, 256, 128))
    else:
        tm, tk = _tile(m, (1024, 512, 256, 128)), _tile(k, _LANE_TILES)
    tn = _tile(n, _LANE_TILES)
    nk = k // tk
    assert out_dtype == F32 or nk == 1

    def kern(a_ref, b_ref, o_ref):
        def part():
            return lax.dot_general(a_ref[...].astype(BF16), b_ref[...].astype(BF16), _DIMS[mode], preferred_element_type=F32)

        if nk == 1:
            o_ref[...] = part().astype(o_ref.dtype)
        else:
            @pl.when(pl.program_id(2) == 0)
            def _():
                o_ref[...] = jnp.zeros(o_ref.shape, F32)

            o_ref[...] += part()

    a_spec = pl.BlockSpec((tk, tm), lambda i, j, q: (q, i)) if mode == "tn" else pl.BlockSpec((tm, tk), lambda i, j, q: (i, q))
    b_spec = pl.BlockSpec((tn, tk), lambda i, j, q: (j, q)) if mode == "nt" else pl.BlockSpec((tk, tn), lambda i, j, q: (q, j))
    return pl.pallas_call(
        kern, grid=(m // tm, n // tn, nk), in_specs=[a_spec, b_spec],
        out_specs=pl.BlockSpec((tm, tn), lambda i, j, q: (i, j)),
        out_shape=jax.ShapeDtypeStruct((m, n), out_dtype), name=name,
        compiler_params=_params(("arbitrary", "arbitrary", "arbitrary")))(a, b)


def _rms(x, g):
    return x * lax.rsqrt(jnp.mean(x * x, axis=-1, keepdims=True) + RMS_EPS) * g


def _silu(x):
    return x * jax.nn.sigmoid(x)


def _bones(n):
    r = lax.broadcasted_iota(jnp.int32, (n, n), 0) // HS
    c = lax.broadcasted_iota(jnp.int32, (n, n), 1) // HS
    return (r == c).astype(BF16)


@jax.custom_vjp
def _segsum(x):
    bones = _bones(x.shape[1])
    hi = x.astype(BF16)
    lo = (x - hi.astype(F32)).astype(BF16)
    return jnp.dot(hi, bones, preferred_element_type=F32) + jnp.dot(lo, bones, preferred_element_type=F32)


_segsum.defvjp(lambda x: (_segsum(x), None), lambda _, ct: (_segsum(ct),))


@jax.custom_vjp
def _dot_nt(x, w):
    return lax.dot_general(x.astype(BF16), w.astype(BF16), _DIMS["nt"], preferred_element_type=F32)


def _dot_nt_bwd(res, ct):
    x, w = res
    ctb = ct.astype(BF16)
    dx = lax.dot_general(ctb, w.astype(BF16), _DIMS["nn"], preferred_element_type=F32)
    dw = lax.dot_general(ctb, x.astype(BF16), _DIMS["tn"], preferred_element_type=F32)
    return dx, dw


_dot_nt.defvjp(lambda x, w: (_dot_nt(x, w), (x, w)), _dot_nt_bwd)


def _swiglu(gu):
    return _silu(gu[:, :DFF]) * gu[:, DFF:]


def _prep(ps, w0, a0, k_k, k_a, p01, p2):
    r, k, v = ps[:, :DR], ps[:, DR:2 * DR], ps[:, 2 * DR:3 * DR]
    wa, xg = ps[:, 3 * DR:3 * DR + 128], ps[:, 3 * DR + 128:]
    first = lax.broadcasted_iota(jnp.int32, (1, 128), 1) < 64
    d = w0 + _dot_nt(jnp.where(first, jnp.tanh(wa), 0.0), p01)
    decay = jnp.exp(-DECAY_SCALE * jax.nn.sigmoid(d))
    a = jax.nn.sigmoid(a0 + _dot_nt(jnp.where(first, 0.0, wa), p01))
    g = _dot_nt(jax.nn.sigmoid(xg), p2)
    kk = k * k_k
    kk = kk * lax.rsqrt(jnp.maximum(_segsum(kk * kk), 1e-12))
    k2 = k * (1.0 + (a - 1.0) * k_a)
    return r, decay, k2, v, -kk, kk * a, g


def _post(y, r, k, v, g, gn_w, gn_b, r_k):
    mu = _segsum(y) * (1.0 / HS)
    yc = y - mu
    var = _segsum(yc * yc) * (1.0 / HS)
    yn = yc * lax.rsqrt(var + GN_EPS) * gn_w + gn_b
    return (yn + _segsum(r * k * r_k) * v) * g


def _ln_silu(c, w, b):
    mu = jnp.mean(c, axis=-1, keepdims=True)
    var = jnp.mean(jnp.square(c - mu), axis=-1, keepdims=True)
    return _silu((c - mu) * lax.rsqrt(var + LN_EPS) * w + b)


def _colsum(x):
    return jnp.sum(x, axis=0, keepdims=True)


def _ffn_fwd(tag, x, pre, w_gu_t, w_down, tm):
    (h,) = _rows_call(tag + "_norm", lambda i, xv, g: ((_rms(xv, g),), ()), [x], [pre], [(D, BF16)], [], tm)
    gu = _matmul(tag + "_gu", h, w_gu_t, "nt")
    (act,) = _rows_call(tag + "_act", lambda i, v: ((_swiglu(v),), ()), [gu], [], [(DFF, BF16)], [], min(tm, 256))
    f = _matmul(tag + "_down", act, w_down, "nn")
    return h, gu, act, f


def _ffn_bwd(tag, x, pre, post, scale, h, gu, act, f, dxo, w_gu_t, w_down, tm):
    def resid_b(i, fv, dv, g):
        _, vjp = jax.vjp(lambda a, b: scale * _rms(a, b), fv, g)
        df, dg = vjp(dv)
        return (df,), (dg,)
    df, dpost = _rows_call(tag + "_resid_b", resid_b, [f, dxo], [post], [(D, BF16)], [(1, D)], tm)
    dact = _matmul(tag + "_dact", df, w_down, "nt")
    dw_down = _matmul(tag + "_dwdown", act, df, "tn")

    def act_b(i, guv, dv):
        _, vjp = jax.vjp(_swiglu, guv)
        return (vjp(dv)[0],), ()
    (dgu,) = _rows_call(tag + "_act_b", act_b, [gu, dact], [], [(2 * DFF, BF16)], [], min(tm, 256))
    dh = _matmul(tag + "_dh", dgu, w_gu_t, "nn")
    dw_gu_t = _matmul(tag + "_dwgu", dgu, h, "tn")

    def norm_b(i, xv, dhv, dv, g):
        _, vjp = jax.vjp(_rms, xv, g)
        dx, dg = vjp(dhv)
        return (dx + dv,), (dg,)
    dx, dpre = _rows_call(tag + "_norm_b", norm_b, [x, dh, dxo], [pre], [(D, F32)], [(1, D)], tm)
    return dx, dpre, dpost, dw_gu_t, dw_down


def _pair_bcast(cols, first):
    return jnp.concatenate([jnp.where(first, cols[2 * p], cols[2 * p + 1]) for p in range(4)], axis=1)


def _head_sums(x, first):
    cols = []
    for p in range(4):
        xp = x[:, 128 * p:128 * (p + 1)]
        cols.append(jnp.sum(jnp.where(first, xp, 0.0), axis=1, keepdims=True))
        cols.append(jnp.sum(jnp.where(first, 0.0, xp), axis=1, keepdims=True))
    return cols


def _split16(x8):
    hi = x8.astype(BF16).astype(F32)
    return jnp.concatenate([hi, x8 - hi], axis=0).astype(BF16)


def _cols8(x8, e16):
    return lax.dot_general(_split16(x8), e16, _DIMS["tn"], preferred_element_type=F32)


def _pair_rows(c, first):
    return jnp.concatenate([jnp.where(first, c[128 * p:128 * p + HS], c[128 * p + HS:128 * (p + 1)]) for p in range(4)], axis=1)


def _rows8(prod, bones, dmask):
    x = prod.astype(BF16)
    full = jnp.concatenate([jnp.dot(x[:, 256 * q:256 * (q + 1)], bones, preferred_element_type=F32) for q in range(2)], axis=1)
    return jnp.concatenate([_colsum(full[HS * j:HS * (j + 1)] * dmask) for j in range(8)], axis=0)


def _rec_step(s, wr, zr, br, kr, vc, first):
    u = _pair_bcast(_head_sums(s * zr, first), first)
    return s * wr + u * br + vc * kr, u


def _rec_consts():
    e16 = (lax.broadcasted_iota(jnp.int32, (16, 1024), 0) % 8 == lax.broadcasted_iota(jnp.int32, (16, 1024), 1) // 128)
    dmask = lax.broadcasted_iota(jnp.int32, (HS, DR), 0) == lax.broadcasted_iota(jnp.int32, (HS, DR), 1) % HS
    return e16.astype(BF16), _bones(256), dmask.astype(F32)


def _const_spec(a):
    return pl.BlockSpec(a.shape, functools.partial(lambda i, n: (0,) * n, n=a.ndim))


def _rec_fwd(r, w, k, z, b, v, shards):
    t_len = r.shape[0]
    nt = t_len // REC_TILE
    ns = len(shards)
    consts = _rec_consts()

    def kern(r_ref, w_ref, k_ref, z_ref, b_ref, v_ref, e_ref, bones_ref, dm_ref, *rest):
        y_ref, ck_ref, states, u_ref = rest[ns:ns + 4]
        s_ref, prod = rest[2 * ns + 4:2 * ns + 6]
        start, middle, finish = _gather_plan(rest[:ns], rest[ns + 4:2 * ns + 4], *rest[2 * ns + 6:])
        i = pl.program_id(0)

        @pl.when(i == 0)
        def _():
            s_ref[...] = jnp.zeros(s_ref.shape, F32)
            start()

        pl.when(i == nt // 2)(middle)
        ck_ref[0] = s_ref[...]
        first = lax.broadcasted_iota(jnp.int32, (1, 128), 1) < HS

        def group(g8, s):
            base = pl.multiple_of(g8 * 8, 8)
            r8, w8, k8, z8, b8, v8 = (ref[pl.ds(base, 8), :] for ref in (r_ref, w_ref, k_ref, z_ref, b_ref, v_ref))
            vcols = _cols8(v8, e_ref[...])
            urows = []
            for j in range(8):
                vc = _pair_rows(vcols[:, 128 * j:128 * (j + 1)], first)
                s, u = _rec_step(s, w8[j:j + 1], z8[j:j + 1], b8[j:j + 1], k8[j:j + 1], vc, first)
                states[base + j] = s
                urows.append(_colsum(u * dm_ref[...]))
                prod[HS * j:HS * (j + 1), :] = s * r8[j:j + 1]
            y_ref[pl.ds(base, 8), :] = _rows8(prod[...], bones_ref[...], dm_ref[...])
            u_ref[pl.ds(base, 8), :] = jnp.concatenate(urows, axis=0)
            return s

        s_ref[...] = lax.fori_loop(0, REC_TILE // 8, group, s_ref[...])
        pl.when(i == nt - 1)(finish)

    row = pl.BlockSpec((REC_TILE, DR), lambda i: (i, 0))
    y, ck, states, u, *got = pl.pallas_call(
        kern, grid=(nt,), in_specs=[row] * 6 + [_const_spec(c) for c in consts] + [ANY] * ns,
        out_specs=[row, pl.BlockSpec((1, HS, DR), lambda i: (i, 0, 0)), pl.BlockSpec((REC_TILE, HS, DR), lambda i: (i, 0, 0)), row]
        + [ANY] * ns,
        out_shape=[jax.ShapeDtypeStruct((t_len, DR), F32), jax.ShapeDtypeStruct((nt, HS, DR), F32),
                   jax.ShapeDtypeStruct((t_len, HS, DR), F32), jax.ShapeDtypeStruct((t_len, DR), F32)] + _gathered_shapes(shards),
        scratch_shapes=[pltpu.VMEM((HS, DR), F32), pltpu.VMEM((8 * HS, DR), F32)] + _gather_sems(ns), name="rec_fwd",
        compiler_params=_params(("arbitrary",)))(r, w, k, z, b, v, *consts, *shards)
    return y, (ck, states, u), _fill_own(got, shards)


def _rec_bwd(r, w, k, z, b, v, dy, saved, sums):
    t_len = r.shape[0]
    nt = t_len // REC_TILE
    ns = len(sums)
    consts = _rec_consts()

    def kern(r_ref, w_ref, k_ref, z_ref, b_ref, v_ref, dy_ref, u_ref, ck_ref, states, e_ref, bones_ref, dm_ref, *rest):
        dr_ref, dw_ref, dk_ref, dz_ref, db_ref, dv_ref = rest[ns:ns + 6]
        ds_ref, prod = rest[2 * ns + 6:2 * ns + 8]
        start, finish = _owners_plan(rest[:ns], rest[ns + 6:2 * ns + 6], *rest[2 * ns + 8:])
        i = pl.program_id(0)

        @pl.when(i == 0)
        def _():
            ds_ref[...] = jnp.zeros(ds_ref.shape, F32)
            start()

        first = lax.broadcasted_iota(jnp.int32, (1, 128), 1) < HS

        def bgroup(gg, ds):
            base = pl.multiple_of((REC_TILE // 8 - 1 - gg) * 8, 8)
            r8, w8, k8, z8, b8, v8, dy8, u8 = (ref[pl.ds(base, 8), :]
                                               for ref in (r_ref, w_ref, k_ref, z_ref, b_ref, v_ref, dy_ref, u_ref))
            vcols = _cols8(v8, e_ref[...])
            dycols = _cols8(dy8, e_ref[...])
            ucols = _cols8(u8, e_ref[...])
            before = jnp.where(base == 0, ck_ref[0], states[jnp.maximum(base - 1, 0)])
            rows = {n: [None] * 8 for n in ("dr", "dw", "dk", "dz", "db")}
            for j in range(7, -1, -1):
                t = base + j
                rr, wr, kr, zr, br = (x[j:j + 1] for x in (r8, w8, k8, z8, b8))
                s_prev, s_t = (states[t - 1] if j else before), states[t]
                dyc = _pair_rows(dycols[:, 128 * j:128 * (j + 1)], first)
                vc = _pair_rows(vcols[:, 128 * j:128 * (j + 1)], first)
                ds = ds + dyc * rr
                rows["dr"][j] = _colsum(s_t * dyc)
                rows["dw"][j] = _colsum(ds * s_prev)
                du = _pair_bcast(_head_sums(ds * br, first), first)
                rows["db"][j] = _colsum(ds * _pair_rows(ucols[:, 128 * j:128 * (j + 1)], first))
                rows["dk"][j] = _colsum(ds * vc)
                prod[HS * j:HS * (j + 1), :] = ds * kr
                rows["dz"][j] = _colsum(s_prev * du)
                ds = ds * wr + du * zr
            for n, ref in (("dr", dr_ref), ("dw", dw_ref), ("dk", dk_ref), ("dz", dz_ref), ("db", db_ref)):
                ref[pl.ds(base, 8), :] = jnp.concatenate(rows[n], axis=0)
            dv_ref[pl.ds(base, 8), :] = _rows8(prod[...], bones_ref[...], dm_ref[...])
            return ds

        ds_ref[...] = lax.fori_loop(0, REC_TILE // 8, bgroup, ds_ref[...])
        pl.when(i == nt - 1)(finish)

    ck, states, u = saved
    row = pl.BlockSpec((REC_TILE, DR), lambda i: (nt - 1 - i, 0))
    outs = pl.pallas_call(
        kern, grid=(nt,),
        in_specs=[row] * 8 + [pl.BlockSpec((1, HS, DR), lambda i: (nt - 1 - i, 0, 0)),
                              pl.BlockSpec((REC_TILE, HS, DR), lambda i: (nt - 1 - i, 0, 0))]
        + [_const_spec(c) for c in consts] + [ANY] * ns,
        out_specs=[row] * 6 + [ANY] * ns, out_shape=[jax.ShapeDtypeStruct((t_len, DR), F32)] * 6 + _owner_shapes(sums),
        scratch_shapes=[pltpu.VMEM((HS, DR), F32), pltpu.VMEM((8 * HS, DR), F32)] + _owner_sems(ns), name="rec_bwd",
        compiler_params=_params(("arbitrary",)))(r, w, k, z, b, v, dy, u, ck, states, *consts, *sums)
    return outs[:6], outs[6:]


def _prev_rows(a, tm, n):
    return (a, (n, a.shape[1]), lambda i: (jnp.maximum(i * (tm // n) - 1, 0), 0))


def _next_rows(a, tm, n):
    last = a.shape[0] // n - 1
    return (a, (n, a.shape[1]), lambda i: (jnp.minimum((i + 1) * (tm // n), last), 0))


def _shifted(x, prev8, i):
    rowid = lax.broadcasted_iota(jnp.int32, x.shape, 0)
    before = jnp.where(i == 0, 0.0, prev8[7:8, :])
    return jnp.where(rowid == 0, before, pltpu.roll(x, 1, 0))


def _shift_fwd(p, mu, tm):
    def fn(i, pv, prev8, muv):
        x = pv[:, :D_SHIFT]
        return (x + (_shifted(x, prev8[:, :D_SHIFT], i) - x) * muv,), ()
    return _rows_call("shift", fn, [p, _prev_rows(p, tm, 8)], [mu], [(D_SHIFT, F32)], [], tm)[0]


def _shift_bwd(dps, p, dpc, mu, tm):
    n_tiles = p.shape[0] // tm

    def fn(i, dv, next8, pv, prev8, dpcv, muv):
        x = pv[:, :D_SHIFT]
        xs = _shifted(x, prev8[:, :D_SHIFT], i)
        rowid = lax.broadcasted_iota(jnp.int32, dv.shape, 0)
        after = jnp.where(i == n_tiles - 1, 0.0, next8[0:1, :])
        dnext = jnp.where(rowid == tm - 1, after, pltpu.roll(dv, tm - 1, 0))
        dp_s = dv * (1.0 - muv) + dnext * muv
        return (jnp.concatenate([dp_s.astype(BF16), dpcv], axis=1),), (_colsum(dv * (xs - x)),)
    return _rows_call("shift_b", fn, [dps, _next_rows(dps, tm, 8), p, _prev_rows(p, tm, 8), dpc], [mu],
                      [(D_IN, BF16)], [(1, D_SHIFT)], tm)


def _glu(pc):
    return pc[:, :DR] * jax.nn.sigmoid(pc[:, DR:])


def _shift_copies(ext, shifted, tm):
    for s in range(1, 8):
        shifted[s - 1] = ext[s:s + tm + 24, :]


def _window(ext, shifted, off, tm):
    if off % 8 == 0:
        return ext[off:off + tm, :]
    return shifted[off % 8 - 1, off // 8 * 8:off // 8 * 8 + tm, :]


def _conv_fwd(p, dw32, cb, lw, lb, tm):
    t_len = p.shape[0]

    def kern(p_ref, ph_ref, dw_ref, cb_ref, lw_ref, lb_ref, glu_ref, c_ref, ob_ref, ext, shifted):
        i = pl.program_id(0)
        glu = _glu(p_ref[:, D_SHIFT:])
        ext[0:32, :] = jnp.where(i == 0, 0.0, _glu(ph_ref[:, D_SHIFT:]))
        ext[32:, :] = glu
        _shift_copies(ext, shifted, tm)
        acc = jnp.zeros((tm, DR), F32)
        for j in range(CW):
            acc = acc + _window(ext, shifted, 2 + j, tm) * dw_ref[j:j + 1, :]
        c = acc + cb_ref[...]
        glu_ref[...] = glu
        c_ref[...] = c
        ob_ref[...] = _ln_silu(c, lw_ref[...], lb_ref[...]).astype(BF16)

    tile = lambda w: pl.BlockSpec((tm, w), lambda i: (i, 0))
    const = lambda a: pl.BlockSpec(a.shape, lambda i: (0, 0))
    halo = pl.BlockSpec((32, D_IN), lambda i: (jnp.maximum(i * (tm // 32) - 1, 0), 0))
    return pl.pallas_call(
        kern, grid=(t_len // tm,), in_specs=[tile(D_IN), halo, const(dw32), const(cb), const(lw), const(lb)],
        out_specs=[tile(DR)] * 3,
        out_shape=[jax.ShapeDtypeStruct((t_len, DR), F32)] * 2 + [jax.ShapeDtypeStruct((t_len, DR), BF16)],
        scratch_shapes=[pltpu.VMEM((tm + 32, DR), F32), pltpu.VMEM((7, tm + 24, DR), F32)], name="conv_fwd",
        compiler_params=_params(("arbitrary",)))(p, p, dw32, cb, lw, lb)


def _conv_bwd(dc, glu, p, dw32, tm):
    t_len = p.shape[0]
    n_tiles = t_len // tm

    def kern(dc_ref, dcn_ref, glu_ref, gluh_ref, p_ref, dw_ref, dpc_ref, ddw_ref, ext_d, ext_g, shifted_d, shifted_g):
        i = pl.program_id(0)

        @pl.when(i == 0)
        def _():
            ddw_ref[...] = jnp.zeros(ddw_ref.shape, F32)

        dcv = dc_ref[...]
        ext_d[0:tm, :] = dcv
        ext_d[tm:, :] = jnp.where(i == n_tiles - 1, 0.0, dcn_ref[...])
        ext_g[0:32, :] = jnp.where(i == 0, 0.0, gluh_ref[...])
        ext_g[32:, :] = glu_ref[...]
        _shift_copies(ext_d, shifted_d, tm)
        _shift_copies(ext_g, shifted_g, tm)
        dglu = jnp.zeros((tm, DR), F32)
        for j in range(CW):
            dglu = dglu + _window(ext_d, shifted_d, 30 - j, tm) * dw_ref[j:j + 1, :]
            ddw_ref[j:j + 1, :] += _colsum(dcv * _window(ext_g, shifted_g, 2 + j, tm))
        pc = p_ref[:, D_SHIFT:]
        sg = jax.nn.sigmoid(pc[:, DR:])
        dpc_ref[...] = jnp.concatenate([dglu * sg, dglu * pc[:, :DR] * sg * (1.0 - sg)], axis=1).astype(BF16)

    tile = lambda w: pl.BlockSpec((tm, w), lambda i: (i, 0))
    nxt = pl.BlockSpec((32, DR), lambda i: (jnp.minimum((i + 1) * (tm // 32), t_len // 32 - 1), 0))
    prv = pl.BlockSpec((32, DR), lambda i: (jnp.maximum(i * (tm // 32) - 1, 0), 0))
    return pl.pallas_call(
        kern, grid=(n_tiles,), in_specs=[tile(DR), nxt, tile(DR), prv, tile(D_IN), pl.BlockSpec((32, DR), lambda i: (0, 0))],
        out_specs=[tile(D), pl.BlockSpec((32, DR), lambda i: (0, 0))],
        out_shape=[jax.ShapeDtypeStruct((t_len, D), BF16), jax.ShapeDtypeStruct((32, DR), F32)],
        scratch_shapes=[pltpu.VMEM((tm + 32, DR), F32)] * 2 + [pltpu.VMEM((7, tm + 24, DR), F32)] * 2, name="conv_bwd",
        compiler_params=_params(("arbitrary",)))(dc, dc, glu, glu, p, dw32)


def _place():
    x, y, c = lax.axis_index("x"), lax.axis_index("y"), lax.axis_index("c")
    chips = [(1 - x, y), (x, 1 - y), (1 - x, 1 - y)]
    return x, y, c, chips


def _gather_weights(shards):
    n = len(shards)

    def body(*refs):
        start, middle, finish = _gather_plan(refs[:n], refs[n:2 * n], *refs[2 * n:])
        start()
        middle()
        finish()

    got = pl.pallas_call(body, in_specs=[ANY] * n, out_specs=[ANY] * n, out_shape=_gathered_shapes(shards),
                         scratch_shapes=_gather_sems(n), name="gather_weights")(*shards)
    return _fill_own(got, shards)


def _gathered_shapes(shards):
    return [jax.ShapeDtypeStruct((4,) + s.shape, s.dtype) for s in shards]


def _gather_sems(n):
    return [pltpu.SemaphoreType.DMA((6 * n,)), pltpu.SemaphoreType.DMA((6 * n,))]


def _fill_own(got, shards):
    me = 2 * lax.axis_index("x") + lax.axis_index("y")
    return [lax.dynamic_update_slice(g, s[None], (me, 0, 0, 0)) for g, s in zip(got, shards)]


def _gather_plan(src, dst, send, recv):
    n = len(src)
    x, y, c, chips = _place()
    me, sib = 2 * x + y, (x, y, 1 - c)

    def rcopy(k, sem, s_ref, d_ref, to):
        return pltpu.make_async_remote_copy(src_ref=s_ref, dst_ref=d_ref, send_sem=send.at[6 * k + sem],
                                            recv_sem=recv.at[6 * k + sem], device_id=to, device_id_type=MESH)

    def landed(k, m, half):
        return dst[k].at[2 * chips[m][0] + chips[m][1], half]

    first = [rcopy(k, m, src[k].at[c], dst[k].at[me, c], (*chips[m], c)) for k in range(n) for m in range(3)]
    passed = [rcopy(k, 3 + m, landed(k, m, c), landed(k, m, c), sib) for k in range(n) for m in range(3)]

    def start():
        for cp in first:
            cp.start()

    def middle():
        for k in range(n):
            for m in range(3):
                rcopy(k, m, landed(k, m, c), landed(k, m, c), sib).wait_recv()
                passed[3 * k + m].start()

    def finish():
        for k in range(n):
            for m in range(3):
                rcopy(k, 3 + m, landed(k, m, 1 - c), landed(k, m, 1 - c), sib).wait_recv()
        for cp in first + passed:
            cp.wait_send()

    return start, middle, finish


def _swap_halves(name, give):
    n = len(give)

    def body(*refs):
        src, got = refs[:n], refs[n:2 * n]
        send, recv = refs[2 * n:]
        x, y, c, _ = _place()
        copies = []
        for k in range(n):
            for j in range(4):
                copies.append(pltpu.make_async_remote_copy(
                    src_ref=src[k].at[j], dst_ref=got[k].at[j], send_sem=send.at[4 * k + j], recv_sem=recv.at[4 * k + j],
                    device_id=(x, y, 1 - c), device_id_type=MESH))
                copies[-1].start()
        for cp in copies:
            cp.wait()

    out_shape = [jax.ShapeDtypeStruct(g.shape, g.dtype) for g in give]
    return pl.pallas_call(body, in_specs=[ANY] * n, out_specs=[ANY] * n, out_shape=out_shape,
                          scratch_shapes=[pltpu.SemaphoreType.DMA((4 * n,)), pltpu.SemaphoreType.DMA((4 * n,))],
                          name=name)(*give)


def _to_owners(sums):
    n = len(sums)

    def body(*refs):
        start, finish = _owners_plan(refs[:n], refs[n:2 * n], *refs[2 * n:])
        start()
        finish()

    return pl.pallas_call(body, in_specs=[ANY] * n, out_specs=[ANY] * n, out_shape=_owner_shapes(sums),
                          scratch_shapes=_owner_sems(n), name="to_owners")(*sums)


def _owner_shapes(sums):
    return [jax.ShapeDtypeStruct((3,) + s.shape[1:], s.dtype) for s in sums]


def _owner_sems(n):
    return [pltpu.SemaphoreType.DMA((3 * n,)), pltpu.SemaphoreType.DMA((3 * n,))]


def _owners_plan(src, dst, send, recv):
    x, y, c, chips = _place()
    copies = [pltpu.make_async_remote_copy(
        src_ref=src[k].at[2 * chip[0] + chip[1]], dst_ref=dst[k].at[m], send_sem=send.at[3 * k + m],
        recv_sem=recv.at[3 * k + m], device_id=(*chip, c), device_id_type=MESH)
        for k in range(len(src)) for m, chip in enumerate(chips)]

    def start():
        for cp in copies:
            cp.start()

    def finish():
        for cp in copies:
            cp.wait()

    return start, finish


def _join_halves(halves):
    n = len(halves)

    def body(*refs):
        src, dst = refs[:n], refs[n:2 * n]
        send, recv = refs[2 * n:]
        x, y, c, _ = _place()
        copies = []
        for k in range(n):
            copies.append(pltpu.make_async_remote_copy(src_ref=src[k], dst_ref=dst[k], send_sem=send.at[k],
                                                       recv_sem=recv.at[k], device_id=(x, y, 1 - c), device_id_type=MESH))
            copies[-1].start()
        for cp in copies:
            cp.wait()

    out_shape = [jax.ShapeDtypeStruct(h.shape, h.dtype) for h in halves]
    return pl.pallas_call(body, in_specs=[ANY] * n, out_specs=[ANY] * n, out_shape=out_shape,
                          scratch_shapes=[pltpu.SemaphoreType.DMA((n,)), pltpu.SemaphoreType.DMA((n,))],
                          name="join_halves")(*halves)


def _allreduce_small(v):
    rows, n = v.shape

    def body(v_ref, o_ref, buf, send, recv):
        x, y, c, _ = _place()
        me = 4 * x + 2 * y + c
        buf[me] = v_ref[...]
        copies = []
        for d in range(1, 8):
            peer = (x ^ (d >> 2), y ^ ((d >> 1) & 1), c ^ (d & 1))
            cp = pltpu.make_async_remote_copy(src_ref=v_ref, dst_ref=buf.at[me], send_sem=send.at[d], recv_sem=recv.at[d],
                                              device_id=peer, device_id_type=MESH)
            cp.start()
            copies.append(cp)
        for d in range(1, 8):
            peer = 4 * (x ^ (d >> 2)) + 2 * (y ^ ((d >> 1) & 1)) + (c ^ (d & 1))
            pltpu.make_async_remote_copy(src_ref=v_ref, dst_ref=buf.at[peer], send_sem=send.at[d], recv_sem=recv.at[d],
                                         device_id=(x, y, c), device_id_type=MESH).wait_recv()
        for cp in copies:
            cp.wait_send()
        acc = buf[0]
        for d in range(1, 8):
            acc = acc + buf[d]
        o_ref[...] = acc

    vm = pl.BlockSpec(memory_space=pltpu.VMEM)
    return pl.pallas_call(body, in_specs=[vm], out_specs=vm, out_shape=jax.ShapeDtypeStruct((rows, n), F32),
                          scratch_shapes=[pltpu.VMEM((8, rows, n), F32), pltpu.SemaphoreType.DMA((8,)),
                                          pltpu.SemaphoreType.DMA((8,))], name="allreduce_small")(v)


def _add_call(name, parts, tm, out_dtype=F32):
    def fn(i, *vals):
        acc = vals[0].astype(F32)
        for v in vals[1:]:
            acc = acc + v.astype(F32)
        return (acc,), ()
    return _rows_call(name, fn, list(parts), [], [(parts[0].shape[1], out_dtype)], [], tm)[0]


def _adamw(name, w, g, m, v):
    c1 = 1.0 / (1.0 - ADAM_B1 ** ADAM_STEP)
    c2 = 1.0 / (1.0 - ADAM_B2 ** ADAM_STEP)

    def fn(i, wv, gv, mv, vv):
        m2 = ADAM_B1 * mv + (1.0 - ADAM_B1) * gv
        v2 = ADAM_B2 * vv + (1.0 - ADAM_B2) * jnp.square(gv)
        delta = -ADAM_LR * ((m2 * c1) / (jnp.sqrt(v2 * c2) + ADAM_EPS) + ADAM_WD * wv)
        return (delta, m2, v2), ()
    cols = w.shape[1]
    tm = _tile(w.shape[0], (256, 176, 128, 64, 8))
    return _rows_call(name, fn, [w, g, m, v], [], [(cols, F32)] * 3, [], tm)


def _canon(name, a, transposed):
    return a[0].T if transposed else a[0]


def _pack_sharded(w_up, a_up, g_up, conv_dw):
    parts = [w_up[0].T, a_up[0].T, g_up[0].T, conv_dw[0].T]
    used = sum(p.shape[1] for p in parts)
    return jnp.concatenate(parts + [jnp.zeros((parts[0].shape[0], PACK_W - used), F32)], axis=1)


def _unpack_sharded(a):
    return [a[:, 0:64].T[None], a[:, 64:128].T[None], a[:, 128:256].T[None], a[:, 256:256 + CW].T[None]]


def _pack_small(vals):
    flat = jnp.concatenate([v.reshape(-1) for v in vals] + [jnp.zeros((SMALL_PAD - SMALL_N,), F32)])
    return flat.reshape(8, SMALL_PAD // 8)


def _unpack_small(a, like):
    flat, out, off = a.reshape(-1), [], 0
    for (_, n), ref in zip(SMALL, like):
        out.append(flat[off:off + n].reshape(ref.shape))
        off += n
    return out


def kernel(x, ffn1_norm_pre, ffn1_norm_post, ffn1_w_gu, ffn1_w_down, mix_norm_pre, mix_norm_post, w_in, shift_mu, w_up, w0, a_up, a0, g_up, k_k, k_a, r_k, gn_w, gn_b, conv_dw, conv_b, conv_ln_w, conv_ln_b, w_out, ffn2_norm_pre, ffn2_norm_post, ffn2_w_gu, ffn2_w_down, loss_target, m_ffn1_norm_pre, m_ffn1_norm_post, m_ffn1_w_gu, m_ffn1_w_down, m_mix_norm_pre, m_mix_norm_post, m_w_in, m_shift_mu, m_w_up, m_w0, m_a_up, m_a0, m_g_up, m_k_k, m_k_a, m_r_k, m_gn_w, m_gn_b, m_conv_dw, m_conv_b, m_conv_ln_w, m_conv_ln_b, m_w_out, m_ffn2_norm_pre, m_ffn2_norm_post, m_ffn2_w_gu, m_ffn2_w_down, v_ffn1_norm_pre, v_ffn1_norm_post, v_ffn1_w_gu, v_ffn1_w_down, v_mix_norm_pre, v_mix_norm_post, v_w_in, v_shift_mu, v_w_up, v_w0, v_a_up, v_a0, v_g_up, v_k_k, v_k_a, v_r_k, v_gn_w, v_gn_b, v_conv_dw, v_conv_b, v_conv_ln_w, v_conv_ln_b, v_w_out, v_ffn2_norm_pre, v_ffn2_norm_post, v_ffn2_w_gu, v_ffn2_w_down):
    w = dict(ffn1_norm_pre=ffn1_norm_pre, ffn1_norm_post=ffn1_norm_post, ffn1_w_gu=ffn1_w_gu, ffn1_w_down=ffn1_w_down, mix_norm_pre=mix_norm_pre, mix_norm_post=mix_norm_post, w_in=w_in, shift_mu=shift_mu, w_up=w_up, w0=w0, a_up=a_up, a0=a0, g_up=g_up, k_k=k_k, k_a=k_a, r_k=r_k, gn_w=gn_w, gn_b=gn_b, conv_dw=conv_dw, conv_b=conv_b, conv_ln_w=conv_ln_w, conv_ln_b=conv_ln_b, w_out=w_out, ffn2_norm_pre=ffn2_norm_pre, ffn2_norm_post=ffn2_norm_post, ffn2_w_gu=ffn2_w_gu, ffn2_w_down=ffn2_w_down)
    mom = dict(ffn1_norm_pre=m_ffn1_norm_pre, ffn1_norm_post=m_ffn1_norm_post, ffn1_w_gu=m_ffn1_w_gu, ffn1_w_down=m_ffn1_w_down, mix_norm_pre=m_mix_norm_pre, mix_norm_post=m_mix_norm_post, w_in=m_w_in, shift_mu=m_shift_mu, w_up=m_w_up, w0=m_w0, a_up=m_a_up, a0=m_a0, g_up=m_g_up, k_k=m_k_k, k_a=m_k_a, r_k=m_r_k, gn_w=m_gn_w, gn_b=m_gn_b, conv_dw=m_conv_dw, conv_b=m_conv_b, conv_ln_w=m_conv_ln_w, conv_ln_b=m_conv_ln_b, w_out=m_w_out, ffn2_norm_pre=m_ffn2_norm_pre, ffn2_norm_post=m_ffn2_norm_post, ffn2_w_gu=m_ffn2_w_gu, ffn2_w_down=m_ffn2_w_down)
    var = dict(ffn1_norm_pre=v_ffn1_norm_pre, ffn1_norm_post=v_ffn1_norm_post, ffn1_w_gu=v_ffn1_w_gu, ffn1_w_down=v_ffn1_w_down, mix_norm_pre=v_mix_norm_pre, mix_norm_post=v_mix_norm_post, w_in=v_w_in, shift_mu=v_shift_mu, w_up=v_w_up, w0=v_w0, a_up=v_a_up, a0=v_a0, g_up=v_g_up, k_k=v_k_k, k_a=v_k_a, r_k=v_r_k, gn_w=v_gn_w, gn_b=v_gn_b, conv_dw=v_conv_dw, conv_b=v_conv_b, conv_ln_w=v_conv_ln_w, conv_ln_b=v_conv_ln_b, w_out=v_w_out, ffn2_norm_pre=v_ffn2_norm_pre, ffn2_norm_post=v_ffn2_norm_post, ffn2_w_gu=v_ffn2_w_gu, ffn2_w_down=v_ffn2_w_down)
    names = list(w)

    x0 = x[0]
    tgt = loss_target[0]
    t_len = x0.shape[0]
    tm = _tile(t_len, (512, 256, 128))

    half = {n: rows // 8 for n, rows, _ in BIG}
    pack_local = _pack_sharded(w["w_up"], w["a_up"], w["g_up"], w["conv_dw"])
    shard = {n: _canon(n, w[n], tr).astype(BF16).reshape(2, half[n], D) for n, _, tr in BIG}
    rows_of = {n: rows for n, rows, _ in BIG}
    gathered = _gather_weights([shard[n] for n in EARLY] + [pack_local.reshape(2, 64, PACK_W)])
    full = {n: g.reshape(rows_of[n], D) for n, g in zip(EARLY, gathered)}
    pack = gathered[-1].reshape(DR, PACK_W)
    p01, p2 = pack[:, 0:128], pack[:, 128:256]
    dw32 = jnp.concatenate([pack[:, 256:256 + CW].T, jnp.zeros((1, DR), F32)], axis=0)

    h1, gu1, act1, f1 = _ffn_fwd("ffn1", x0, w["ffn1_norm_pre"], full["ffn1_w_gu"], full["ffn1_w_down"], tm)
    (x1,) = _rows_call("ffn1_resid", lambda i, xv, fv, g: ((xv + 0.5 * _rms(fv, g),), ()), [x0, f1],
                       [w["ffn1_norm_post"]], [(D, F32)], [], tm)
    (hm,) = _rows_call("mix_norm", lambda i, xv, g: ((_rms(xv, g),), ()), [x1], [w["mix_norm_pre"]], [(D, BF16)], [], tm)
    p = _matmul("mix_in", hm, full["w_in"], "nt")
    ps = _shift_fwd(p, w["shift_mu"], tm)
    rkc = w["r_k"].reshape(1, DR)
    prep_consts = [w["w0"], w["a0"], w["k_k"], w["k_a"], p01, p2]
    r_, dec, k2, v_, z_, b_, g_ = _rows_call(
        "prep", lambda i, psv, *cs: (_prep(psv, *cs), ()), [ps], prep_consts, [(DR, F32)] * 7, [], min(tm, 256))
    y, ck, late = _rec_fwd(r_, dec, k2, z_, b_, v_, [shard[n] for n in LATE])
    full.update({n: g.reshape(rows_of[n], D) for n, g in zip(LATE, late)})
    glu, cpre, ob = _conv_fwd(p, dw32, w["conv_b"], w["conv_ln_w"], w["conv_ln_b"], tm)
    post_consts = [w["gn_w"], w["gn_b"], rkc]
    (o,) = _rows_call(
        "post", lambda i, yv, rv, kv, vv, gv, obv, *cs: ((jnp.concatenate([_post(yv, rv, kv, vv, gv, *cs).astype(BF16), obv], axis=1),), ()),
        [y, r_, k2, v_, g_, ob], post_consts, [(D, BF16)], [], min(tm, 256))
    mo = _matmul("mix_out", o, full["w_out"], "nn")
    (x2,) = _rows_call("mix_resid", lambda i, xv, fv, g: ((xv + _rms(fv, g),), ()), [x1, mo],
                       [w["mix_norm_post"]], [(D, F32)], [], tm)
    h2, gu2, act2, f2 = _ffn_fwd("ffn2", x2, w["ffn2_norm_pre"], full["ffn2_w_gu"], full["ffn2_w_down"], tm)

    def loss_fn(i, xv, fv, tv, g):
        err = xv + 0.5 * _rms(fv, g) - tv
        part = 0.5 * jnp.sum(jnp.mean(err * err, axis=-1, keepdims=True), axis=0, keepdims=True)
        return (err * (1.0 / D),), (jnp.broadcast_to(part, (8, 128)),)
    dx3, loss_part = _rows_call("loss", loss_fn, [x2, f2, tgt], [w["ffn2_norm_post"]], [(D, F32)], [(8, 128)], tm)
    loss = lax.psum(loss_part[0, 0], ("x", "y", "c"))

    g_small = {}
    dx2, g_small["ffn2_norm_pre"], g_small["ffn2_norm_post"], dgu2_t, dwd2 = _ffn_bwd(
        "ffn2", x2, w["ffn2_norm_pre"], w["ffn2_norm_post"], 0.5, h2, gu2, act2, f2, dx3, full["ffn2_w_gu"], full["ffn2_w_down"], tm)

    def mix_resid_b(i, fv, dv, g):
        _, vjp = jax.vjp(_rms, fv, g)
        df, dg = vjp(dv)
        return (df,), (dg,)
    dmo, g_small["mix_norm_post"] = _rows_call("mix_resid_b", mix_resid_b, [mo, dx2], [w["mix_norm_post"]],
                                               [(D, BF16)], [(1, D)], tm)
    do = _matmul("mix_do", dmo, full["w_out"], "nt")
    dw_out = _matmul("mix_dwout", o, dmo, "tn")

    def post_b(i, yv, rv, kv, vv, gv, dov, *cs):
        _, vjp = jax.vjp(_post, yv, rv, kv, vv, gv, *cs)
        dy, dr, dk, dv, dg, dgw, dgb, drk = vjp(dov[:, :DR])
        return (dy, dr, dk, dv, dg), (_colsum(dgw), _colsum(dgb), _colsum(drk))
    dy, dr1, dk1, dv1, dg, g_small["gn_w"], g_small["gn_b"], g_small["r_k"] = _rows_call(
        "post_b", post_b, [y, r_, k2, v_, g_, do], post_consts, [(DR, F32)] * 5, [(1, DR)] * 3, min(tm, 256))
    my_c = lax.axis_index("c")
    my_chip = 2 * lax.axis_index("x") + lax.axis_index("y")
    g_big = dict(w_out=dw_out, ffn2_w_gu=dgu2_t, ffn2_w_down=dwd2)

    def halves_of(group, which):
        return jnp.concatenate([lax.dynamic_index_in_dim(g_big[n].reshape(4, 2, half[n], D), which, 1, keepdims=False)
                                for n in group], axis=1)

    def pair_sums(tag, group, extra_mine=(), extra_give=()):
        mine = halves_of(group, my_c)
        got, *got_x = _swap_halves("swap_halves_" + tag, [halves_of(group, 1 - my_c)] + list(extra_give))
        rows = mine.shape[1]
        tile = _tile(rows, (352, 592, 16))
        travels = _add_call("add_pair_" + tag, [mine.reshape(4 * rows, D), got.reshape(4 * rows, D)], tile, BF16)
        return mine, got, travels.reshape(4, rows, D), got_x

    def owner_sum(tag, mine, got, parts):
        own = [lax.dynamic_index_in_dim(a, my_chip, 0, keepdims=False) for a in (mine, got)]
        return _add_call("add_chips_" + tag, own + [parts[m] for m in range(3)], _tile(mine.shape[1], (352, 592, 16)))

    mine_l, got_l, sum_l, _ = pair_sums("late", LATE)
    (dr2, ddec, dk2, dz, db, dv2), (parts_l,) = _rec_bwd(r_, dec, k2, z_, b_, v_, dy, ck, [sum_l])

    def prep_b(i, psv, a1, a2, c1, c2, e1, e2, dwv, dzv, dbv, dgv, *cs):
        _, vjp = jax.vjp(_prep, psv, *cs)
        dps, dw0, da0, dkk, dka, dp01, dp2 = vjp((a1 + a2, dwv, c1 + c2, e1 + e2, dzv, dbv, dgv))
        return (dps,), (_colsum(dw0), _colsum(da0), _colsum(dkk), _colsum(dka), dp01, dp2)
    dps, g_small["w0"], g_small["a0"], g_small["k_k"], g_small["k_a"], dp01, dp2 = _rows_call(
        "prep_b", prep_b, [ps, dr1, dr2, dk1, dk2, dv1, dv2, ddec, dz, db, dg], prep_consts, [(D_SHIFT, F32)],
        [(1, DR)] * 4 + [(DR, 128)] * 2, min(tm, 256))

    def convln_b(i, cv, dov, lw, lb):
        _, vjp = jax.vjp(_ln_silu, cv, lw, lb)
        dc, dlw, dlb = vjp(dov[:, DR:])
        return (dc,), (_colsum(dc), _colsum(dlw), _colsum(dlb))
    dc, g_small["conv_b"], g_small["conv_ln_w"], g_small["conv_ln_b"] = _rows_call(
        "convln_b", convln_b, [cpre, do], [w["conv_ln_w"], w["conv_ln_b"]], [(DR, F32)], [(1, DR)] * 3, tm)
    dpc, ddw32 = _conv_bwd(dc, glu, p, dw32, tm)
    dp, g_small["shift_mu"] = _shift_bwd(dps, p, dpc, w["shift_mu"], tm)
    dhm = _matmul("mix_dh", dp, full["w_in"], "nn")
    dw_in_t = _matmul("mix_dwin", dp, hm, "tn")

    def norm_b(i, xv, dhv, dv, g):
        _, vjp = jax.vjp(_rms, xv, g)
        dx, dg_ = vjp(dhv)
        return (dx + dv,), (dg_,)
    dx1, g_small["mix_norm_pre"] = _rows_call("mix_norm_b", norm_b, [x1, dhm, dx2], [w["mix_norm_pre"]],
                                              [(D, F32)], [(1, D)], tm)
    grad_x, g_small["ffn1_norm_pre"], g_small["ffn1_norm_post"], dgu1_t, dwd1 = _ffn_bwd(
        "ffn1", x0, w["ffn1_norm_pre"], w["ffn1_norm_post"], 0.5, h1, gu1, act1, f1, dx1, full["ffn1_w_gu"], full["ffn1_w_down"], tm)

    g_big.update(ffn1_w_gu=dgu1_t, ffn1_w_down=dwd1, w_in=dw_in_t)
    pack_grads = jnp.concatenate([dp01, dp2, ddw32.T, jnp.zeros((DR, PACK_W - 288), F32)], axis=1).reshape(4, 2, 64, PACK_W)
    mine_p, give_p = (lax.dynamic_index_in_dim(pack_grads, which, 1, keepdims=False) for which in (my_c, 1 - my_c))
    mine_e, got_e, sum_e, (got_p,) = pair_sums("early", EARLY, extra_give=[give_p])
    sum_p = _add_call("add_pair_pack", [mine_p.reshape(256, PACK_W), got_p.reshape(256, PACK_W)], 256).reshape(4, 64, PACK_W)
    parts_e, parts_p = _to_owners([sum_e, sum_p])
    fin_e = owner_sum("early", mine_e, got_e, parts_e)
    fin_l = owner_sum("late", mine_l, got_l, parts_l)
    fin_p = _add_call("add_chips_pack", [lax.dynamic_index_in_dim(sum_p, my_chip, 0, keepdims=False)] + [parts_p[m] for m in range(3)], 64)
    fins = [fin_e, fin_l, fin_p]
    red_e, red_l, red_p = [jnp.where(my_c == 0, jnp.stack([f, s]), jnp.stack([s, f])) for f, s in zip(fins, _join_halves(fins))]
    small_sum = _allreduce_small(_pack_small([g_small[n] for n, _ in SMALL]))

    grads, delta, new_m, new_v = {}, {}, {}, {}
    reduced = {}
    for group, red in ((EARLY, red_e), (LATE, red_l)):
        off = 0
        for n in group:
            reduced[n] = red[:, off:off + half[n], :].reshape(rows_of[n] // 4, D)
            off += half[n]
    for n, rows, tr in BIG:
        g = reduced[n]
        g = (g.T if tr else g)[None]
        grads[n] = g
        d_, m_, v2_ = _adamw("adamw_" + n, w[n][0], g[0], mom[n][0], var[n][0])
        delta[n], new_m[n], new_v[n] = d_[None], m_[None], v2_[None]
    sh = ("w_up", "a_up", "g_up", "conv_dw")
    g_pack = red_p.reshape(128, PACK_W)
    d_, m_, v2_ = _adamw("adamw_pack", pack_local, g_pack, _pack_sharded(*[mom[n] for n in sh]), _pack_sharded(*[var[n] for n in sh]))
    for dst, src in ((grads, g_pack), (delta, d_), (new_m, m_), (new_v, v2_)):
        for n, a in zip(sh, _unpack_sharded(src)):
            dst[n] = a
    sm = [n for n, _ in SMALL]
    d_, m_, v2_ = _adamw("adamw_small", _pack_small([w[n] for n in sm]), small_sum, _pack_small([mom[n] for n in sm]),
                         _pack_small([var[n] for n in sm]))
    like = [w[n] for n in sm]
    for dst, src in ((grads, small_sum), (delta, d_), (new_m, m_), (new_v, v2_)):
        for n, a in zip(sm, _unpack_small(src, like)):
            dst[n] = a

    wn = names
    return (loss, grad_x[None], *[grads[n] for n in wn], *[delta[n] for n in wn], *[new_m[n] for n in wn],
            *[new_v[n] for n in wn])
```

```python
import functools
import math

import jax
import jax.numpy as jnp
from jax import lax
from jax.experimental import pallas as pl
from jax.experimental.pallas import tpu as pltpu

F32 = jnp.float32
BF16 = jnp.bfloat16

D = 1024
DFF = 2816
DR = 512
HS = 64
D_SHIFT = 1792
D_IN = 2816
CW = 31
RMS_EPS = 1e-6
GN_EPS = 64e-5
LN_EPS = 1e-5
DECAY_SCALE = math.exp(-0.5)
ADAM_LR, ADAM_B1, ADAM_B2, ADAM_EPS, ADAM_WD, ADAM_STEP = 0.001, 0.9, 0.999, 1e-8, 0.01, 10

REC_TILE = 128
VMEM_LIMIT = 56 * 1024 * 1024
MESH = pl.DeviceIdType.MESH
ANY = pl.BlockSpec(memory_space=pl.ANY)

BIG = (("ffn1_w_gu", 2 * DFF, True), ("ffn1_w_down", DFF, False), ("w_in", D_IN, True),
       ("w_out", D, False), ("ffn2_w_gu", 2 * DFF, True), ("ffn2_w_down", DFF, False))
EARLY = ("ffn1_w_gu", "ffn1_w_down", "w_in")
LATE = ("w_out", "ffn2_w_gu", "ffn2_w_down")
PACK_W = 384
SMALL = (("ffn1_norm_pre", D), ("ffn1_norm_post", D), ("mix_norm_pre", D), ("mix_norm_post", D),
         ("shift_mu", D_SHIFT), ("w0", DR), ("a0", DR), ("k_k", DR), ("k_a", DR), ("r_k", DR),
         ("gn_w", DR), ("gn_b", DR), ("conv_b", DR), ("conv_ln_w", DR), ("conv_ln_b", DR),
         ("ffn2_norm_pre", D), ("ffn2_norm_post", D))
SMALL_N = sum(n for _, n in SMALL)
SMALL_PAD = 8 * 1664


def _params(sem):
    return pltpu.CompilerParams(dimension_semantics=sem, vmem_limit_bytes=VMEM_LIMIT)


def _tile(n, prefs):
    for p in prefs:
        if n % p == 0:
            return p
    return n


def _rows_call(name, fn, rows, consts, row_outs, acc_outs, tm):
    specs, arrs = [], []
    for r in rows:
        if isinstance(r, tuple):
            a, bs, im = r
            specs.append(pl.BlockSpec(bs, im))
        else:
            a = r
            specs.append(pl.BlockSpec((tm, a.shape[1]), lambda i: (i, 0)))
        arrs.append(a)
    t_rows = arrs[0].shape[0]
    for c in consts:
        specs.append(pl.BlockSpec(c.shape, functools.partial(lambda i, n: (0,) * n, n=c.ndim)))
        arrs.append(c)
    n_in, n_o, n_a = len(arrs), len(row_outs), len(acc_outs)

    def kern(*refs):
        i = pl.program_id(0)
        vals = [r[...] for r in refs[:n_in]]
        ro, ao = fn(i, *vals)
        outs = refs[n_in:]
        for k in range(n_o):
            outs[k][...] = ro[k].astype(outs[k].dtype)
        if n_a:
            @pl.when(i == 0)
            def _():
                for k in range(n_a):
                    outs[n_o + k][...] = jnp.zeros(outs[n_o + k].shape, F32)
            for k in range(n_a):
                outs[n_o + k][...] += ao[k]

    out_specs = [pl.BlockSpec((tm, w), lambda i: (i, 0)) for (w, _) in row_outs]
    out_specs += [pl.BlockSpec(s, functools.partial(lambda i, n: (0,) * n, n=len(s))) for s in acc_outs]
    out_shape = [jax.ShapeDtypeStruct((t_rows, w), dt) for (w, dt) in row_outs]
    out_shape += [jax.ShapeDtypeStruct(s, F32) for s in acc_outs]
    return pl.pallas_call(kern, grid=(t_rows // tm,), in_specs=specs, out_specs=out_specs, out_shape=out_shape,
                          name=name, compiler_params=_params(("arbitrary",)))(*arrs)


_DIMS = {"nn": (((1,), (0,)), ((), ())), "nt": (((1,), (1,)), ((), ())), "tn": (((0,), (0,)), ((), ()))}
_LANE_TILES = (1408, 1024, 512, 384, 256, 128)


def _matmul(name, a, b, mode, out_dtype=F32, owners=None):
    if mode == "nn":
        (m, k), (_, n) = a.shape, b.shape
    elif mode == "nt":
        (m, k), (n, _) = a.shape, b.shape
    else:
        (k, m), (_, n) = a.shape, b.shape
    if mode == "tn":
        tm, tk = _tile(m, _LANE_TILES), _tile(k, (512, 256, 128))
    else:
        tm, tk = _tile(m, (1024, 512, 256, 128)), _tile(k, _LANE_TILES)
    tn = _tile(n, _LANE_TILES)
    nk = k // tk
    assert out_dtype == F32 or nk == 1

    owners = list(owners or ())
    ns = len(owners)
    grid = (m // tm, n // tn, nk)

    def kern(a_ref, b_ref, *rest):
        o_ref = rest[ns]
        if ns:
            start, finish = _owners_plan(rest[:ns], rest[ns + 1:2 * ns + 1], *rest[2 * ns + 1:])
            step = (pl.program_id(0) * grid[1] + pl.program_id(1)) * nk + pl.program_id(2)
            pl.when(step == 0)(start)

        def part():
            return lax.dot_general(a_ref[...].astype(BF16), b_ref[...].astype(BF16), _DIMS[mode], preferred_element_type=F32)

        if nk == 1:
            o_ref[...] = part().astype(o_ref.dtype)
        else:
            @pl.when(pl.program_id(2) == 0)
            def _():
                o_ref[...] = jnp.zeros(o_ref.shape, F32)

            o_ref[...] += part()
        if ns:
            pl.when(step == grid[0] * grid[1] * nk - 1)(finish)

    a_spec = pl.BlockSpec((tk, tm), lambda i, j, q: (q, i)) if mode == "tn" else pl.BlockSpec((tm, tk), lambda i, j, q: (i, q))
    b_spec = pl.BlockSpec((tn, tk), lambda i, j, q: (j, q)) if mode == "nt" else pl.BlockSpec((tk, tn), lambda i, j, q: (q, j))
    out, *arrived = pl.pallas_call(
        kern, grid=grid, in_specs=[a_spec, b_spec] + [ANY] * ns,
        out_specs=[pl.BlockSpec((tm, tn), lambda i, j, q: (i, j))] + [ANY] * ns,
        out_shape=[jax.ShapeDtypeStruct((m, n), out_dtype)] + (_owner_shapes(owners) if ns else []),
        scratch_shapes=_owner_sems(ns) if ns else [], name=name,
        compiler_params=_params(("arbitrary", "arbitrary", "arbitrary")))(a, b, *owners)
    return (out, arrived) if ns else out


def _rms(x, g):
    return x * lax.rsqrt(jnp.mean(x * x, axis=-1, keepdims=True) + RMS_EPS) * g


def _silu(x):
    return x * jax.nn.sigmoid(x)


def _bones(n):
    r = lax.broadcasted_iota(jnp.int32, (n, n), 0) // HS
    c = lax.broadcasted_iota(jnp.int32, (n, n), 1) // HS
    return (r == c).astype(BF16)


@jax.custom_vjp
def _segsum(x):
    bones = _bones(x.shape[1])
    hi = x.astype(BF16)
    lo = (x - hi.astype(F32)).astype(BF16)
    return jnp.dot(hi, bones, preferred_element_type=F32) + jnp.dot(lo, bones, preferred_element_type=F32)


_segsum.defvjp(lambda x: (_segsum(x), None), lambda _, ct: (_segsum(ct),))


@jax.custom_vjp
def _dot_nt(x, w):
    return lax.dot_general(x.astype(BF16), w.astype(BF16), _DIMS["nt"], preferred_element_type=F32)


def _dot_nt_bwd(res, ct):
    x, w = res
    ctb = ct.astype(BF16)
    dx = lax.dot_general(ctb, w.astype(BF16), _DIMS["nn"], preferred_element_type=F32)
    dw = lax.dot_general(ctb, x.astype(BF16), _DIMS["tn"], preferred_element_type=F32)
    return dx, dw


_dot_nt.defvjp(lambda x, w: (_dot_nt(x, w), (x, w)), _dot_nt_bwd)


def _swiglu(gu):
    return _silu(gu[:, :DFF]) * gu[:, DFF:]


def _prep(ps, w0, a0, k_k, k_a, p01, p2):
    r, k, v = ps[:, :DR], ps[:, DR:2 * DR], ps[:, 2 * DR:3 * DR]
    wa, xg = ps[:, 3 * DR:3 * DR + 128], ps[:, 3 * DR + 128:]
    first = lax.broadcasted_iota(jnp.int32, (1, 128), 1) < 64
    d = w0 + _dot_nt(jnp.where(first, jnp.tanh(wa), 0.0), p01)
    decay = jnp.exp(-DECAY_SCALE * jax.nn.sigmoid(d))
    a = jax.nn.sigmoid(a0 + _dot_nt(jnp.where(first, 0.0, wa), p01))
    g = _dot_nt(jax.nn.sigmoid(xg), p2)
    kk = k * k_k
    kk = kk * lax.rsqrt(jnp.maximum(_segsum(kk * kk), 1e-12))
    k2 = k * (1.0 + (a - 1.0) * k_a)
    return r, decay, k2, v, -kk, kk * a, g


def _post(y, r, k, v, g, gn_w, gn_b, r_k):
    mu = _segsum(y) * (1.0 / HS)
    yc = y - mu
    var = _segsum(yc * yc) * (1.0 / HS)
    yn = yc * lax.rsqrt(var + GN_EPS) * gn_w + gn_b
    return (yn + _segsum(r * k * r_k) * v) * g


def _ln_silu(c, w, b):
    mu = jnp.mean(c, axis=-1, keepdims=True)
    var = jnp.mean(jnp.square(c - mu), axis=-1, keepdims=True)
    return _silu((c - mu) * lax.rsqrt(var + LN_EPS) * w + b)


def _colsum(x):
    return jnp.sum(x, axis=0, keepdims=True)


def _ffn_fwd(tag, x, pre, w_gu_t, w_down, tm):
    (h,) = _rows_call(tag + "_norm", lambda i, xv, g: ((_rms(xv, g),), ()), [x], [pre], [(D, BF16)], [], tm)
    gu = _matmul(tag + "_gu", h, w_gu_t, "nt")
    (act,) = _rows_call(tag + "_act", lambda i, v: ((_swiglu(v),), ()), [gu], [], [(DFF, BF16)], [], min(tm, 256))
    f = _matmul(tag + "_down", act, w_down, "nn")
    return h, gu, act, f


def _ffn_bwd(tag, x, pre, post, scale, h, gu, act, f, dxo, w_gu_t, w_down, tm, to_owners=None):
    def resid_b(i, fv, dv, g):
        _, vjp = jax.vjp(lambda a, b: scale * _rms(a, b), fv, g)
        df, dg = vjp(dv)
        return (df,), (dg,)
    df, dpost = _rows_call(tag + "_resid_b", resid_b, [f, dxo], [post], [(D, BF16)], [(1, D)], tm)
    dact = _matmul(tag + "_dact", df, w_down, "nt")
    dw_down = _matmul(tag + "_dwdown", act, df, "tn")

    def act_b(i, guv, dv):
        _, vjp = jax.vjp(_swiglu, guv)
        return (vjp(dv)[0],), ()
    (dgu,) = _rows_call(tag + "_act_b", act_b, [gu, dact], [], [(2 * DFF, BF16)], [], min(tm, 256))
    dw_gu_t = _matmul(tag + "_dwgu", dgu, h, "tn")
    sums = to_owners(dw_gu_t, dw_down) if to_owners else []
    dh = _matmul(tag + "_dh", dgu, w_gu_t, "nn", owners=sums)
    dh, parts = dh if sums else (dh, [])

    def norm_b(i, xv, dhv, dv, g):
        _, vjp = jax.vjp(_rms, xv, g)
        dx, dg = vjp(dhv)
        return (dx + dv,), (dg,)
    dx, dpre = _rows_call(tag + "_norm_b", norm_b, [x, dh, dxo], [pre], [(D, F32)], [(1, D)], tm)
    return dx, dpre, dpost, dw_gu_t, dw_down, parts


def _pair_bcast(cols, first):
    return jnp.concatenate([jnp.where(first, cols[2 * p], cols[2 * p + 1]) for p in range(4)], axis=1)


def _head_sums(x, first):
    cols = []
    for p in range(4):
        xp = x[:, 128 * p:128 * (p + 1)]
        cols.append(jnp.sum(jnp.where(first, xp, 0.0), axis=1, keepdims=True))
        cols.append(jnp.sum(jnp.where(first, 0.0, xp), axis=1, keepdims=True))
    return cols


def _split16(x8):
    hi = x8.astype(BF16).astype(F32)
    return jnp.concatenate([hi, x8 - hi], axis=0).astype(BF16)


def _cols8(x8, e16):
    return lax.dot_general(_split16(x8), e16, _DIMS["tn"], preferred_element_type=F32)


def _pair_rows(c, first):
    return jnp.concatenate([jnp.where(first, c[128 * p:128 * p + HS], c[128 * p + HS:128 * (p + 1)]) for p in range(4)], axis=1)


def _rows8(prod, bones, dmask):
    x = prod.astype(BF16)
    full = jnp.concatenate([jnp.dot(x[:, 256 * q:256 * (q + 1)], bones, preferred_element_type=F32) for q in range(2)], axis=1)
    return jnp.concatenate([_colsum(full[HS * j:HS * (j + 1)] * dmask) for j in range(8)], axis=0)


def _rec_step(s, wr, zr, br, kr, vc, first):
    u = _pair_bcast(_head_sums(s * zr, first), first)
    return s * wr + u * br + vc * kr, u


def _rec_consts():
    e16 = (lax.broadcasted_iota(jnp.int32, (16, 1024), 0) % 8 == lax.broadcasted_iota(jnp.int32, (16, 1024), 1) // 128)
    dmask = lax.broadcasted_iota(jnp.int32, (HS, DR), 0) == lax.broadcasted_iota(jnp.int32, (HS, DR), 1) % HS
    return e16.astype(BF16), _bones(256), dmask.astype(F32)


def _const_spec(a):
    return pl.BlockSpec(a.shape, functools.partial(lambda i, n: (0,) * n, n=a.ndim))


def _rec_fwd(r, w, k, z, b, v, shards):
    t_len = r.shape[0]
    nt = t_len // REC_TILE
    ns = len(shards)
    consts = _rec_consts()

    def kern(r_ref, w_ref, k_ref, z_ref, b_ref, v_ref, e_ref, bones_ref, dm_ref, *rest):
        y_ref, ck_ref, states, u_ref = rest[ns:ns + 4]
        s_ref, prod = rest[2 * ns + 4:2 * ns + 6]
        start, middle, finish = _gather_plan(rest[:ns], rest[ns + 4:2 * ns + 4], *rest[2 * ns + 6:])
        i = pl.program_id(0)

        @pl.when(i == 0)
        def _():
            s_ref[...] = jnp.zeros(s_ref.shape, F32)
            start()

        pl.when(i == nt // 2)(middle)
        ck_ref[0] = s_ref[...]
        first = lax.broadcasted_iota(jnp.int32, (1, 128), 1) < HS

        def group(g8, s):
            base = pl.multiple_of(g8 * 8, 8)
            r8, w8, k8, z8, b8, v8 = (ref[pl.ds(base, 8), :] for ref in (r_ref, w_ref, k_ref, z_ref, b_ref, v_ref))
            vcols = _cols8(v8, e_ref[...])
            urows = []
            for j in range(8):
                vc = _pair_rows(vcols[:, 128 * j:128 * (j + 1)], first)
                s, u = _rec_step(s, w8[j:j + 1], z8[j:j + 1], b8[j:j + 1], k8[j:j + 1], vc, first)
                states[base + j] = s
                urows.append(_colsum(u * dm_ref[...]))
                prod[HS * j:HS * (j + 1), :] = s * r8[j:j + 1]
            y_ref[pl.ds(base, 8), :] = _rows8(prod[...], bones_ref[...], dm_ref[...])
            u_ref[pl.ds(base, 8), :] = jnp.concatenate(urows, axis=0)
            return s

        s_ref[...] = lax.fori_loop(0, REC_TILE // 8, group, s_ref[...])
        pl.when(i == nt - 1)(finish)

    row = pl.BlockSpec((REC_TILE, DR), lambda i: (i, 0))
    y, ck, states, u, *got = pl.pallas_call(
        kern, grid=(nt,), in_specs=[row] * 6 + [_const_spec(c) for c in consts] + [ANY] * ns,
        out_specs=[row, pl.BlockSpec((1, HS, DR), lambda i: (i, 0, 0)), pl.BlockSpec((REC_TILE, HS, DR), lambda i: (i, 0, 0)), row]
        + [ANY] * ns,
        out_shape=[jax.ShapeDtypeStruct((t_len, DR), F32), jax.ShapeDtypeStruct((nt, HS, DR), F32),
                   jax.ShapeDtypeStruct((t_len, HS, DR), F32), jax.ShapeDtypeStruct((t_len, DR), F32)] + _gathered_shapes(shards),
        scratch_shapes=[pltpu.VMEM((HS, DR), F32), pltpu.VMEM((8 * HS, DR), F32)] + _gather_sems(ns), name="rec_fwd",
        compiler_params=_params(("arbitrary",)))(r, w, k, z, b, v, *consts, *shards)
    return y, (ck, states, u), _fill_own(got, shards)


def _rec_bwd(r, w, k, z, b, v, dy, saved, sums):
    t_len = r.shape[0]
    nt = t_len // REC_TILE
    ns = len(sums)
    consts = _rec_consts()

    def kern(r_ref, w_ref, k_ref, z_ref, b_ref, v_ref, dy_ref, u_ref, ck_ref, states, e_ref, bones_ref, dm_ref, *rest):
        dr_ref, dw_ref, dk_ref, dz_ref, db_ref, dv_ref = rest[ns:ns + 6]
        ds_ref, prod = rest[2 * ns + 6:2 * ns + 8]
        start, finish = _owners_plan(rest[:ns], rest[ns + 6:2 * ns + 6], *rest[2 * ns + 8:])
        i = pl.program_id(0)

        @pl.when(i == 0)
        def _():
            ds_ref[...] = jnp.zeros(ds_ref.shape, F32)
            start()

        first = lax.broadcasted_iota(jnp.int32, (1, 128), 1) < HS

        def bgroup(gg, ds):
            base = pl.multiple_of((REC_TILE // 8 - 1 - gg) * 8, 8)
            r8, w8, k8, z8, b8, v8, dy8, u8 = (ref[pl.ds(base, 8), :]
                                               for ref in (r_ref, w_ref, k_ref, z_ref, b_ref, v_ref, dy_ref, u_ref))
            vcols = _cols8(v8, e_ref[...])
            dycols = _cols8(dy8, e_ref[...])
            ucols = _cols8(u8, e_ref[...])
            before = jnp.where(base == 0, ck_ref[0], states[jnp.maximum(base - 1, 0)])
            rows = {n: [None] * 8 for n in ("dr", "dw", "dk", "dz", "db")}
            for j in range(7, -1, -1):
                t = base + j
                rr, wr, kr, zr, br = (x[j:j + 1] for x in (r8, w8, k8, z8, b8))
                s_prev, s_t = (states[t - 1] if j else before), states[t]
                dyc = _pair_rows(dycols[:, 128 * j:128 * (j + 1)], first)
                vc = _pair_rows(vcols[:, 128 * j:128 * (j + 1)], first)
                ds = ds + dyc * rr
                rows["dr"][j] = _colsum(s_t * dyc)
                rows["dw"][j] = _colsum(ds * s_prev)
                du = _pair_bcast(_head_sums(ds * br, first), first)
                rows["db"][j] = _colsum(ds * _pair_rows(ucols[:, 128 * j:128 * (j + 1)], first))
                rows["dk"][j] = _colsum(ds * vc)
                prod[HS * j:HS * (j + 1), :] = ds * kr
                rows["dz"][j] = _colsum(s_prev * du)
                ds = ds * wr + du * zr
            for n, ref in (("dr", dr_ref), ("dw", dw_ref), ("dk", dk_ref), ("dz", dz_ref), ("db", db_ref)):
                ref[pl.ds(base, 8), :] = jnp.concatenate(rows[n], axis=0)
            dv_ref[pl.ds(base, 8), :] = _rows8(prod[...], bones_ref[...], dm_ref[...])
            return ds

        ds_ref[...] = lax.fori_loop(0, REC_TILE // 8, bgroup, ds_ref[...])
        pl.when(i == nt - 1)(finish)

    ck, states, u = saved
    row = pl.BlockSpec((REC_TILE, DR), lambda i: (nt - 1 - i, 0))
    outs = pl.pallas_call(
        kern, grid=(nt,),
        in_specs=[row] * 8 + [pl.BlockSpec((1, HS, DR), lambda i: (nt - 1 - i, 0, 0)),
                              pl.BlockSpec((REC_TILE, HS, DR), lambda i: (nt - 1 - i, 0, 0))]
        + [_const_spec(c) for c in consts] + [ANY] * ns,
        out_specs=[row] * 6 + [ANY] * ns, out_shape=[jax.ShapeDtypeStruct((t_len, DR), F32)] * 6 + _owner_shapes(sums),
        scratch_shapes=[pltpu.VMEM((HS, DR), F32), pltpu.VMEM((8 * HS, DR), F32)] + _owner_sems(ns), name="rec_bwd",
        compiler_params=_params(("arbitrary",)))(r, w, k, z, b, v, dy, u, ck, states, *consts, *sums)
    return outs[:6], outs[6:]


def _prev_rows(a, tm, n):
    return (a, (n, a.shape[1]), lambda i: (jnp.maximum(i * (tm // n) - 1, 0), 0))


def _next_rows(a, tm, n):
    last = a.shape[0] // n - 1
    return (a, (n, a.shape[1]), lambda i: (jnp.minimum((i + 1) * (tm // n), last), 0))


def _shifted(x, prev8, i):
    rowid = lax.broadcasted_iota(jnp.int32, x.shape, 0)
    before = jnp.where(i == 0, 0.0, prev8[7:8, :])
    return jnp.where(rowid == 0, before, pltpu.roll(x, 1, 0))


def _shift_fwd(p, mu, tm):
    def fn(i, pv, prev8, muv):
        x = pv[:, :D_SHIFT]
        return (x + (_shifted(x, prev8[:, :D_SHIFT], i) - x) * muv,), ()
    return _rows_call("shift", fn, [p, _prev_rows(p, tm, 8)], [mu], [(D_SHIFT, F32)], [], tm)[0]


def _shift_bwd(dps, p, dpc, mu, tm):
    n_tiles = p.shape[0] // tm

    def fn(i, dv, next8, pv, prev8, dpcv, muv):
        x = pv[:, :D_SHIFT]
        xs = _shifted(x, prev8[:, :D_SHIFT], i)
        rowid = lax.broadcasted_iota(jnp.int32, dv.shape, 0)
        after = jnp.where(i == n_tiles - 1, 0.0, next8[0:1, :])
        dnext = jnp.where(rowid == tm - 1, after, pltpu.roll(dv, tm - 1, 0))
        dp_s = dv * (1.0 - muv) + dnext * muv
        return (jnp.concatenate([dp_s.astype(BF16), dpcv], axis=1),), (_colsum(dv * (xs - x)),)
    return _rows_call("shift_b", fn, [dps, _next_rows(dps, tm, 8), p, _prev_rows(p, tm, 8), dpc], [mu],
                      [(D_IN, BF16)], [(1, D_SHIFT)], tm)


def _glu(pc):
    return pc[:, :DR] * jax.nn.sigmoid(pc[:, DR:])


def _shift_copies(ext, shifted, tm):
    for s in range(1, 8):
        shifted[s - 1] = ext[s:s + tm + 24, :]


def _window(ext, shifted, off, tm):
    if off % 8 == 0:
        return ext[off:off + tm, :]
    return shifted[off % 8 - 1, off // 8 * 8:off // 8 * 8 + tm, :]


def _conv_fwd(p, dw32, cb, lw, lb, tm):
    t_len = p.shape[0]

    def kern(p_ref, ph_ref, dw_ref, cb_ref, lw_ref, lb_ref, glu_ref, c_ref, ob_ref, ext, shifted):
        i = pl.program_id(0)
        glu = _glu(p_ref[:, D_SHIFT:])
        ext[0:32, :] = jnp.where(i == 0, 0.0, _glu(ph_ref[:, D_SHIFT:]))
        ext[32:, :] = glu
        _shift_copies(ext, shifted, tm)
        acc = jnp.zeros((tm, DR), F32)
        for j in range(CW):
            acc = acc + _window(ext, shifted, 2 + j, tm) * dw_ref[j:j + 1, :]
        c = acc + cb_ref[...]
        glu_ref[...] = glu
        c_ref[...] = c
        ob_ref[...] = _ln_silu(c, lw_ref[...], lb_ref[...]).astype(BF16)

    tile = lambda w: pl.BlockSpec((tm, w), lambda i: (i, 0))
    const = lambda a: pl.BlockSpec(a.shape, lambda i: (0, 0))
    halo = pl.BlockSpec((32, D_IN), lambda i: (jnp.maximum(i * (tm // 32) - 1, 0), 0))
    return pl.pallas_call(
        kern, grid=(t_len // tm,), in_specs=[tile(D_IN), halo, const(dw32), const(cb), const(lw), const(lb)],
        out_specs=[tile(DR)] * 3,
        out_shape=[jax.ShapeDtypeStruct((t_len, DR), F32)] * 2 + [jax.ShapeDtypeStruct((t_len, DR), BF16)],
        scratch_shapes=[pltpu.VMEM((tm + 32, DR), F32), pltpu.VMEM((7, tm + 24, DR), F32)], name="conv_fwd",
        compiler_params=_params(("arbitrary",)))(p, p, dw32, cb, lw, lb)


def _conv_bwd(dc, glu, p, dw32, tm):
    t_len = p.shape[0]
    n_tiles = t_len // tm

    def kern(dc_ref, dcn_ref, glu_ref, gluh_ref, p_ref, dw_ref, dpc_ref, ddw_ref, ext_d, ext_g, shifted_d, shifted_g):
        i = pl.program_id(0)

        @pl.when(i == 0)
        def _():
            ddw_ref[...] = jnp.zeros(ddw_ref.shape, F32)

        dcv = dc_ref[...]
        ext_d[0:tm, :] = dcv
        ext_d[tm:, :] = jnp.where(i == n_tiles - 1, 0.0, dcn_ref[...])
        ext_g[0:32, :] = jnp.where(i == 0, 0.0, gluh_ref[...])
        ext_g[32:, :] = glu_ref[...]
        _shift_copies(ext_d, shifted_d, tm)
        _shift_copies(ext_g, shifted_g, tm)
        dglu = jnp.zeros((tm, DR), F32)
        for j in range(CW):
            dglu = dglu + _window(ext_d, shifted_d, 30 - j, tm) * dw_ref[j:j + 1, :]
            ddw_ref[j:j + 1, :] += _colsum(dcv * _window(ext_g, shifted_g, 2 + j, tm))
        pc = p_ref[:, D_SHIFT:]
        sg = jax.nn.sigmoid(pc[:, DR:])
        dpc_ref[...] = jnp.concatenate([dglu * sg, dglu * pc[:, :DR] * sg * (1.0 - sg)], axis=1).astype(BF16)

    tile = lambda w: pl.BlockSpec((tm, w), lambda i: (i, 0))
    nxt = pl.BlockSpec((32, DR), lambda i: (jnp.minimum((i + 1) * (tm // 32), t_len // 32 - 1), 0))
    prv = pl.BlockSpec((32, DR), lambda i: (jnp.maximum(i * (tm // 32) - 1, 0), 0))
    return pl.pallas_call(
        kern, grid=(n_tiles,), in_specs=[tile(DR), nxt, tile(DR), prv, tile(D_IN), pl.BlockSpec((32, DR), lambda i: (0, 0))],
        out_specs=[tile(D), pl.BlockSpec((32, DR), lambda i: (0, 0))],
        out_shape=[jax.ShapeDtypeStruct((t_len, D), BF16), jax.ShapeDtypeStruct((32, DR), F32)],
        scratch_shapes=[pltpu.VMEM((tm + 32, DR), F32)] * 2 + [pltpu.VMEM((7, tm + 24, DR), F32)] * 2, name="conv_bwd",
        compiler_params=_params(("arbitrary",)))(dc, dc, glu, glu, p, dw32)


def _place():
    x, y, c = lax.axis_index("x"), lax.axis_index("y"), lax.axis_index("c")
    chips = [(1 - x, y), (x, 1 - y), (1 - x, 1 - y)]
    return x, y, c, chips


def _gather_weights(shards):
    n = len(shards)

    def body(*refs):
        start, middle, finish = _gather_plan(refs[:n], refs[n:2 * n], *refs[2 * n:])
        start()
        middle()
        finish()

    got = pl.pallas_call(body, in_specs=[ANY] * n, out_specs=[ANY] * n, out_shape=_gathered_shapes(shards),
                         scratch_shapes=_gather_sems(n), name="gather_weights")(*shards)
    return _fill_own(got, shards)


def _gathered_shapes(shards):
    return [jax.ShapeDtypeStruct((4,) + s.shape, s.dtype) for s in shards]


def _gather_sems(n):
    return [pltpu.SemaphoreType.DMA((6 * n,)), pltpu.SemaphoreType.DMA((6 * n,))]


def _fill_own(got, shards):
    me = 2 * lax.axis_index("x") + lax.axis_index("y")
    return [lax.dynamic_update_slice(g, s[None], (me, 0, 0, 0)) for g, s in zip(got, shards)]


def _gather_plan(src, dst, send, recv):
    n = len(src)
    x, y, c, chips = _place()
    me, sib = 2 * x + y, (x, y, 1 - c)

    def rcopy(k, sem, s_ref, d_ref, to):
        return pltpu.make_async_remote_copy(src_ref=s_ref, dst_ref=d_ref, send_sem=send.at[6 * k + sem],
                                            recv_sem=recv.at[6 * k + sem], device_id=to, device_id_type=MESH)

    def landed(k, m, half):
        return dst[k].at[2 * chips[m][0] + chips[m][1], half]

    first = [rcopy(k, m, src[k].at[c], dst[k].at[me, c], (*chips[m], c)) for k in range(n) for m in range(3)]
    passed = [rcopy(k, 3 + m, landed(k, m, c), landed(k, m, c), sib) for k in range(n) for m in range(3)]

    def start():
        for cp in first:
            cp.start()

    def middle():
        for k in range(n):
            for m in range(3):
                rcopy(k, m, landed(k, m, c), landed(k, m, c), sib).wait_recv()
                passed[3 * k + m].start()

    def finish():
        for k in range(n):
            for m in range(3):
                rcopy(k, 3 + m, landed(k, m, 1 - c), landed(k, m, 1 - c), sib).wait_recv()
        for cp in first + passed:
            cp.wait_send()

    return start, middle, finish


def _swap_halves(name, give):
    n = len(give)

    def body(*refs):
        src, got = refs[:n], refs[n:2 * n]
        send, recv = refs[2 * n:]
        x, y, c, _ = _place()
        copies = []
        for k in range(n):
            for j in range(4):
                copies.append(pltpu.make_async_remote_copy(
                    src_ref=src[k].at[j], dst_ref=got[k].at[j], send_sem=send.at[4 * k + j], recv_sem=recv.at[4 * k + j],
                    device_id=(x, y, 1 - c), device_id_type=MESH))
                copies[-1].start()
        for cp in copies:
            cp.wait()

    out_shape = [jax.ShapeDtypeStruct(g.shape, g.dtype) for g in give]
    return pl.pallas_call(body, in_specs=[ANY] * n, out_specs=[ANY] * n, out_shape=out_shape,
                          scratch_shapes=[pltpu.SemaphoreType.DMA((4 * n,)), pltpu.SemaphoreType.DMA((4 * n,))],
                          name=name)(*give)


def _to_owners(sums):
    n = len(sums)

    def body(*refs):
        start, finish = _owners_plan(refs[:n], refs[n:2 * n], *refs[2 * n:])
        start()
        finish()

    return pl.pallas_call(body, in_specs=[ANY] * n, out_specs=[ANY] * n, out_shape=_owner_shapes(sums),
                          scratch_shapes=_owner_sems(n), name="to_owners")(*sums)


def _owner_shapes(sums):
    return [jax.ShapeDtypeStruct((3,) + s.shape[1:], s.dtype) for s in sums]


def _owner_sems(n):
    return [pltpu.SemaphoreType.DMA((3 * n,)), pltpu.SemaphoreType.DMA((3 * n,))]


def _owners_plan(src, dst, send, recv):
    x, y, c, chips = _place()
    copies = [pltpu.make_async_remote_copy(
        src_ref=src[k].at[2 * chip[0] + chip[1]], dst_ref=dst[k].at[m], send_sem=send.at[3 * k + m],
        recv_sem=recv.at[3 * k + m], device_id=(*chip, c), device_id_type=MESH)
        for k in range(len(src)) for m, chip in enumerate(chips)]

    def start():
        for cp in copies:
            cp.start()

    def finish():
        for cp in copies:
            cp.wait()

    return start, finish


def _join_halves(halves):
    n = len(halves)

    def body(*refs):
        src, dst = refs[:n], refs[n:2 * n]
        send, recv = refs[2 * n:]
        x, y, c, _ = _place()
        copies = []
        for k in range(n):
            copies.append(pltpu.make_async_remote_copy(src_ref=src[k], dst_ref=dst[k], send_sem=send.at[k],
                                                       recv_sem=recv.at[k], device_id=(x, y, 1 - c), device_id_type=MESH))
            copies[-1].start()
        for cp in copies:
            cp.wait()

    out_shape = [jax.ShapeDtypeStruct(h.shape, h.dtype) for h in halves]
    return pl.pallas_call(body, in_specs=[ANY] * n, out_specs=[ANY] * n, out_shape=out_shape,
                          scratch_shapes=[pltpu.SemaphoreType.DMA((n,)), pltpu.SemaphoreType.DMA((n,))],
                          name="join_halves")(*halves)


def _allreduce_small(v):
    rows, n = v.shape

    def body(v_ref, o_ref, buf, send, recv):
        x, y, c, _ = _place()
        me = 4 * x + 2 * y + c
        buf[me] = v_ref[...]
        copies = []
        for d in range(1, 8):
            peer = (x ^ (d >> 2), y ^ ((d >> 1) & 1), c ^ (d & 1))
            cp = pltpu.make_async_remote_copy(src_ref=v_ref, dst_ref=buf.at[me], send_sem=send.at[d], recv_sem=recv.at[d],
                                              device_id=peer, device_id_type=MESH)
            cp.start()
            copies.append(cp)
        for d in range(1, 8):
            peer = 4 * (x ^ (d >> 2)) + 2 * (y ^ ((d >> 1) & 1)) + (c ^ (d & 1))
            pltpu.make_async_remote_copy(src_ref=v_ref, dst_ref=buf.at[peer], send_sem=send.at[d], recv_sem=recv.at[d],
                                         device_id=(x, y, c), device_id_type=MESH).wait_recv()
        for cp in copies:
            cp.wait_send()
        acc = buf[0]
        for d in range(1, 8):
            acc = acc + buf[d]
        o_ref[...] = acc

    vm = pl.BlockSpec(memory_space=pltpu.VMEM)
    return pl.pallas_call(body, in_specs=[vm], out_specs=vm, out_shape=jax.ShapeDtypeStruct((rows, n), F32),
                          scratch_shapes=[pltpu.VMEM((8, rows, n), F32), pltpu.SemaphoreType.DMA((8,)),
                                          pltpu.SemaphoreType.DMA((8,))], name="allreduce_small")(v)


def _add_call(name, parts, tm, out_dtype=F32):
    def fn(i, *vals):
        acc = vals[0].astype(F32)
        for v in vals[1:]:
            acc = acc + v.astype(F32)
        return (acc,), ()
    return _rows_call(name, fn, list(parts), [], [(parts[0].shape[1], out_dtype)], [], tm)[0]


def _adamw(name, w, g, m, v):
    c1 = 1.0 / (1.0 - ADAM_B1 ** ADAM_STEP)
    c2 = 1.0 / (1.0 - ADAM_B2 ** ADAM_STEP)

    def fn(i, wv, gv, mv, vv):
        m2 = ADAM_B1 * mv + (1.0 - ADAM_B1) * gv
        v2 = ADAM_B2 * vv + (1.0 - ADAM_B2) * jnp.square(gv)
        delta = -ADAM_LR * ((m2 * c1) / (jnp.sqrt(v2 * c2) + ADAM_EPS) + ADAM_WD * wv)
        return (delta, m2, v2), ()
    cols = w.shape[1]
    tm = _tile(w.shape[0], (256, 176, 128, 64, 8))
    return _rows_call(name, fn, [w, g, m, v], [], [(cols, F32)] * 3, [], tm)


def _canon(name, a, transposed):
    return a[0].T if transposed else a[0]


def _pack_sharded(w_up, a_up, g_up, conv_dw):
    parts = [w_up[0].T, a_up[0].T, g_up[0].T, conv_dw[0].T]
    used = sum(p.shape[1] for p in parts)
    return jnp.concatenate(parts + [jnp.zeros((parts[0].shape[0], PACK_W - used), F32)], axis=1)


def _unpack_sharded(a):
    return [a[:, 0:64].T[None], a[:, 64:128].T[None], a[:, 128:256].T[None], a[:, 256:256 + CW].T[None]]


def _pack_small(vals):
    flat = jnp.concatenate([v.reshape(-1) for v in vals] + [jnp.zeros((SMALL_PAD - SMALL_N,), F32)])
    return flat.reshape(8, SMALL_PAD // 8)


def _unpack_small(a, like):
    flat, out, off = a.reshape(-1), [], 0
    for (_, n), ref in zip(SMALL, like):
        out.append(flat[off:off + n].reshape(ref.shape))
        off += n
    return out


def kernel(x, ffn1_norm_pre, ffn1_norm_post, ffn1_w_gu, ffn1_w_down, mix_norm_pre, mix_norm_post, w_in, shift_mu, w_up, w0, a_up, a0, g_up, k_k, k_a, r_k, gn_w, gn_b, conv_dw, conv_b, conv_ln_w, conv_ln_b, w_out, ffn2_norm_pre, ffn2_norm_post, ffn2_w_gu, ffn2_w_down, loss_target, m_ffn1_norm_pre, m_ffn1_norm_post, m_ffn1_w_gu, m_ffn1_w_down, m_mix_norm_pre, m_mix_norm_post, m_w_in, m_shift_mu, m_w_up, m_w0, m_a_up, m_a0, m_g_up, m_k_k, m_k_a, m_r_k, m_gn_w, m_gn_b, m_conv_dw, m_conv_b, m_conv_ln_w, m_conv_ln_b, m_w_out, m_ffn2_norm_pre, m_ffn2_norm_post, m_ffn2_w_gu, m_ffn2_w_down, v_ffn1_norm_pre, v_ffn1_norm_post, v_ffn1_w_gu, v_ffn1_w_down, v_mix_norm_pre, v_mix_norm_post, v_w_in, v_shift_mu, v_w_up, v_w0, v_a_up, v_a0, v_g_up, v_k_k, v_k_a, v_r_k, v_gn_w, v_gn_b, v_conv_dw, v_conv_b, v_conv_ln_w, v_conv_ln_b, v_w_out, v_ffn2_norm_pre, v_ffn2_norm_post, v_ffn2_w_gu, v_ffn2_w_down):
    w = dict(ffn1_norm_pre=ffn1_norm_pre, ffn1_norm_post=ffn1_norm_post, ffn1_w_gu=ffn1_w_gu, ffn1_w_down=ffn1_w_down, mix_norm_pre=mix_norm_pre, mix_norm_post=mix_norm_post, w_in=w_in, shift_mu=shift_mu, w_up=w_up, w0=w0, a_up=a_up, a0=a0, g_up=g_up, k_k=k_k, k_a=k_a, r_k=r_k, gn_w=gn_w, gn_b=gn_b, conv_dw=conv_dw, conv_b=conv_b, conv_ln_w=conv_ln_w, conv_ln_b=conv_ln_b, w_out=w_out, ffn2_norm_pre=ffn2_norm_pre, ffn2_norm_post=ffn2_norm_post, ffn2_w_gu=ffn2_w_gu, ffn2_w_down=ffn2_w_down)
    mom = dict(ffn1_norm_pre=m_ffn1_norm_pre, ffn1_norm_post=m_ffn1_norm_post, ffn1_w_gu=m_ffn1_w_gu, ffn1_w_down=m_ffn1_w_down, mix_norm_pre=m_mix_norm_pre, mix_norm_post=m_mix_norm_post, w_in=m_w_in, shift_mu=m_shift_mu, w_up=m_w_up, w0=m_w0, a_up=m_a_up, a0=m_a0, g_up=m_g_up, k_k=m_k_k, k_a=m_k_a, r_k=m_r_k, gn_w=m_gn_w, gn_b=m_gn_b, conv_dw=m_conv_dw, conv_b=m_conv_b, conv_ln_w=m_conv_ln_w, conv_ln_b=m_conv_ln_b, w_out=m_w_out, ffn2_norm_pre=m_ffn2_norm_pre, ffn2_norm_post=m_ffn2_norm_post, ffn2_w_gu=m_ffn2_w_gu, ffn2_w_down=m_ffn2_w_down)
    var = dict(ffn1_norm_pre=v_ffn1_norm_pre, ffn1_norm_post=v_ffn1_norm_post, ffn1_w_gu=v_ffn1_w_gu, ffn1_w_down=v_ffn1_w_down, mix_norm_pre=v_mix_norm_pre, mix_norm_post=v_mix_norm_post, w_in=v_w_in, shift_mu=v_shift_mu, w_up=v_w_up, w0=v_w0, a_up=v_a_up, a0=v_a0, g_up=v_g_up, k_k=v_k_k, k_a=v_k_a, r_k=v_r_k, gn_w=v_gn_w, gn_b=v_gn_b, conv_dw=v_conv_dw, conv_b=v_conv_b, conv_ln_w=v_conv_ln_w, conv_ln_b=v_conv_ln_b, w_out=v_w_out, ffn2_norm_pre=v_ffn2_norm_pre, ffn2_norm_post=v_ffn2_norm_post, ffn2_w_gu=v_ffn2_w_gu, ffn2_w_down=v_ffn2_w_down)
    names = list(w)

    x0 = x[0]
    tgt = loss_target[0]
    t_len = x0.shape[0]
    tm = _tile(t_len, (512, 256, 128))

    half = {n: rows // 8 for n, rows, _ in BIG}
    pack_local = _pack_sharded(w["w_up"], w["a_up"], w["g_up"], w["conv_dw"])
    shard = {n: _canon(n, w[n], tr).astype(BF16).reshape(2, half[n], D) for n, _, tr in BIG}
    rows_of = {n: rows for n, rows, _ in BIG}
    gathered = _gather_weights([shard[n] for n in EARLY] + [pack_local.reshape(2, 64, PACK_W)])
    full = {n: g.reshape(rows_of[n], D) for n, g in zip(EARLY, gathered)}
    pack = gathered[-1].reshape(DR, PACK_W)
    p01, p2 = pack[:, 0:128], pack[:, 128:256]
    dw32 = jnp.concatenate([pack[:, 256:256 + CW].T, jnp.zeros((1, DR), F32)], axis=0)

    h1, gu1, act1, f1 = _ffn_fwd("ffn1", x0, w["ffn1_norm_pre"], full["ffn1_w_gu"], full["ffn1_w_down"], tm)
    (x1,) = _rows_call("ffn1_resid", lambda i, xv, fv, g: ((xv + 0.5 * _rms(fv, g),), ()), [x0, f1],
                       [w["ffn1_norm_post"]], [(D, F32)], [], tm)
    (hm,) = _rows_call("mix_norm", lambda i, xv, g: ((_rms(xv, g),), ()), [x1], [w["mix_norm_pre"]], [(D, BF16)], [], tm)
    p = _matmul("mix_in", hm, full["w_in"], "nt")
    ps = _shift_fwd(p, w["shift_mu"], tm)
    rkc = w["r_k"].reshape(1, DR)
    prep_consts = [w["w0"], w["a0"], w["k_k"], w["k_a"], p01, p2]
    r_, dec, k2, v_, z_, b_, g_ = _rows_call(
        "prep", lambda i, psv, *cs: (_prep(psv, *cs), ()), [ps], prep_consts, [(DR, F32)] * 7, [], min(tm, 256))
    y, ck, late = _rec_fwd(r_, dec, k2, z_, b_, v_, [shard[n] for n in LATE])
    full.update({n: g.reshape(rows_of[n], D) for n, g in zip(LATE, late)})
    glu, cpre, ob = _conv_fwd(p, dw32, w["conv_b"], w["conv_ln_w"], w["conv_ln_b"], tm)
    post_consts = [w["gn_w"], w["gn_b"], rkc]
    (o,) = _rows_call(
        "post", lambda i, yv, rv, kv, vv, gv, obv, *cs: ((jnp.concatenate([_post(yv, rv, kv, vv, gv, *cs).astype(BF16), obv], axis=1),), ()),
        [y, r_, k2, v_, g_, ob], post_consts, [(D, BF16)], [], min(tm, 256))
    mo = _matmul("mix_out", o, full["w_out"], "nn")
    (x2,) = _rows_call("mix_resid", lambda i, xv, fv, g: ((xv + _rms(fv, g),), ()), [x1, mo],
                       [w["mix_norm_post"]], [(D, F32)], [], tm)
    h2, gu2, act2, f2 = _ffn_fwd("ffn2", x2, w["ffn2_norm_pre"], full["ffn2_w_gu"], full["ffn2_w_down"], tm)

    def loss_fn(i, xv, fv, tv, g):
        err = xv + 0.5 * _rms(fv, g) - tv
        part = 0.5 * jnp.sum(jnp.mean(err * err, axis=-1, keepdims=True), axis=0, keepdims=True)
        return (err * (1.0 / D),), (jnp.broadcast_to(part, (8, 128)),)
    dx3, loss_part = _rows_call("loss", loss_fn, [x2, f2, tgt], [w["ffn2_norm_post"]], [(D, F32)], [(8, 128)], tm)
    loss = lax.psum(loss_part[0, 0], ("x", "y", "c"))

    g_small = {}
    dx2, g_small["ffn2_norm_pre"], g_small["ffn2_norm_post"], dgu2_t, dwd2, _ = _ffn_bwd(
        "ffn2", x2, w["ffn2_norm_pre"], w["ffn2_norm_post"], 0.5, h2, gu2, act2, f2, dx3, full["ffn2_w_gu"], full["ffn2_w_down"], tm)

    def mix_resid_b(i, fv, dv, g):
        _, vjp = jax.vjp(_rms, fv, g)
        df, dg = vjp(dv)
        return (df,), (dg,)
    dmo, g_small["mix_norm_post"] = _rows_call("mix_resid_b", mix_resid_b, [mo, dx2], [w["mix_norm_post"]],
                                               [(D, BF16)], [(1, D)], tm)
    do = _matmul("mix_do", dmo, full["w_out"], "nt")
    dw_out = _matmul("mix_dwout", o, dmo, "tn")

    def post_b(i, yv, rv, kv, vv, gv, dov, *cs):
        _, vjp = jax.vjp(_post, yv, rv, kv, vv, gv, *cs)
        dy, dr, dk, dv, dg, dgw, dgb, drk = vjp(dov[:, :DR])
        return (dy, dr, dk, dv, dg), (_colsum(dgw), _colsum(dgb), _colsum(drk))
    dy, dr1, dk1, dv1, dg, g_small["gn_w"], g_small["gn_b"], g_small["r_k"] = _rows_call(
        "post_b", post_b, [y, r_, k2, v_, g_, do], post_consts, [(DR, F32)] * 5, [(1, DR)] * 3, min(tm, 256))
    my_c = lax.axis_index("c")
    my_chip = 2 * lax.axis_index("x") + lax.axis_index("y")
    g_big = dict(w_out=dw_out, ffn2_w_gu=dgu2_t, ffn2_w_down=dwd2)

    def halves_of(group, which):
        return jnp.concatenate([lax.dynamic_index_in_dim(g_big[n].reshape(4, 2, half[n], D), which, 1, keepdims=False)
                                for n in group], axis=1)

    def pair_sums(tag, group, extra_mine=(), extra_give=()):
        mine = halves_of(group, my_c)
        got, *got_x = _swap_halves("swap_halves_" + tag, [halves_of(group, 1 - my_c).astype(BF16)] + list(extra_give))
        rows = mine.shape[1]
        tile = _tile(rows, (352, 592, 16))
        travels = _add_call("add_pair_" + tag, [mine.reshape(4 * rows, D), got.reshape(4 * rows, D)], tile, BF16)
        return mine, got, travels.reshape(4, rows, D), got_x

    def owner_sum(tag, mine, got, parts):
        own = [lax.dynamic_index_in_dim(a, my_chip, 0, keepdims=False) for a in (mine, got)]
        return _add_call("add_chips_" + tag, own + [parts[m] for m in range(3)], _tile(mine.shape[1], (352, 592, 16)))

    mine_l, got_l, sum_l, _ = pair_sums("late", LATE)
    (dr2, ddec, dk2, dz, db, dv2), (parts_l,) = _rec_bwd(r_, dec, k2, z_, b_, v_, dy, ck, [sum_l])

    def prep_b(i, psv, a1, a2, c1, c2, e1, e2, dwv, dzv, dbv, dgv, *cs):
        _, vjp = jax.vjp(_prep, psv, *cs)
        dps, dw0, da0, dkk, dka, dp01, dp2 = vjp((a1 + a2, dwv, c1 + c2, e1 + e2, dzv, dbv, dgv))
        return (dps,), (_colsum(dw0), _colsum(da0), _colsum(dkk), _colsum(dka), dp01, dp2)
    dps, g_small["w0"], g_small["a0"], g_small["k_k"], g_small["k_a"], dp01, dp2 = _rows_call(
        "prep_b", prep_b, [ps, dr1, dr2, dk1, dk2, dv1, dv2, ddec, dz, db, dg], prep_consts, [(D_SHIFT, F32)],
        [(1, DR)] * 4 + [(DR, 128)] * 2, min(tm, 256))

    def convln_b(i, cv, dov, lw, lb):
        _, vjp = jax.vjp(_ln_silu, cv, lw, lb)
        dc, dlw, dlb = vjp(dov[:, DR:])
        return (dc,), (_colsum(dc), _colsum(dlw), _colsum(dlb))
    dc, g_small["conv_b"], g_small["conv_ln_w"], g_small["conv_ln_b"] = _rows_call(
        "convln_b", convln_b, [cpre, do], [w["conv_ln_w"], w["conv_ln_b"]], [(DR, F32)], [(1, DR)] * 3, tm)
    dpc, ddw32 = _conv_bwd(dc, glu, p, dw32, tm)
    dp, g_small["shift_mu"] = _shift_bwd(dps, p, dpc, w["shift_mu"], tm)
    dhm = _matmul("mix_dh", dp, full["w_in"], "nn")
    dw_in_t = _matmul("mix_dwin", dp, hm, "tn")

    def norm_b(i, xv, dhv, dv, g):
        _, vjp = jax.vjp(_rms, xv, g)
        dx, dg_ = vjp(dhv)
        return (dx + dv,), (dg_,)
    dx1, g_small["mix_norm_pre"] = _rows_call("mix_norm_b", norm_b, [x1, dhm, dx2], [w["mix_norm_pre"]],
                                              [(D, F32)], [(1, D)], tm)
    early = []

    def early_sums(dgu1_t, dwd1):
        g_big.update(ffn1_w_gu=dgu1_t, ffn1_w_down=dwd1, w_in=dw_in_t)
        pack_grads = jnp.concatenate([dp01, dp2, ddw32.T, jnp.zeros((DR, PACK_W - 288), F32)], axis=1).reshape(4, 2, 64, PACK_W)
        mine_p, give_p = (lax.dynamic_index_in_dim(pack_grads, which, 1, keepdims=False) for which in (my_c, 1 - my_c))
        mine_e, got_e, sum_e, (got_p,) = pair_sums("early", EARLY, extra_give=[give_p])
        sum_p = _add_call("add_pair_pack", [mine_p.reshape(256, PACK_W), got_p.reshape(256, PACK_W)], 256).reshape(4, 64, PACK_W)
        early.extend([mine_e, got_e, sum_p])
        return [sum_e, sum_p]

    grad_x, g_small["ffn1_norm_pre"], g_small["ffn1_norm_post"], _, _, (parts_e, parts_p) = _ffn_bwd(
        "ffn1", x0, w["ffn1_norm_pre"], w["ffn1_norm_post"], 0.5, h1, gu1, act1, f1, dx1, full["ffn1_w_gu"], full["ffn1_w_down"], tm,
        to_owners=early_sums)

    mine_e, got_e, sum_p = early
    fin_e = owner_sum("early", mine_e, got_e, parts_e)
    fin_l = owner_sum("late", mine_l, got_l, parts_l)
    fin_p = _add_call("add_chips_pack", [lax.dynamic_index_in_dim(sum_p, my_chip, 0, keepdims=False)] + [parts_p[m] for m in range(3)], 64)
    fins = [fin_e, fin_l, fin_p]
    red_e, red_l, red_p = [jnp.where(my_c == 0, jnp.stack([f, s]), jnp.stack([s, f])) for f, s in zip(fins, _join_halves(fins))]
    small_sum = _allreduce_small(_pack_small([g_small[n] for n, _ in SMALL]))

    grads, delta, new_m, new_v = {}, {}, {}, {}
    reduced = {}
    for group, red in ((EARLY, red_e), (LATE, red_l)):
        off = 0
        for n in group:
            reduced[n] = red[:, off:off + half[n], :].reshape(rows_of[n] // 4, D)
            off += half[n]
    for n, rows, tr in BIG:
        g = reduced[n]
        g = (g.T if tr else g)[None]
        grads[n] = g
        d_, m_, v2_ = _adamw("adamw_" + n, w[n][0], g[0], mom[n][0], var[n][0])
        delta[n], new_m[n], new_v[n] = d_[None], m_[None], v2_[None]
    sh = ("w_up", "a_up", "g_up", "conv_dw")
    g_pack = red_p.reshape(128, PACK_W)
    d_, m_, v2_ = _adamw("adamw_pack", pack_local, g_pack, _pack_sharded(*[mom[n] for n in sh]), _pack_sharded(*[var[n] for n in sh]))
    for dst, src in ((grads, g_pack), (delta, d_), (new_m, m_), (new_v, v2_)):
        for n, a in zip(sh, _unpack_sharded(src)):
            dst[n] = a
    sm = [n for n, _ in SMALL]
    d_, m_, v2_ = _adamw("adamw_small", _pack_small([w[n] for n in sm]), small_sum, _pack_small([mom[n] for n in sm]),
                         _pack_small([var[n] for n in sm]))
    like = [w[n] for n in sm]
    for dst, src in ((grads, small_sum), (delta, d_), (new_m, m_), (new_v, v2_)):
        for n, a in zip(sm, _unpack_small(src, like)):
            dst[n] = a

    wn = names
    return (loss, grad_x[None], *[grads[n] for n in wn], *[delta[n] for n in wn], *[new_m[n] for n in wn],
            *[new_v[n] for n in wn])
```

```python
import functools
import math

import jax
import jax.numpy as jnp
from jax import lax
from jax.experimental import pallas as pl
from jax.experimental.pallas import tpu as pltpu

F32 = jnp.float32
BF16 = jnp.bfloat16

D = 1024
DFF = 2816
DR = 512
HS = 64
D_SHIFT = 1792
D_IN = 2816
CW = 31
RMS_EPS = 1e-6
GN_EPS = 64e-5
LN_EPS = 1e-5
DECAY_SCALE = math.exp(-0.5)
ADAM_LR, ADAM_B1, ADAM_B2, ADAM_EPS, ADAM_WD, ADAM_STEP = 0.001, 0.9, 0.999, 1e-8, 0.01, 10

REC_TILE = 128
VMEM_LIMIT = 56 * 1024 * 1024
MESH = pl.DeviceIdType.MESH
ANY = pl.BlockSpec(memory_space=pl.ANY)

BIG = (("ffn1_w_gu", 2 * DFF, True), ("ffn1_w_down", DFF, False), ("w_in", D_IN, True),
       ("w_out", D, False), ("ffn2_w_gu", 2 * DFF, True), ("ffn2_w_down", DFF, False))
SECOND = ("ffn1_w_down", "w_in")
LATE = ("w_out", "ffn2_w_gu", "ffn2_w_down")
PACK_W = 384
SMALL = (("ffn1_norm_pre", D), ("ffn1_norm_post", D), ("mix_norm_pre", D), ("mix_norm_post", D),
         ("shift_mu", D_SHIFT), ("w0", DR), ("a0", DR), ("k_k", DR), ("k_a", DR), ("r_k", DR),
         ("gn_w", DR), ("gn_b", DR), ("conv_b", DR), ("conv_ln_w", DR), ("conv_ln_b", DR),
         ("ffn2_norm_pre", D), ("ffn2_norm_post", D))
SMALL_N = sum(n for _, n in SMALL)
SMALL_PAD = 8 * 1664


def _params(sem):
    return pltpu.CompilerParams(dimension_semantics=sem, vmem_limit_bytes=VMEM_LIMIT)


def _tile(n, prefs):
    for p in prefs:
        if n % p == 0:
            return p
    return n


def _rows_call(name, fn, rows, consts, row_outs, acc_outs, tm):
    specs, arrs = [], []
    for r in rows:
        if isinstance(r, tuple):
            a, bs, im = r
            specs.append(pl.BlockSpec(bs, im))
        else:
            a = r
            specs.append(pl.BlockSpec((tm, a.shape[1]), lambda i: (i, 0)))
        arrs.append(a)
    t_rows = arrs[0].shape[0]
    for c in consts:
        specs.append(pl.BlockSpec(c.shape, functools.partial(lambda i, n: (0,) * n, n=c.ndim)))
        arrs.append(c)
    n_in, n_o, n_a = len(arrs), len(row_outs), len(acc_outs)

    def kern(*refs):
        i = pl.program_id(0)
        vals = [r[...] for r in refs[:n_in]]
        ro, ao = fn(i, *vals)
        outs = refs[n_in:]
        for k in range(n_o):
            outs[k][...] = ro[k].astype(outs[k].dtype)
        if n_a:
            @pl.when(i == 0)
            def _():
                for k in range(n_a):
                    outs[n_o + k][...] = jnp.zeros(outs[n_o + k].shape, F32)
            for k in range(n_a):
                outs[n_o + k][...] += ao[k]

    out_specs = [pl.BlockSpec((tm, w), lambda i: (i, 0)) for (w, _) in row_outs]
    out_specs += [pl.BlockSpec(s, functools.partial(lambda i, n: (0,) * n, n=len(s))) for s in acc_outs]
    out_shape = [jax.ShapeDtypeStruct((t_rows, w), dt) for (w, dt) in row_outs]
    out_shape += [jax.ShapeDtypeStruct(s, F32) for s in acc_outs]
    return pl.pallas_call(kern, grid=(t_rows // tm,), in_specs=specs, out_specs=out_specs, out_shape=out_shape,
                          name=name, compiler_params=_params(("arbitrary",)))(*arrs)


_DIMS = {"nn": (((1,), (0,)), ((), ())), "nt": (((1,), (1,)), ((), ())), "tn": (((0,), (0,)), ((), ()))}
_LANE_TILES = (1408, 1024, 512, 384, 256, 128)


def _matmul(name, a, b, mode, out_dtype=F32, owners=None, gather=None):
    if mode == "nn":
        (m, k), (_, n) = a.shape, b.shape
    elif mode == "nt":
        (m, k), (n, _) = a.shape, b.shape
    else:
        (k, m), (_, n) = a.shape, b.shape
    if mode == "tn":
        tm, tk = _tile(m, _LANE_TILES), _tile(k, (512, 256, 128))
    else:
        tm, tk = _tile(m, (1024, 512, 256, 128)), _tile(k, _LANE_TILES)
    tn = _tile(n, _LANE_TILES)
    nk = k // tk
    assert out_dtype == F32 or nk == 1

    owners, gather = list(owners or ()), list(gather or ())
    assert not (owners and gather)
    riders = owners + gather
    ns = len(riders)
    grid = (m // tm, n // tn, nk)
    steps = grid[0] * grid[1] * nk

    def kern(a_ref, b_ref, *rest):
        o_ref = rest[ns]
        if ns:
            step = (pl.program_id(0) * grid[1] + pl.program_id(1)) * nk + pl.program_id(2)
            if owners:
                start, finish = _owners_plan(rest[:ns], rest[ns + 1:2 * ns + 1], *rest[2 * ns + 1:])
                pl.when(step == 0)(start)
            else:
                start, middle, finish = _gather_plan(rest[:ns], rest[ns + 1:2 * ns + 1], *rest[2 * ns + 1:])
                pl.when(step == 0)(start)
                pl.when(step == steps // 2)(middle)

        def part():
            return lax.dot_general(a_ref[...].astype(BF16), b_ref[...].astype(BF16), _DIMS[mode], preferred_element_type=F32)

        if nk == 1:
            o_ref[...] = part().astype(o_ref.dtype)
        else:
            @pl.when(pl.program_id(2) == 0)
            def _():
                o_ref[...] = jnp.zeros(o_ref.shape, F32)

            o_ref[...] += part()
        if ns:
            pl.when(step == steps - 1)(finish)

    a_spec = pl.BlockSpec((tk, tm), lambda i, j, q: (q, i)) if mode == "tn" else pl.BlockSpec((tm, tk), lambda i, j, q: (i, q))
    b_spec = pl.BlockSpec((tn, tk), lambda i, j, q: (j, q)) if mode == "nt" else pl.BlockSpec((tk, tn), lambda i, j, q: (q, j))
    extra_shapes = _owner_shapes(owners) if owners else _gathered_shapes(gather)
    extra_sems = [] if not ns else _owner_sems(ns) if owners else _gather_sems(ns)
    out, *arrived = pl.pallas_call(
        kern, grid=grid, in_specs=[a_spec, b_spec] + [ANY] * ns,
        out_specs=[pl.BlockSpec((tm, tn), lambda i, j, q: (i, j))] + [ANY] * ns,
        out_shape=[jax.ShapeDtypeStruct((m, n), out_dtype)] + extra_shapes,
        scratch_shapes=extra_sems, name=name,
        compiler_params=_params(("arbitrary", "arbitrary", "arbitrary")))(a, b, *riders)
    if gather:
        arrived = _fill_own(arrived, gather)
    return (out, arrived) if ns else out


def _rms(x, g):
    return x * lax.rsqrt(jnp.mean(x * x, axis=-1, keepdims=True) + RMS_EPS) * g


def _silu(x):
    return x * jax.nn.sigmoid(x)


def _bones(n):
    r = lax.broadcasted_iota(jnp.int32, (n, n), 0) // HS
    c = lax.broadcasted_iota(jnp.int32, (n, n), 1) // HS
    return (r == c).astype(BF16)


@jax.custom_vjp
def _segsum(x):
    bones = _bones(x.shape[1])
    hi = x.astype(BF16)
    lo = (x - hi.astype(F32)).astype(BF16)
    return jnp.dot(hi, bones, preferred_element_type=F32) + jnp.dot(lo, bones, preferred_element_type=F32)


_segsum.defvjp(lambda x: (_segsum(x), None), lambda _, ct: (_segsum(ct),))


@jax.custom_vjp
def _dot_nt(x, w):
    return lax.dot_general(x.astype(BF16), w.astype(BF16), _DIMS["nt"], preferred_element_type=F32)


def _dot_nt_bwd(res, ct):
    x, w = res
    ctb = ct.astype(BF16)
    dx = lax.dot_general(ctb, w.astype(BF16), _DIMS["nn"], preferred_element_type=F32)
    dw = lax.dot_general(ctb, x.astype(BF16), _DIMS["tn"], preferred_element_type=F32)
    return dx, dw


_dot_nt.defvjp(lambda x, w: (_dot_nt(x, w), (x, w)), _dot_nt_bwd)


def _swiglu(gu):
    return _silu(gu[:, :DFF]) * gu[:, DFF:]


def _prep(ps, w0, a0, k_k, k_a, p01, p2):
    r, k, v = ps[:, :DR], ps[:, DR:2 * DR], ps[:, 2 * DR:3 * DR]
    wa, xg = ps[:, 3 * DR:3 * DR + 128], ps[:, 3 * DR + 128:]
    first = lax.broadcasted_iota(jnp.int32, (1, 128), 1) < 64
    d = w0 + _dot_nt(jnp.where(first, jnp.tanh(wa), 0.0), p01)
    decay = jnp.exp(-DECAY_SCALE * jax.nn.sigmoid(d))
    a = jax.nn.sigmoid(a0 + _dot_nt(jnp.where(first, 0.0, wa), p01))
    g = _dot_nt(jax.nn.sigmoid(xg), p2)
    kk = k * k_k
    kk = kk * lax.rsqrt(jnp.maximum(_segsum(kk * kk), 1e-12))
    k2 = k * (1.0 + (a - 1.0) * k_a)
    return r, decay, k2, v, -kk, kk * a, g


def _post(y, r, k, v, g, gn_w, gn_b, r_k):
    mu = _segsum(y) * (1.0 / HS)
    yc = y - mu
    var = _segsum(yc * yc) * (1.0 / HS)
    yn = yc * lax.rsqrt(var + GN_EPS) * gn_w + gn_b
    return (yn + _segsum(r * k * r_k) * v) * g


def _ln_silu(c, w, b):
    mu = jnp.mean(c, axis=-1, keepdims=True)
    var = jnp.mean(jnp.square(c - mu), axis=-1, keepdims=True)
    return _silu((c - mu) * lax.rsqrt(var + LN_EPS) * w + b)


def _colsum(x):
    return jnp.sum(x, axis=0, keepdims=True)


def _ffn_fwd(tag, x, pre, w_gu_t, w_down, tm, gather=None):
    (h,) = _rows_call(tag + "_norm", lambda i, xv, g: ((_rms(xv, g),), ()), [x], [pre], [(D, BF16)], [], tm)
    gu = _matmul(tag + "_gu", h, w_gu_t, "nt", gather=gather)
    gu, gathered = gu if gather else (gu, [])
    if gather:
        w_down = gathered[0].reshape(DFF, D)
    (act,) = _rows_call(tag + "_act", lambda i, v: ((_swiglu(v),), ()), [gu], [], [(DFF, BF16)], [], min(tm, 256))
    f = _matmul(tag + "_down", act, w_down, "nn")
    return h, gu, act, f, gathered


def _ffn_bwd(tag, x, pre, post, scale, h, gu, act, f, dxo, w_gu_t, w_down, tm, after_down=None, after_gu=None):
    def resid_b(i, fv, dv, g):
        _, vjp = jax.vjp(lambda a, b: scale * _rms(a, b), fv, g)
        df, dg = vjp(dv)
        return (df,), (dg,)
    df, dpost = _rows_call(tag + "_resid_b", resid_b, [f, dxo], [post], [(D, BF16)], [(1, D)], tm)
    dact = _matmul(tag + "_dact", df, w_down, "nt")
    dw_down = _matmul(tag + "_dwdown", act, df, "tn")

    def act_b(i, guv, dv):
        _, vjp = jax.vjp(_swiglu, guv)
        return (vjp(dv)[0],), ()
    (dgu,) = _rows_call(tag + "_act_b", act_b, [gu, dact], [], [(2 * DFF, BF16)], [], min(tm, 256))
    sums = after_down(dw_down) if after_down else []
    dw_gu_t = _matmul(tag + "_dwgu", dgu, h, "tn", owners=sums)
    dw_gu_t, parts_down = dw_gu_t if sums else (dw_gu_t, [])
    sums = after_gu(dw_gu_t) if after_gu else []
    dh = _matmul(tag + "_dh", dgu, w_gu_t, "nn", owners=sums)
    dh, parts_gu = dh if sums else (dh, [])
    parts = (parts_down, parts_gu)

    def norm_b(i, xv, dhv, dv, g):
        _, vjp = jax.vjp(_rms, xv, g)
        dx, dg = vjp(dhv)
        return (dx + dv,), (dg,)
    dx, dpre = _rows_call(tag + "_norm_b", norm_b, [x, dh, dxo], [pre], [(D, F32)], [(1, D)], tm)
    return dx, dpre, dpost, dw_gu_t, dw_down, parts


def _pair_bcast(cols, first):
    return jnp.concatenate([jnp.where(first, cols[2 * p], cols[2 * p + 1]) for p in range(4)], axis=1)


def _head_sums(x, first):
    cols = []
    for p in range(4):
        xp = x[:, 128 * p:128 * (p + 1)]
        cols.append(jnp.sum(jnp.where(first, xp, 0.0), axis=1, keepdims=True))
        cols.append(jnp.sum(jnp.where(first, 0.0, xp), axis=1, keepdims=True))
    return cols


def _split16(x8):
    hi = x8.astype(BF16).astype(F32)
    return jnp.concatenate([hi, x8 - hi], axis=0).astype(BF16)


def _cols8(x8, e16):
    return lax.dot_general(_split16(x8), e16, _DIMS["tn"], preferred_element_type=F32)


def _pair_rows(c, first):
    return jnp.concatenate([jnp.where(first, c[128 * p:128 * p + HS], c[128 * p + HS:128 * (p + 1)]) for p in range(4)], axis=1)


def _rows8(prod, bones, dmask):
    x = prod.astype(BF16)
    full = jnp.concatenate([jnp.dot(x[:, 256 * q:256 * (q + 1)], bones, preferred_element_type=F32) for q in range(2)], axis=1)
    return jnp.concatenate([_colsum(full[HS * j:HS * (j + 1)] * dmask) for j in range(8)], axis=0)


def _rec_step(s, wr, zr, br, kr, vc, first):
    u = _pair_bcast(_head_sums(s * zr, first), first)
    return s * wr + u * br + vc * kr, u


def _rec_consts():
    e16 = (lax.broadcasted_iota(jnp.int32, (16, 1024), 0) % 8 == lax.broadcasted_iota(jnp.int32, (16, 1024), 1) // 128)
    dmask = lax.broadcasted_iota(jnp.int32, (HS, DR), 0) == lax.broadcasted_iota(jnp.int32, (HS, DR), 1) % HS
    return e16.astype(BF16), _bones(256), dmask.astype(F32)


def _const_spec(a):
    return pl.BlockSpec(a.shape, functools.partial(lambda i, n: (0,) * n, n=a.ndim))


def _rec_fwd(r, w, k, z, b, v, shards):
    t_len = r.shape[0]
    nt = t_len // REC_TILE
    ns = len(shards)
    consts = _rec_consts()

    def kern(r_ref, w_ref, k_ref, z_ref, b_ref, v_ref, e_ref, bones_ref, dm_ref, *rest):
        y_ref, ck_ref, states, u_ref = rest[ns:ns + 4]
        s_ref, prod = rest[2 * ns + 4:2 * ns + 6]
        start, middle, finish = _gather_plan(rest[:ns], rest[ns + 4:2 * ns + 4], *rest[2 * ns + 6:])
        i = pl.program_id(0)

        @pl.when(i == 0)
        def _():
            s_ref[...] = jnp.zeros(s_ref.shape, F32)
            start()

        pl.when(i == nt // 2)(middle)
        ck_ref[0] = s_ref[...]
        first = lax.broadcasted_iota(jnp.int32, (1, 128), 1) < HS

        def group(g8, s):
            base = pl.multiple_of(g8 * 8, 8)
            r8, w8, k8, z8, b8, v8 = (ref[pl.ds(base, 8), :] for ref in (r_ref, w_ref, k_ref, z_ref, b_ref, v_ref))
            vcols = _cols8(v8, e_ref[...])
            urows = []
            for j in range(8):
                vc = _pair_rows(vcols[:, 128 * j:128 * (j + 1)], first)
                s, u = _rec_step(s, w8[j:j + 1], z8[j:j + 1], b8[j:j + 1], k8[j:j + 1], vc, first)
                states[base + j] = s
                urows.append(_colsum(u * dm_ref[...]))
                prod[HS * j:HS * (j + 1), :] = s * r8[j:j + 1]
            y_ref[pl.ds(base, 8), :] = _rows8(prod[...], bones_ref[...], dm_ref[...])
            u_ref[pl.ds(base, 8), :] = jnp.concatenate(urows, axis=0)
            return s

        s_ref[...] = lax.fori_loop(0, REC_TILE // 8, group, s_ref[...])
        pl.when(i == nt - 1)(finish)

    row = pl.BlockSpec((REC_TILE, DR), lambda i: (i, 0))
    y, ck, states, u, *got = pl.pallas_call(
        kern, grid=(nt,), in_specs=[row] * 6 + [_const_spec(c) for c in consts] + [ANY] * ns,
        out_specs=[row, pl.BlockSpec((1, HS, DR), lambda i: (i, 0, 0)), pl.BlockSpec((REC_TILE, HS, DR), lambda i: (i, 0, 0)), row]
        + [ANY] * ns,
        out_shape=[jax.ShapeDtypeStruct((t_len, DR), F32), jax.ShapeDtypeStruct((nt, HS, DR), F32),
                   jax.ShapeDtypeStruct((t_len, HS, DR), F32), jax.ShapeDtypeStruct((t_len, DR), F32)] + _gathered_shapes(shards),
        scratch_shapes=[pltpu.VMEM((HS, DR), F32), pltpu.VMEM((8 * HS, DR), F32)] + _gather_sems(ns), name="rec_fwd",
        compiler_params=_params(("arbitrary",)))(r, w, k, z, b, v, *consts, *shards)
    return y, (ck, states, u), _fill_own(got, shards)


def _rec_bwd(r, w, k, z, b, v, dy, saved, sums):
    t_len = r.shape[0]
    nt = t_len // REC_TILE
    ns = len(sums)
    consts = _rec_consts()

    def kern(r_ref, w_ref, k_ref, z_ref, b_ref, v_ref, dy_ref, u_ref, ck_ref, states, e_ref, bones_ref, dm_ref, *rest):
        dr_ref, dw_ref, dk_ref, dz_ref, db_ref, dv_ref = rest[ns:ns + 6]
        ds_ref, prod = rest[2 * ns + 6:2 * ns + 8]
        start, finish = _owners_plan(rest[:ns], rest[ns + 6:2 * ns + 6], *rest[2 * ns + 8:])
        i = pl.program_id(0)

        @pl.when(i == 0)
        def _():
            ds_ref[...] = jnp.zeros(ds_ref.shape, F32)
            start()

        first = lax.broadcasted_iota(jnp.int32, (1, 128), 1) < HS

        def bgroup(gg, ds):
            base = pl.multiple_of((REC_TILE // 8 - 1 - gg) * 8, 8)
            r8, w8, k8, z8, b8, v8, dy8, u8 = (ref[pl.ds(base, 8), :]
                                               for ref in (r_ref, w_ref, k_ref, z_ref, b_ref, v_ref, dy_ref, u_ref))
            vcols = _cols8(v8, e_ref[...])
            dycols = _cols8(dy8, e_ref[...])
            ucols = _cols8(u8, e_ref[...])
            before = jnp.where(base == 0, ck_ref[0], states[jnp.maximum(base - 1, 0)])
            rows = {n: [None] * 8 for n in ("dr", "dw", "dk", "dz", "db")}
            for j in range(7, -1, -1):
                t = base + j
                rr, wr, kr, zr, br = (x[j:j + 1] for x in (r8, w8, k8, z8, b8))
                s_prev, s_t = (states[t - 1] if j else before), states[t]
                dyc = _pair_rows(dycols[:, 128 * j:128 * (j + 1)], first)
                vc = _pair_rows(vcols[:, 128 * j:128 * (j + 1)], first)
                ds = ds + dyc * rr
                rows["dr"][j] = _colsum(s_t * dyc)
                rows["dw"][j] = _colsum(ds * s_prev)
                du = _pair_bcast(_head_sums(ds * br, first), first)
                rows["db"][j] = _colsum(ds * _pair_rows(ucols[:, 128 * j:128 * (j + 1)], first))
                rows["dk"][j] = _colsum(ds * vc)
                prod[HS * j:HS * (j + 1), :] = ds * kr
                rows["dz"][j] = _colsum(s_prev * du)
                ds = ds * wr + du * zr
            for n, ref in (("dr", dr_ref), ("dw", dw_ref), ("dk", dk_ref), ("dz", dz_ref), ("db", db_ref)):
                ref[pl.ds(base, 8), :] = jnp.concatenate(rows[n], axis=0)
            dv_ref[pl.ds(base, 8), :] = _rows8(prod[...], bones_ref[...], dm_ref[...])
            return ds

        ds_ref[...] = lax.fori_loop(0, REC_TILE // 8, bgroup, ds_ref[...])
        pl.when(i == nt - 1)(finish)

    ck, states, u = saved
    row = pl.BlockSpec((REC_TILE, DR), lambda i: (nt - 1 - i, 0))
    outs = pl.pallas_call(
        kern, grid=(nt,),
        in_specs=[row] * 8 + [pl.BlockSpec((1, HS, DR), lambda i: (nt - 1 - i, 0, 0)),
                              pl.BlockSpec((REC_TILE, HS, DR), lambda i: (nt - 1 - i, 0, 0))]
        + [_const_spec(c) for c in consts] + [ANY] * ns,
        out_specs=[row] * 6 + [ANY] * ns, out_shape=[jax.ShapeDtypeStruct((t_len, DR), F32)] * 6 + _owner_shapes(sums),
        scratch_shapes=[pltpu.VMEM((HS, DR), F32), pltpu.VMEM((8 * HS, DR), F32)] + _owner_sems(ns), name="rec_bwd",
        compiler_params=_params(("arbitrary",)))(r, w, k, z, b, v, dy, u, ck, states, *consts, *sums)
    return outs[:6], outs[6:]


def _prev_rows(a, tm, n):
    return (a, (n, a.shape[1]), lambda i: (jnp.maximum(i * (tm // n) - 1, 0), 0))


def _next_rows(a, tm, n):
    last = a.shape[0] // n - 1
    return (a, (n, a.shape[1]), lambda i: (jnp.minimum((i + 1) * (tm // n), last), 0))


def _shifted(x, prev8, i):
    rowid = lax.broadcasted_iota(jnp.int32, x.shape, 0)
    before = jnp.where(i == 0, 0.0, prev8[7:8, :])
    return jnp.where(rowid == 0, before, pltpu.roll(x, 1, 0))


def _shift_fwd(p, mu, tm):
    def fn(i, pv, prev8, muv):
        x = pv[:, :D_SHIFT]
        return (x + (_shifted(x, prev8[:, :D_SHIFT], i) - x) * muv,), ()
    return _rows_call("shift", fn, [p, _prev_rows(p, tm, 8)], [mu], [(D_SHIFT, F32)], [], tm)[0]


def _shift_bwd(dps, p, dpc, mu, tm):
    n_tiles = p.shape[0] // tm

    def fn(i, dv, next8, pv, prev8, dpcv, muv):
        x = pv[:, :D_SHIFT]
        xs = _shifted(x, prev8[:, :D_SHIFT], i)
        rowid = lax.broadcasted_iota(jnp.int32, dv.shape, 0)
        after = jnp.where(i == n_tiles - 1, 0.0, next8[0:1, :])
        dnext = jnp.where(rowid == tm - 1, after, pltpu.roll(dv, tm - 1, 0))
        dp_s = dv * (1.0 - muv) + dnext * muv
        return (jnp.concatenate([dp_s.astype(BF16), dpcv], axis=1),), (_colsum(dv * (xs - x)),)
    return _rows_call("shift_b", fn, [dps, _next_rows(dps, tm, 8), p, _prev_rows(p, tm, 8), dpc], [mu],
                      [(D_IN, BF16)], [(1, D_SHIFT)], tm)


def _glu(pc):
    return pc[:, :DR] * jax.nn.sigmoid(pc[:, DR:])


def _shift_copies(ext, shifted, tm):
    for s in range(1, 8):
        shifted[s - 1] = ext[s:s + tm + 24, :]


def _window(ext, shifted, off, tm):
    if off % 8 == 0:
        return ext[off:off + tm, :]
    return shifted[off % 8 - 1, off // 8 * 8:off // 8 * 8 + tm, :]


def _conv_fwd(p, dw32, cb, lw, lb, tm):
    t_len = p.shape[0]

    def kern(p_ref, ph_ref, dw_ref, cb_ref, lw_ref, lb_ref, glu_ref, c_ref, ob_ref, ext, shifted):
        i = pl.program_id(0)
        glu = _glu(p_ref[:, D_SHIFT:])
        ext[0:32, :] = jnp.where(i == 0, 0.0, _glu(ph_ref[:, D_SHIFT:]))
        ext[32:, :] = glu
        _shift_copies(ext, shifted, tm)
        acc = jnp.zeros((tm, DR), F32)
        for j in range(CW):
            acc = acc + _window(ext, shifted, 2 + j, tm) * dw_ref[j:j + 1, :]
        c = acc + cb_ref[...]
        glu_ref[...] = glu
        c_ref[...] = c
        ob_ref[...] = _ln_silu(c, lw_ref[...], lb_ref[...]).astype(BF16)

    tile = lambda w: pl.BlockSpec((tm, w), lambda i: (i, 0))
    const = lambda a: pl.BlockSpec(a.shape, lambda i: (0, 0))
    halo = pl.BlockSpec((32, D_IN), lambda i: (jnp.maximum(i * (tm // 32) - 1, 0), 0))
    return pl.pallas_call(
        kern, grid=(t_len // tm,), in_specs=[tile(D_IN), halo, const(dw32), const(cb), const(lw), const(lb)],
        out_specs=[tile(DR)] * 3,
        out_shape=[jax.ShapeDtypeStruct((t_len, DR), F32)] * 2 + [jax.ShapeDtypeStruct((t_len, DR), BF16)],
        scratch_shapes=[pltpu.VMEM((tm + 32, DR), F32), pltpu.VMEM((7, tm + 24, DR), F32)], name="conv_fwd",
        compiler_params=_params(("arbitrary",)))(p, p, dw32, cb, lw, lb)


def _conv_bwd(dc, glu, p, dw32, tm):
    t_len = p.shape[0]
    n_tiles = t_len // tm

    def kern(dc_ref, dcn_ref, glu_ref, gluh_ref, p_ref, dw_ref, dpc_ref, ddw_ref, ext_d, ext_g, shifted_d, shifted_g):
        i = pl.program_id(0)

        @pl.when(i == 0)
        def _():
            ddw_ref[...] = jnp.zeros(ddw_ref.shape, F32)

        dcv = dc_ref[...]
        ext_d[0:tm, :] = dcv
        ext_d[tm:, :] = jnp.where(i == n_tiles - 1, 0.0, dcn_ref[...])
        ext_g[0:32, :] = jnp.where(i == 0, 0.0, gluh_ref[...])
        ext_g[32:, :] = glu_ref[...]
        _shift_copies(ext_d, shifted_d, tm)
        _shift_copies(ext_g, shifted_g, tm)
        dglu = jnp.zeros((tm, DR), F32)
        for j in range(CW):
            dglu = dglu + _window(ext_d, shifted_d, 30 - j, tm) * dw_ref[j:j + 1, :]
            ddw_ref[j:j + 1, :] += _colsum(dcv * _window(ext_g, shifted_g, 2 + j, tm))
        pc = p_ref[:, D_SHIFT:]
        sg = jax.nn.sigmoid(pc[:, DR:])
        dpc_ref[...] = jnp.concatenate([dglu * sg, dglu * pc[:, :DR] * sg * (1.0 - sg)], axis=1).astype(BF16)

    tile = lambda w: pl.BlockSpec((tm, w), lambda i: (i, 0))
    nxt = pl.BlockSpec((32, DR), lambda i: (jnp.minimum((i + 1) * (tm // 32), t_len // 32 - 1), 0))
    prv = pl.BlockSpec((32, DR), lambda i: (jnp.maximum(i * (tm // 32) - 1, 0), 0))
    return pl.pallas_call(
        kern, grid=(n_tiles,), in_specs=[tile(DR), nxt, tile(DR), prv, tile(D_IN), pl.BlockSpec((32, DR), lambda i: (0, 0))],
        out_specs=[tile(D), pl.BlockSpec((32, DR), lambda i: (0, 0))],
        out_shape=[jax.ShapeDtypeStruct((t_len, D), BF16), jax.ShapeDtypeStruct((32, DR), F32)],
        scratch_shapes=[pltpu.VMEM((tm + 32, DR), F32)] * 2 + [pltpu.VMEM((7, tm + 24, DR), F32)] * 2, name="conv_bwd",
        compiler_params=_params(("arbitrary",)))(dc, dc, glu, glu, p, dw32)


def _place():
    x, y, c = lax.axis_index("x"), lax.axis_index("y"), lax.axis_index("c")
    chips = [(1 - x, y), (x, 1 - y), (1 - x, 1 - y)]
    return x, y, c, chips


def _gather_weights(shards):
    n = len(shards)

    def body(*refs):
        start, middle, finish = _gather_plan(refs[:n], refs[n:2 * n], *refs[2 * n:])
        start()
        middle()
        finish()

    got = pl.pallas_call(body, in_specs=[ANY] * n, out_specs=[ANY] * n, out_shape=_gathered_shapes(shards),
                         scratch_shapes=_gather_sems(n), name="gather_weights")(*shards)
    return _fill_own(got, shards)


def _gathered_shapes(shards):
    return [jax.ShapeDtypeStruct((4,) + s.shape, s.dtype) for s in shards]


def _gather_sems(n):
    return [pltpu.SemaphoreType.DMA((6 * n,)), pltpu.SemaphoreType.DMA((6 * n,))]


def _fill_own(got, shards):
    me = 2 * lax.axis_index("x") + lax.axis_index("y")
    return [lax.dynamic_update_slice(g, s[None], (me, 0, 0, 0)) for g, s in zip(got, shards)]


def _gather_plan(src, dst, send, recv):
    n = len(src)
    x, y, c, chips = _place()
    me, sib = 2 * x + y, (x, y, 1 - c)

    def rcopy(k, sem, s_ref, d_ref, to):
        return pltpu.make_async_remote_copy(src_ref=s_ref, dst_ref=d_ref, send_sem=send.at[6 * k + sem],
                                            recv_sem=recv.at[6 * k + sem], device_id=to, device_id_type=MESH)

    def landed(k, m, half):
        return dst[k].at[2 * chips[m][0] + chips[m][1], half]

    first = [rcopy(k, m, src[k].at[c], dst[k].at[me, c], (*chips[m], c)) for k in range(n) for m in range(3)]
    passed = [rcopy(k, 3 + m, landed(k, m, c), landed(k, m, c), sib) for k in range(n) for m in range(3)]

    def start():
        for cp in first:
            cp.start()

    def middle():
        for k in range(n):
            for m in range(3):
                rcopy(k, m, landed(k, m, c), landed(k, m, c), sib).wait_recv()
                passed[3 * k + m].start()

    def finish():
        for k in range(n):
            for m in range(3):
                rcopy(k, 3 + m, landed(k, m, 1 - c), landed(k, m, 1 - c), sib).wait_recv()
        for cp in first + passed:
            cp.wait_send()

    return start, middle, finish


def _swap_halves(name, give):
    n = len(give)

    def body(*refs):
        src, got = refs[:n], refs[n:2 * n]
        send, recv = refs[2 * n:]
        x, y, c, _ = _place()
        copies = []
        for k in range(n):
            for j in range(4):
                copies.append(pltpu.make_async_remote_copy(
                    src_ref=src[k].at[j], dst_ref=got[k].at[j], send_sem=send.at[4 * k + j], recv_sem=recv.at[4 * k + j],
                    device_id=(x, y, 1 - c), device_id_type=MESH))
                copies[-1].start()
        for cp in copies:
            cp.wait()

    out_shape = [jax.ShapeDtypeStruct(g.shape, g.dtype) for g in give]
    return pl.pallas_call(body, in_specs=[ANY] * n, out_specs=[ANY] * n, out_shape=out_shape,
                          scratch_shapes=[pltpu.SemaphoreType.DMA((4 * n,)), pltpu.SemaphoreType.DMA((4 * n,))],
                          name=name)(*give)


def _owner_shapes(sums):
    return [jax.ShapeDtypeStruct((3,) + s.shape[1:], s.dtype) for s in sums]


def _owner_sems(n):
    return [pltpu.SemaphoreType.DMA((3 * n,)), pltpu.SemaphoreType.DMA((3 * n,))]


def _owners_plan(src, dst, send, recv):
    x, y, c, chips = _place()
    copies = [pltpu.make_async_remote_copy(
        src_ref=src[k].at[2 * chip[0] + chip[1]], dst_ref=dst[k].at[m], send_sem=send.at[3 * k + m],
        recv_sem=recv.at[3 * k + m], device_id=(*chip, c), device_id_type=MESH)
        for k in range(len(src)) for m, chip in enumerate(chips)]

    def start():
        for cp in copies:
            cp.start()

    def finish():
        for cp in copies:
            cp.wait()

    return start, finish


def _join_halves(halves):
    n = len(halves)

    def body(*refs):
        src, dst = refs[:n], refs[n:2 * n]
        send, recv = refs[2 * n:]
        x, y, c, _ = _place()
        copies = []
        for k in range(n):
            copies.append(pltpu.make_async_remote_copy(src_ref=src[k], dst_ref=dst[k], send_sem=send.at[k],
                                                       recv_sem=recv.at[k], device_id=(x, y, 1 - c), device_id_type=MESH))
            copies[-1].start()
        for cp in copies:
            cp.wait()

    out_shape = [jax.ShapeDtypeStruct(h.shape, h.dtype) for h in halves]
    return pl.pallas_call(body, in_specs=[ANY] * n, out_specs=[ANY] * n, out_shape=out_shape,
                          scratch_shapes=[pltpu.SemaphoreType.DMA((n,)), pltpu.SemaphoreType.DMA((n,))],
                          name="join_halves")(*halves)


def _allreduce_small(v):
    rows, n = v.shape

    def body(v_ref, o_ref, buf, send, recv):
        x, y, c, _ = _place()
        me = 4 * x + 2 * y + c
        buf[me] = v_ref[...]
        copies = []
        for d in range(1, 8):
            peer = (x ^ (d >> 2), y ^ ((d >> 1) & 1), c ^ (d & 1))
            cp = pltpu.make_async_remote_copy(src_ref=v_ref, dst_ref=buf.at[me], send_sem=send.at[d], recv_sem=recv.at[d],
                                              device_id=peer, device_id_type=MESH)
            cp.start()
            copies.append(cp)
        for d in range(1, 8):
            peer = 4 * (x ^ (d >> 2)) + 2 * (y ^ ((d >> 1) & 1)) + (c ^ (d & 1))
            pltpu.make_async_remote_copy(src_ref=v_ref, dst_ref=buf.at[peer], send_sem=send.at[d], recv_sem=recv.at[d],
                                         device_id=(x, y, c), device_id_type=MESH).wait_recv()
        for cp in copies:
            cp.wait_send()
        acc = buf[0]
        for d in range(1, 8):
            acc = acc + buf[d]
        o_ref[...] = acc

    vm = pl.BlockSpec(memory_space=pltpu.VMEM)
    return pl.pallas_call(body, in_specs=[vm], out_specs=vm, out_shape=jax.ShapeDtypeStruct((rows, n), F32),
                          scratch_shapes=[pltpu.VMEM((8, rows, n), F32), pltpu.SemaphoreType.DMA((8,)),
                                          pltpu.SemaphoreType.DMA((8,))], name="allreduce_small")(v)


def _add_call(name, parts, tm, out_dtype=F32):
    def fn(i, *vals):
        acc = vals[0].astype(F32)
        for v in vals[1:]:
            acc = acc + v.astype(F32)
        return (acc,), ()
    return _rows_call(name, fn, list(parts), [], [(parts[0].shape[1], out_dtype)], [], tm)[0]


def _adamw(name, w, g, m, v):
    c1 = 1.0 / (1.0 - ADAM_B1 ** ADAM_STEP)
    c2 = 1.0 / (1.0 - ADAM_B2 ** ADAM_STEP)

    def fn(i, wv, gv, mv, vv):
        m2 = ADAM_B1 * mv + (1.0 - ADAM_B1) * gv
        v2 = ADAM_B2 * vv + (1.0 - ADAM_B2) * jnp.square(gv)
        delta = -ADAM_LR * ((m2 * c1) / (jnp.sqrt(v2 * c2) + ADAM_EPS) + ADAM_WD * wv)
        return (delta, m2, v2), ()
    cols = w.shape[1]
    tm = _tile(w.shape[0], (256, 176, 128, 64, 8))
    return _rows_call(name, fn, [w, g, m, v], [], [(cols, F32)] * 3, [], tm)


def _canon(name, a, transposed):
    return a[0].T if transposed else a[0]


def _pack_sharded(w_up, a_up, g_up, conv_dw):
    parts = [w_up[0].T, a_up[0].T, g_up[0].T, conv_dw[0].T]
    used = sum(p.shape[1] for p in parts)
    return jnp.concatenate(parts + [jnp.zeros((parts[0].shape[0], PACK_W - used), F32)], axis=1)


def _unpack_sharded(a):
    return [a[:, 0:64].T[None], a[:, 64:128].T[None], a[:, 128:256].T[None], a[:, 256:256 + CW].T[None]]


def _pack_small(vals):
    flat = jnp.concatenate([v.reshape(-1) for v in vals] + [jnp.zeros((SMALL_PAD - SMALL_N,), F32)])
    return flat.reshape(8, SMALL_PAD // 8)


def _unpack_small(a, like):
    flat, out, off = a.reshape(-1), [], 0
    for (_, n), ref in zip(SMALL, like):
        out.append(flat[off:off + n].reshape(ref.shape))
        off += n
    return out


def kernel(x, ffn1_norm_pre, ffn1_norm_post, ffn1_w_gu, ffn1_w_down, mix_norm_pre, mix_norm_post, w_in, shift_mu, w_up, w0, a_up, a0, g_up, k_k, k_a, r_k, gn_w, gn_b, conv_dw, conv_b, conv_ln_w, conv_ln_b, w_out, ffn2_norm_pre, ffn2_norm_post, ffn2_w_gu, ffn2_w_down, loss_target, m_ffn1_norm_pre, m_ffn1_norm_post, m_ffn1_w_gu, m_ffn1_w_down, m_mix_norm_pre, m_mix_norm_post, m_w_in, m_shift_mu, m_w_up, m_w0, m_a_up, m_a0, m_g_up, m_k_k, m_k_a, m_r_k, m_gn_w, m_gn_b, m_conv_dw, m_conv_b, m_conv_ln_w, m_conv_ln_b, m_w_out, m_ffn2_norm_pre, m_ffn2_norm_post, m_ffn2_w_gu, m_ffn2_w_down, v_ffn1_norm_pre, v_ffn1_norm_post, v_ffn1_w_gu, v_ffn1_w_down, v_mix_norm_pre, v_mix_norm_post, v_w_in, v_shift_mu, v_w_up, v_w0, v_a_up, v_a0, v_g_up, v_k_k, v_k_a, v_r_k, v_gn_w, v_gn_b, v_conv_dw, v_conv_b, v_conv_ln_w, v_conv_ln_b, v_w_out, v_ffn2_norm_pre, v_ffn2_norm_post, v_ffn2_w_gu, v_ffn2_w_down):
    w = dict(ffn1_norm_pre=ffn1_norm_pre, ffn1_norm_post=ffn1_norm_post, ffn1_w_gu=ffn1_w_gu, ffn1_w_down=ffn1_w_down, mix_norm_pre=mix_norm_pre, mix_norm_post=mix_norm_post, w_in=w_in, shift_mu=shift_mu, w_up=w_up, w0=w0, a_up=a_up, a0=a0, g_up=g_up, k_k=k_k, k_a=k_a, r_k=r_k, gn_w=gn_w, gn_b=gn_b, conv_dw=conv_dw, conv_b=conv_b, conv_ln_w=conv_ln_w, conv_ln_b=conv_ln_b, w_out=w_out, ffn2_norm_pre=ffn2_norm_pre, ffn2_norm_post=ffn2_norm_post, ffn2_w_gu=ffn2_w_gu, ffn2_w_down=ffn2_w_down)
    mom = dict(ffn1_norm_pre=m_ffn1_norm_pre, ffn1_norm_post=m_ffn1_norm_post, ffn1_w_gu=m_ffn1_w_gu, ffn1_w_down=m_ffn1_w_down, mix_norm_pre=m_mix_norm_pre, mix_norm_post=m_mix_norm_post, w_in=m_w_in, shift_mu=m_shift_mu, w_up=m_w_up, w0=m_w0, a_up=m_a_up, a0=m_a0, g_up=m_g_up, k_k=m_k_k, k_a=m_k_a, r_k=m_r_k, gn_w=m_gn_w, gn_b=m_gn_b, conv_dw=m_conv_dw, conv_b=m_conv_b, conv_ln_w=m_conv_ln_w, conv_ln_b=m_conv_ln_b, w_out=m_w_out, ffn2_norm_pre=m_ffn2_norm_pre, ffn2_norm_post=m_ffn2_norm_post, ffn2_w_gu=m_ffn2_w_gu, ffn2_w_down=m_ffn2_w_down)
    var = dict(ffn1_norm_pre=v_ffn1_norm_pre, ffn1_norm_post=v_ffn1_norm_post, ffn1_w_gu=v_ffn1_w_gu, ffn1_w_down=v_ffn1_w_down, mix_norm_pre=v_mix_norm_pre, mix_norm_post=v_mix_norm_post, w_in=v_w_in, shift_mu=v_shift_mu, w_up=v_w_up, w0=v_w0, a_up=v_a_up, a0=v_a0, g_up=v_g_up, k_k=v_k_k, k_a=v_k_a, r_k=v_r_k, gn_w=v_gn_w, gn_b=v_gn_b, conv_dw=v_conv_dw, conv_b=v_conv_b, conv_ln_w=v_conv_ln_w, conv_ln_b=v_conv_ln_b, w_out=v_w_out, ffn2_norm_pre=v_ffn2_norm_pre, ffn2_norm_post=v_ffn2_norm_post, ffn2_w_gu=v_ffn2_w_gu, ffn2_w_down=v_ffn2_w_down)
    names = list(w)

    x0 = x[0]
    tgt = loss_target[0]
    t_len = x0.shape[0]
    tm = _tile(t_len, (512, 256, 128))

    half = {n: rows // 8 for n, rows, _ in BIG}
    pack_local = _pack_sharded(w["w_up"], w["a_up"], w["g_up"], w["conv_dw"])
    shard = {n: _canon(n, w[n], tr).astype(BF16).reshape(2, half[n], D) for n, _, tr in BIG}
    rows_of = {n: rows for n, rows, _ in BIG}
    full = {"ffn1_w_gu": _gather_weights([shard["ffn1_w_gu"]])[0].reshape(2 * DFF, D)}

    h1, gu1, act1, f1, gathered = _ffn_fwd("ffn1", x0, w["ffn1_norm_pre"], full["ffn1_w_gu"], None, tm,
                                           gather=[shard[n] for n in SECOND] + [pack_local.reshape(2, 64, PACK_W)])
    full.update({n: g.reshape(rows_of[n], D) for n, g in zip(SECOND, gathered)})
    pack = gathered[-1].reshape(DR, PACK_W)
    p01, p2 = pack[:, 0:128], pack[:, 128:256]
    dw32 = jnp.concatenate([pack[:, 256:256 + CW].T, jnp.zeros((1, DR), F32)], axis=0)
    (x1,) = _rows_call("ffn1_resid", lambda i, xv, fv, g: ((xv + 0.5 * _rms(fv, g),), ()), [x0, f1],
                       [w["ffn1_norm_post"]], [(D, F32)], [], tm)
    (hm,) = _rows_call("mix_norm", lambda i, xv, g: ((_rms(xv, g),), ()), [x1], [w["mix_norm_pre"]], [(D, BF16)], [], tm)
    p = _matmul("mix_in", hm, full["w_in"], "nt")
    ps = _shift_fwd(p, w["shift_mu"], tm)
    rkc = w["r_k"].reshape(1, DR)
    prep_consts = [w["w0"], w["a0"], w["k_k"], w["k_a"], p01, p2]
    r_, dec, k2, v_, z_, b_, g_ = _rows_call(
        "prep", lambda i, psv, *cs: (_prep(psv, *cs), ()), [ps], prep_consts, [(DR, F32)] * 7, [], min(tm, 256))
    y, ck, late = _rec_fwd(r_, dec, k2, z_, b_, v_, [shard[n] for n in LATE])
    full.update({n: g.reshape(rows_of[n], D) for n, g in zip(LATE, late)})
    glu, cpre, ob = _conv_fwd(p, dw32, w["conv_b"], w["conv_ln_w"], w["conv_ln_b"], tm)
    post_consts = [w["gn_w"], w["gn_b"], rkc]
    (o,) = _rows_call(
        "post", lambda i, yv, rv, kv, vv, gv, obv, *cs: ((jnp.concatenate([_post(yv, rv, kv, vv, gv, *cs).astype(BF16), obv], axis=1),), ()),
        [y, r_, k2, v_, g_, ob], post_consts, [(D, BF16)], [], min(tm, 256))
    mo = _matmul("mix_out", o, full["w_out"], "nn")
    (x2,) = _rows_call("mix_resid", lambda i, xv, fv, g: ((xv + _rms(fv, g),), ()), [x1, mo],
                       [w["mix_norm_post"]], [(D, F32)], [], tm)
    h2, gu2, act2, f2, _ = _ffn_fwd("ffn2", x2, w["ffn2_norm_pre"], full["ffn2_w_gu"], full["ffn2_w_down"], tm)

    def loss_fn(i, xv, fv, tv, g):
        err = xv + 0.5 * _rms(fv, g) - tv
        part = 0.5 * jnp.sum(jnp.mean(err * err, axis=-1, keepdims=True), axis=0, keepdims=True)
        return (err * (1.0 / D),), (jnp.broadcast_to(part, (8, 128)),)
    dx3, loss_part = _rows_call("loss", loss_fn, [x2, f2, tgt], [w["ffn2_norm_post"]], [(D, F32)], [(8, 128)], tm)
    loss = lax.psum(loss_part[0, 0], ("x", "y", "c"))

    g_small = {}
    dx2, g_small["ffn2_norm_pre"], g_small["ffn2_norm_post"], dgu2_t, dwd2, _ = _ffn_bwd(
        "ffn2", x2, w["ffn2_norm_pre"], w["ffn2_norm_post"], 0.5, h2, gu2, act2, f2, dx3, full["ffn2_w_gu"], full["ffn2_w_down"], tm)

    def mix_resid_b(i, fv, dv, g):
        _, vjp = jax.vjp(_rms, fv, g)
        df, dg = vjp(dv)
        return (df,), (dg,)
    dmo, g_small["mix_norm_post"] = _rows_call("mix_resid_b", mix_resid_b, [mo, dx2], [w["mix_norm_post"]],
                                               [(D, BF16)], [(1, D)], tm)
    do = _matmul("mix_do", dmo, full["w_out"], "nt")
    dw_out = _matmul("mix_dwout", o, dmo, "tn")

    def post_b(i, yv, rv, kv, vv, gv, dov, *cs):
        _, vjp = jax.vjp(_post, yv, rv, kv, vv, gv, *cs)
        dy, dr, dk, dv, dg, dgw, dgb, drk = vjp(dov[:, :DR])
        return (dy, dr, dk, dv, dg), (_colsum(dgw), _colsum(dgb), _colsum(drk))
    dy, dr1, dk1, dv1, dg, g_small["gn_w"], g_small["gn_b"], g_small["r_k"] = _rows_call(
        "post_b", post_b, [y, r_, k2, v_, g_, do], post_consts, [(DR, F32)] * 5, [(1, DR)] * 3, min(tm, 256))
    my_c = lax.axis_index("c")
    my_chip = 2 * lax.axis_index("x") + lax.axis_index("y")
    g_big = dict(w_out=dw_out, ffn2_w_gu=dgu2_t, ffn2_w_down=dwd2)

    def halves_of(group, which):
        return jnp.concatenate([lax.dynamic_index_in_dim(g_big[n].reshape(4, 2, half[n], D), which, 1, keepdims=False)
                                for n in group], axis=1)

    def pair_sums(tag, group, extra_mine=(), extra_give=()):
        mine = halves_of(group, my_c)
        got, *got_x = _swap_halves("swap_halves_" + tag, [halves_of(group, 1 - my_c).astype(BF16)] + list(extra_give))
        rows = mine.shape[1]
        tile = _tile(rows, (352, 592, 16))
        travels = _add_call("add_pair_" + tag, [mine.reshape(4 * rows, D), got.reshape(4 * rows, D)], tile, BF16)
        return mine, got, travels.reshape(4, rows, D), got_x

    def owner_sum(tag, mine, got, parts):
        own = [lax.dynamic_index_in_dim(a, my_chip, 0, keepdims=False) for a in (mine, got)]
        return _add_call("add_chips_" + tag, own + [parts[m] for m in range(3)], _tile(mine.shape[1], (352, 592, 16)))

    mine_l, got_l, sum_l, _ = pair_sums("late", LATE)
    (dr2, ddec, dk2, dz, db, dv2), (parts_l,) = _rec_bwd(r_, dec, k2, z_, b_, v_, dy, ck, [sum_l])

    def prep_b(i, psv, a1, a2, c1, c2, e1, e2, dwv, dzv, dbv, dgv, *cs):
        _, vjp = jax.vjp(_prep, psv, *cs)
        dps, dw0, da0, dkk, dka, dp01, dp2 = vjp((a1 + a2, dwv, c1 + c2, e1 + e2, dzv, dbv, dgv))
        return (dps,), (_colsum(dw0), _colsum(da0), _colsum(dkk), _colsum(dka), dp01, dp2)
    dps, g_small["w0"], g_small["a0"], g_small["k_k"], g_small["k_a"], dp01, dp2 = _rows_call(
        "prep_b", prep_b, [ps, dr1, dr2, dk1, dk2, dv1, dv2, ddec, dz, db, dg], prep_consts, [(D_SHIFT, F32)],
        [(1, DR)] * 4 + [(DR, 128)] * 2, min(tm, 256))

    def convln_b(i, cv, dov, lw, lb):
        _, vjp = jax.vjp(_ln_silu, cv, lw, lb)
        dc, dlw, dlb = vjp(dov[:, DR:])
        return (dc,), (_colsum(dc), _colsum(dlw), _colsum(dlb))
    dc, g_small["conv_b"], g_small["conv_ln_w"], g_small["conv_ln_b"] = _rows_call(
        "convln_b", convln_b, [cpre, do], [w["conv_ln_w"], w["conv_ln_b"]], [(DR, F32)], [(1, DR)] * 3, tm)
    dpc, ddw32 = _conv_bwd(dc, glu, p, dw32, tm)
    dp, g_small["shift_mu"] = _shift_bwd(dps, p, dpc, w["shift_mu"], tm)
    dhm = _matmul("mix_dh", dp, full["w_in"], "nn")
    dw_in_t = _matmul("mix_dwin", dp, hm, "tn")

    def norm_b(i, xv, dhv, dv, g):
        _, vjp = jax.vjp(_rms, xv, g)
        dx, dg_ = vjp(dhv)
        return (dx + dv,), (dg_,)
    dx1, g_small["mix_norm_pre"] = _rows_call("mix_norm_b", norm_b, [x1, dhm, dx2], [w["mix_norm_pre"]],
                                              [(D, F32)], [(1, D)], tm)
    kept = {}

    def second_sums(dwd1):
        g_big.update(ffn1_w_down=dwd1, w_in=dw_in_t)
        pack_grads = jnp.concatenate([dp01, dp2, ddw32.T, jnp.zeros((DR, PACK_W - 288), F32)], axis=1).reshape(4, 2, 64, PACK_W)
        mine_p, give_p = (lax.dynamic_index_in_dim(pack_grads, which, 1, keepdims=False) for which in (my_c, 1 - my_c))
        mine, got, travels, (got_p,) = pair_sums("second", SECOND, extra_give=[give_p])
        sum_p = _add_call("add_pair_pack", [mine_p.reshape(256, PACK_W), got_p.reshape(256, PACK_W)], 256).reshape(4, 64, PACK_W)
        kept.update(second=(mine, got), sum_p=sum_p)
        return [travels, sum_p]

    def first_sums(dgu1_t):
        g_big.update(ffn1_w_gu=dgu1_t)
        mine, got, travels, _ = pair_sums("first", ("ffn1_w_gu",))
        kept.update(first=(mine, got))
        return [travels]

    grad_x, g_small["ffn1_norm_pre"], g_small["ffn1_norm_post"], _, _, ((parts_s, parts_p), (parts_f,)) = _ffn_bwd(
        "ffn1", x0, w["ffn1_norm_pre"], w["ffn1_norm_post"], 0.5, h1, gu1, act1, f1, dx1, full["ffn1_w_gu"], full["ffn1_w_down"], tm,
        after_down=second_sums, after_gu=first_sums)

    sum_p = kept["sum_p"]
    fin_s = owner_sum("second", *kept["second"], parts_s)
    fin_f = owner_sum("first", *kept["first"], parts_f)
    fin_l = owner_sum("late", mine_l, got_l, parts_l)
    fin_p = _add_call("add_chips_pack", [lax.dynamic_index_in_dim(sum_p, my_chip, 0, keepdims=False)] + [parts_p[m] for m in range(3)], 64)
    fins = [fin_f, fin_s, fin_l, fin_p]
    red_f, red_s, red_l, red_p = [jnp.where(my_c == 0, jnp.stack([f, s]), jnp.stack([s, f])) for f, s in zip(fins, _join_halves(fins))]
    small_sum = _allreduce_small(_pack_small([g_small[n] for n, _ in SMALL]))

    grads, delta, new_m, new_v = {}, {}, {}, {}
    reduced = {}
    for group, red in ((("ffn1_w_gu",), red_f), (SECOND, red_s), (LATE, red_l)):
        off = 0
        for n in group:
            reduced[n] = red[:, off:off + half[n], :].reshape(rows_of[n] // 4, D)
            off += half[n]
    for n, rows, tr in BIG:
        g = reduced[n]
        g = (g.T if tr else g)[None]
        grads[n] = g
        d_, m_, v2_ = _adamw("adamw_" + n, w[n][0], g[0], mom[n][0], var[n][0])
        delta[n], new_m[n], new_v[n] = d_[None], m_[None], v2_[None]
    sh = ("w_up", "a_up", "g_up", "conv_dw")
    g_pack = red_p.reshape(128, PACK_W)
    d_, m_, v2_ = _adamw("adamw_pack", pack_local, g_pack, _pack_sharded(*[mom[n] for n in sh]), _pack_sharded(*[var[n] for n in sh]))
    for dst, src in ((grads, g_pack), (delta, d_), (new_m, m_), (new_v, v2_)):
        for n, a in zip(sh, _unpack_sharded(src)):
            dst[n] = a
    sm = [n for n, _ in SMALL]
    d_, m_, v2_ = _adamw("adamw_small", _pack_small([w[n] for n in sm]), small_sum, _pack_small([mom[n] for n in sm]),
                         _pack_small([var[n] for n in sm]))
    like = [w[n] for n in sm]
    for dst, src in ((grads, small_sum), (delta, d_), (new_m, m_), (new_v, v2_)):
        for n, a in zip(sm, _unpack_small(src, like)):
            dst[n] = a

    wn = names
    return (loss, grad_x[None], *[grads[n] for n in wn], *[delta[n] for n in wn], *[new_m[n] for n in wn],
            *[new_v[n] for n in wn])
```

```python
import functools
import math

import jax
import jax.numpy as jnp
from jax import lax
from jax.experimental import pallas as pl
from jax.experimental.pallas import tpu as pltpu

F32 = jnp.float32
BF16 = jnp.bfloat16

D = 1024
DFF = 2816
DR = 512
HS = 64
D_SHIFT = 1792
D_IN = 2816
CW = 31
RMS_EPS = 1e-6
GN_EPS = 64e-5
LN_EPS = 1e-5
DECAY_SCALE = math.exp(-0.5)
ADAM_LR, ADAM_B1, ADAM_B2, ADAM_EPS, ADAM_WD, ADAM_STEP = 0.001, 0.9, 0.999, 1e-8, 0.01, 10

REC_TILE = 128
VMEM_LIMIT = 56 * 1024 * 1024
MESH = pl.DeviceIdType.MESH
ANY = pl.BlockSpec(memory_space=pl.ANY)

BIG = (("ffn1_w_gu", 2 * DFF, True), ("ffn1_w_down", DFF, False), ("w_in", D_IN, True),
       ("w_out", D, False), ("ffn2_w_gu", 2 * DFF, True), ("ffn2_w_down", DFF, False))
SECOND = ("ffn1_w_down", "w_in")
LATE = ("w_out", "ffn2_w_gu", "ffn2_w_down")
PACK_W = 384
SMALL = (("ffn1_norm_pre", D), ("ffn1_norm_post", D), ("mix_norm_pre", D), ("mix_norm_post", D),
         ("shift_mu", D_SHIFT), ("w0", DR), ("a0", DR), ("k_k", DR), ("k_a", DR), ("r_k", DR),
         ("gn_w", DR), ("gn_b", DR), ("conv_b", DR), ("conv_ln_w", DR), ("conv_ln_b", DR),
         ("ffn2_norm_pre", D), ("ffn2_norm_post", D))
SMALL_N = sum(n for _, n in SMALL)
SMALL_PAD = 8 * 1664


def _params(sem):
    return pltpu.CompilerParams(dimension_semantics=sem, vmem_limit_bytes=VMEM_LIMIT)


def _tile(n, prefs):
    for p in prefs:
        if n % p == 0:
            return p
    return n


def _rows_call(name, fn, rows, consts, row_outs, acc_outs, tm):
    specs, arrs = [], []
    for r in rows:
        if isinstance(r, tuple):
            a, bs, im = r
            specs.append(pl.BlockSpec(bs, im))
        else:
            a = r
            specs.append(pl.BlockSpec((tm, a.shape[1]), lambda i: (i, 0)))
        arrs.append(a)
    t_rows = arrs[0].shape[0]
    for c in consts:
        specs.append(pl.BlockSpec(c.shape, functools.partial(lambda i, n: (0,) * n, n=c.ndim)))
        arrs.append(c)
    n_in, n_o, n_a = len(arrs), len(row_outs), len(acc_outs)

    def kern(*refs):
        i = pl.program_id(0)
        vals = [r[...] for r in refs[:n_in]]
        ro, ao = fn(i, *vals)
        outs = refs[n_in:]
        for k in range(n_o):
            outs[k][...] = ro[k].astype(outs[k].dtype)
        if n_a:
            @pl.when(i == 0)
            def _():
                for k in range(n_a):
                    outs[n_o + k][...] = jnp.zeros(outs[n_o + k].shape, F32)
            for k in range(n_a):
                outs[n_o + k][...] += ao[k]

    out_specs = [pl.BlockSpec((tm, w), lambda i: (i, 0)) for (w, _) in row_outs]
    out_specs += [pl.BlockSpec(s, functools.partial(lambda i, n: (0,) * n, n=len(s))) for s in acc_outs]
    out_shape = [jax.ShapeDtypeStruct((t_rows, w), dt) for (w, dt) in row_outs]
    out_shape += [jax.ShapeDtypeStruct(s, F32) for s in acc_outs]
    return pl.pallas_call(kern, grid=(t_rows // tm,), in_specs=specs, out_specs=out_specs, out_shape=out_shape,
                          name=name, compiler_params=_params(("arbitrary",)))(*arrs)


_DIMS = {"nn": (((1,), (0,)), ((), ())), "nt": (((1,), (1,)), ((), ())), "tn": (((0,), (0,)), ((), ()))}
_LANE_TILES = (1408, 1024, 512, 384, 256, 128)


def _matmul(name, a, b, mode, out_dtype=F32, owners=None, gather=None):
    if mode == "nn":
        (m, k), (_, n) = a.shape, b.shape
    elif mode == "nt":
        (m, k), (n, _) = a.shape, b.shape
    else:
        (k, m), (_, n) = a.shape, b.shape
    if mode == "tn":
        tm, tk = _tile(m, _LANE_TILES), _tile(k, (512, 256, 128))
    else:
        tm, tk = _tile(m, (1024, 512, 256, 128)), _tile(k, _LANE_TILES)
    tn = _tile(n, _LANE_TILES)
    nk = k // tk
    assert out_dtype == F32 or nk == 1

    owners, gather = list(owners or ()), list(gather or ())
    assert not (owners and gather)
    riders = owners + gather
    ns = len(riders)
    grid = (m // tm, n // tn, nk)
    steps = grid[0] * grid[1] * nk

    def kern(a_ref, b_ref, *rest):
        o_ref = rest[ns]
        if ns:
            step = (pl.program_id(0) * grid[1] + pl.program_id(1)) * nk + pl.program_id(2)
            if owners:
                start, finish = _owners_plan(rest[:ns], rest[ns + 1:2 * ns + 1], *rest[2 * ns + 1:])
                pl.when(step == 0)(start)
            else:
                start, middle, finish = _gather_plan(rest[:ns], rest[ns + 1:2 * ns + 1], *rest[2 * ns + 1:])
                pl.when(step == 0)(start)
                pl.when(step == steps // 2)(middle)

        def part():
            return lax.dot_general(a_ref[...].astype(BF16), b_ref[...].astype(BF16), _DIMS[mode], preferred_element_type=F32)

        if nk == 1:
            o_ref[...] = part().astype(o_ref.dtype)
        else:
            @pl.when(pl.program_id(2) == 0)
            def _():
                o_ref[...] = jnp.zeros(o_ref.shape, F32)

            o_ref[...] += part()
        if ns:
            pl.when(step == steps - 1)(finish)

    a_spec = pl.BlockSpec((tk, tm), lambda i, j, q: (q, i)) if mode == "tn" else pl.BlockSpec((tm, tk), lambda i, j, q: (i, q))
    b_spec = pl.BlockSpec((tn, tk), lambda i, j, q: (j, q)) if mode == "nt" else pl.BlockSpec((tk, tn), lambda i, j, q: (q, j))
    extra_shapes = _owner_shapes(owners) if owners else _gathered_shapes(gather)
    extra_sems = [] if not ns else _owner_sems(ns) if owners else _gather_sems(ns)
    out, *arrived = pl.pallas_call(
        kern, grid=grid, in_specs=[a_spec, b_spec] + [ANY] * ns,
        out_specs=[pl.BlockSpec((tm, tn), lambda i, j, q: (i, j))] + [ANY] * ns,
        out_shape=[jax.ShapeDtypeStruct((m, n), out_dtype)] + extra_shapes,
        scratch_shapes=extra_sems, name=name,
        compiler_params=_params(("arbitrary", "arbitrary", "arbitrary")))(a, b, *riders)
    if gather:
        arrived = _fill_own(arrived, gather)
    return (out, arrived) if ns else out


def _rms(x, g):
    return x * lax.rsqrt(jnp.mean(x * x, axis=-1, keepdims=True) + RMS_EPS) * g


def _silu(x):
    return x * jax.nn.sigmoid(x)


def _bones(n):
    r = lax.broadcasted_iota(jnp.int32, (n, n), 0) // HS
    c = lax.broadcasted_iota(jnp.int32, (n, n), 1) // HS
    return (r == c).astype(BF16)


@jax.custom_vjp
def _segsum(x):
    bones = _bones(x.shape[1])
    hi = x.astype(BF16)
    lo = (x - hi.astype(F32)).astype(BF16)
    return jnp.dot(hi, bones, preferred_element_type=F32) + jnp.dot(lo, bones, preferred_element_type=F32)


_segsum.defvjp(lambda x: (_segsum(x), None), lambda _, ct: (_segsum(ct),))


@jax.custom_vjp
def _dot_nt(x, w):
    return lax.dot_general(x.astype(BF16), w.astype(BF16), _DIMS["nt"], preferred_element_type=F32)


def _dot_nt_bwd(res, ct):
    x, w = res
    ctb = ct.astype(BF16)
    dx = lax.dot_general(ctb, w.astype(BF16), _DIMS["nn"], preferred_element_type=F32)
    dw = lax.dot_general(ctb, x.astype(BF16), _DIMS["tn"], preferred_element_type=F32)
    return dx, dw


_dot_nt.defvjp(lambda x, w: (_dot_nt(x, w), (x, w)), _dot_nt_bwd)


def _prep(ps, w0, a0, k_k, k_a, p01, p2):
    r, k, v = ps[:, :DR], ps[:, DR:2 * DR], ps[:, 2 * DR:3 * DR]
    wa, xg = ps[:, 3 * DR:3 * DR + 128], ps[:, 3 * DR + 128:]
    first = lax.broadcasted_iota(jnp.int32, (1, 128), 1) < 64
    d = w0 + _dot_nt(jnp.where(first, jnp.tanh(wa), 0.0), p01)
    decay = jnp.exp(-DECAY_SCALE * jax.nn.sigmoid(d))
    a = jax.nn.sigmoid(a0 + _dot_nt(jnp.where(first, 0.0, wa), p01))
    g = _dot_nt(jax.nn.sigmoid(xg), p2)
    kk = k * k_k
    kk = kk * lax.rsqrt(jnp.maximum(_segsum(kk * kk), 1e-12))
    k2 = k * (1.0 + (a - 1.0) * k_a)
    return r, decay, k2, v, -kk, kk * a, g


def _post(y, r, k, v, g, gn_w, gn_b, r_k):
    mu = _segsum(y) * (1.0 / HS)
    yc = y - mu
    var = _segsum(yc * yc) * (1.0 / HS)
    yn = yc * lax.rsqrt(var + GN_EPS) * gn_w + gn_b
    return (yn + _segsum(r * k * r_k) * v) * g


def _ln_silu(c, w, b):
    mu = jnp.mean(c, axis=-1, keepdims=True)
    var = jnp.mean(jnp.square(c - mu), axis=-1, keepdims=True)
    return _silu((c - mu) * lax.rsqrt(var + LN_EPS) * w + b)


def _colsum(x):
    return jnp.sum(x, axis=0, keepdims=True)


def _gu_swiglu(name, h, w_gu_t, gather=None):
    m = h.shape[0]
    tm, tn = _tile(m, (512, 256, 128)), 1408
    nj = DFF // tn
    gather = list(gather or ())
    ns = len(gather)
    steps = (m // tm) * nj

    def kern(a_ref, bg_ref, bu_ref, *rest):
        gate_ref, up_ref, act_ref = rest[ns:ns + 3]
        if ns:
            step = pl.program_id(0) * nj + pl.program_id(1)
            start, middle, finish = _gather_plan(rest[:ns], rest[ns + 3:2 * ns + 3], *rest[2 * ns + 3:])
            pl.when(step == 0)(start)
            pl.when(step == steps // 2)(middle)
        a = a_ref[...]
        gate = lax.dot_general(a, bg_ref[...], _DIMS["nt"], preferred_element_type=F32)
        up = lax.dot_general(a, bu_ref[...], _DIMS["nt"], preferred_element_type=F32)
        gate_ref[...] = gate
        up_ref[...] = up
        act_ref[...] = (_silu(gate) * up).astype(BF16)
        if ns:
            pl.when(step == steps - 1)(finish)

    tile = pl.BlockSpec((tm, tn), lambda i, j: (i, j))
    gate, up, act, *got = pl.pallas_call(
        kern, grid=(m // tm, nj),
        in_specs=[pl.BlockSpec((tm, D), lambda i, j: (i, 0)), pl.BlockSpec((tn, D), lambda i, j: (j, 0)),
                  pl.BlockSpec((tn, D), lambda i, j: (j + nj, 0))] + [ANY] * ns,
        out_specs=[tile] * 3 + [ANY] * ns,
        out_shape=[jax.ShapeDtypeStruct((m, DFF), F32)] * 2 + [jax.ShapeDtypeStruct((m, DFF), BF16)] + _gathered_shapes(gather),
        scratch_shapes=_gather_sems(ns) if ns else [], name=name,
        compiler_params=_params(("arbitrary", "arbitrary")))(h, w_gu_t, w_gu_t, *gather)
    return gate, up, act, _fill_own(got, gather)


def _ffn_fwd(tag, x, pre, w_gu_t, w_down, tm, gather=None):
    (h,) = _rows_call(tag + "_norm", lambda i, xv, g: ((_rms(xv, g),), ()), [x], [pre], [(D, BF16)], [], tm)
    gate, up, act, gathered = _gu_swiglu(tag + "_gu", h, w_gu_t, gather)
    if gather:
        w_down = gathered[0].reshape(DFF, D)
    f = _matmul(tag + "_down", act, w_down, "nn")
    return h, (gate, up), act, f, gathered


def _ffn_bwd(tag, x, pre, post, scale, h, gu, act, f, dxo, w_gu_t, w_down, tm, after_down=None, after_gu=None):
    def resid_b(i, fv, dv, g):
        _, vjp = jax.vjp(lambda a, b: scale * _rms(a, b), fv, g)
        df, dg = vjp(dv)
        return (df,), (dg,)
    df, dpost = _rows_call(tag + "_resid_b", resid_b, [f, dxo], [post], [(D, BF16)], [(1, D)], tm)
    dact = _matmul(tag + "_dact", df, w_down, "nt")
    dw_down = _matmul(tag + "_dwdown", act, df, "tn")

    def act_b(i, gv, uv, dv):
        _, vjp = jax.vjp(lambda a, b: _silu(a) * b, gv, uv)
        return (jnp.concatenate(vjp(dv), axis=1),), ()
    (dgu,) = _rows_call(tag + "_act_b", act_b, [*gu, dact], [], [(2 * DFF, BF16)], [], min(tm, 256))
    sums = after_down(dw_down) if after_down else []
    dw_gu_t = _matmul(tag + "_dwgu", dgu, h, "tn", owners=sums)
    dw_gu_t, parts_down = dw_gu_t if sums else (dw_gu_t, [])
    sums = after_gu(dw_gu_t) if after_gu else []
    dh = _matmul(tag + "_dh", dgu, w_gu_t, "nn", owners=sums)
    dh, parts_gu = dh if sums else (dh, [])
    parts = (parts_down, parts_gu)

    def norm_b(i, xv, dhv, dv, g):
        _, vjp = jax.vjp(_rms, xv, g)
        dx, dg = vjp(dhv)
        return (dx + dv,), (dg,)
    dx, dpre = _rows_call(tag + "_norm_b", norm_b, [x, dh, dxo], [pre], [(D, F32)], [(1, D)], tm)
    return dx, dpre, dpost, dw_gu_t, dw_down, parts


def _pair_bcast(cols, first):
    return jnp.concatenate([jnp.where(first, cols[2 * p], cols[2 * p + 1]) for p in range(4)], axis=1)


def _head_sums(x, first):
    cols = []
    for p in range(4):
        xp = x[:, 128 * p:128 * (p + 1)]
        cols.append(jnp.sum(jnp.where(first, xp, 0.0), axis=1, keepdims=True))
        cols.append(jnp.sum(jnp.where(first, 0.0, xp), axis=1, keepdims=True))
    return cols


def _split16(x8):
    hi = x8.astype(BF16).astype(F32)
    return jnp.concatenate([hi, x8 - hi], axis=0).astype(BF16)


def _cols8(x8, e16):
    return lax.dot_general(_split16(x8), e16, _DIMS["tn"], preferred_element_type=F32)


def _pair_rows(c, first):
    return jnp.concatenate([jnp.where(first, c[128 * p:128 * p + HS], c[128 * p + HS:128 * (p + 1)]) for p in range(4)], axis=1)


def _rows8(prod, bones, dmask):
    x = prod.astype(BF16)
    full = jnp.concatenate([jnp.dot(x[:, 256 * q:256 * (q + 1)], bones, preferred_element_type=F32) for q in range(2)], axis=1)
    return jnp.concatenate([_colsum(full[HS * j:HS * (j + 1)] * dmask) for j in range(8)], axis=0)


def _rec_step(s, wr, zr, br, kr, vc, first):
    u = _pair_bcast(_head_sums(s * zr, first), first)
    return s * wr + u * br + vc * kr, u


def _rec_consts():
    e16 = (lax.broadcasted_iota(jnp.int32, (16, 1024), 0) % 8 == lax.broadcasted_iota(jnp.int32, (16, 1024), 1) // 128)
    dmask = lax.broadcasted_iota(jnp.int32, (HS, DR), 0) == lax.broadcasted_iota(jnp.int32, (HS, DR), 1) % HS
    return e16.astype(BF16), _bones(256), dmask.astype(F32)


def _const_spec(a):
    return pl.BlockSpec(a.shape, functools.partial(lambda i, n: (0,) * n, n=a.ndim))


def _rec_fwd(r, w, k, z, b, v, shards):
    t_len = r.shape[0]
    nt = t_len // REC_TILE
    ns = len(shards)
    consts = _rec_consts()

    def kern(r_ref, w_ref, k_ref, z_ref, b_ref, v_ref, e_ref, bones_ref, dm_ref, *rest):
        y_ref, ck_ref, states, u_ref = rest[ns:ns + 4]
        s_ref, prod = rest[2 * ns + 4:2 * ns + 6]
        start, middle, finish = _gather_plan(rest[:ns], rest[ns + 4:2 * ns + 4], *rest[2 * ns + 6:])
        i = pl.program_id(0)

        @pl.when(i == 0)
        def _():
            s_ref[...] = jnp.zeros(s_ref.shape, F32)
            start()

        pl.when(i == nt // 2)(middle)
        ck_ref[0] = s_ref[...]
        first = lax.broadcasted_iota(jnp.int32, (1, 128), 1) < HS

        def group(g8, s):
            base = pl.multiple_of(g8 * 8, 8)
            r8, w8, k8, z8, b8, v8 = (ref[pl.ds(base, 8), :] for ref in (r_ref, w_ref, k_ref, z_ref, b_ref, v_ref))
            vcols = _cols8(v8, e_ref[...])
            urows = []
            for j in range(8):
                vc = _pair_rows(vcols[:, 128 * j:128 * (j + 1)], first)
                s, u = _rec_step(s, w8[j:j + 1], z8[j:j + 1], b8[j:j + 1], k8[j:j + 1], vc, first)
                states[base + j] = s
                urows.append(_colsum(u * dm_ref[...]))
                prod[HS * j:HS * (j + 1), :] = s * r8[j:j + 1]
            y_ref[pl.ds(base, 8), :] = _rows8(prod[...], bones_ref[...], dm_ref[...])
            u_ref[pl.ds(base, 8), :] = jnp.concatenate(urows, axis=0)
            return s

        s_ref[...] = lax.fori_loop(0, REC_TILE // 8, group, s_ref[...])
        pl.when(i == nt - 1)(finish)

    row = pl.BlockSpec((REC_TILE, DR), lambda i: (i, 0))
    y, ck, states, u, *got = pl.pallas_call(
        kern, grid=(nt,), in_specs=[row] * 6 + [_const_spec(c) for c in consts] + [ANY] * ns,
        out_specs=[row, pl.BlockSpec((1, HS, DR), lambda i: (i, 0, 0)), pl.BlockSpec((REC_TILE, HS, DR), lambda i: (i, 0, 0)), row]
        + [ANY] * ns,
        out_shape=[jax.ShapeDtypeStruct((t_len, DR), F32), jax.ShapeDtypeStruct((nt, HS, DR), F32),
                   jax.ShapeDtypeStruct((t_len, HS, DR), F32), jax.ShapeDtypeStruct((t_len, DR), F32)] + _gathered_shapes(shards),
        scratch_shapes=[pltpu.VMEM((HS, DR), F32), pltpu.VMEM((8 * HS, DR), F32)] + _gather_sems(ns), name="rec_fwd",
        compiler_params=_params(("arbitrary",)))(r, w, k, z, b, v, *consts, *shards)
    return y, (ck, states, u), _fill_own(got, shards)


def _rec_bwd(r, w, k, z, b, v, dy, saved, sums):
    t_len = r.shape[0]
    nt = t_len // REC_TILE
    ns = len(sums)
    consts = _rec_consts()

    def kern(r_ref, w_ref, k_ref, z_ref, b_ref, v_ref, dy_ref, u_ref, ck_ref, states, e_ref, bones_ref, dm_ref, *rest):
        dr_ref, dw_ref, dk_ref, dz_ref, db_ref, dv_ref = rest[ns:ns + 6]
        ds_ref, prod = rest[2 * ns + 6:2 * ns + 8]
        start, finish = _owners_plan(rest[:ns], rest[ns + 6:2 * ns + 6], *rest[2 * ns + 8:])
        i = pl.program_id(0)

        @pl.when(i == 0)
        def _():
            ds_ref[...] = jnp.zeros(ds_ref.shape, F32)
            start()

        first = lax.broadcasted_iota(jnp.int32, (1, 128), 1) < HS

        def bgroup(gg, ds):
            base = pl.multiple_of((REC_TILE // 8 - 1 - gg) * 8, 8)
            r8, w8, k8, z8, b8, v8, dy8, u8 = (ref[pl.ds(base, 8), :]
                                               for ref in (r_ref, w_ref, k_ref, z_ref, b_ref, v_ref, dy_ref, u_ref))
            vcols = _cols8(v8, e_ref[...])
            dycols = _cols8(dy8, e_ref[...])
            ucols = _cols8(u8, e_ref[...])
            before = jnp.where(base == 0, ck_ref[0], states[jnp.maximum(base - 1, 0)])
            rows = {n: [None] * 8 for n in ("dr", "dw", "dk", "dz", "db")}
            for j in range(7, -1, -1):
                t = base + j
                rr, wr, kr, zr, br = (x[j:j + 1] for x in (r8, w8, k8, z8, b8))
                s_prev, s_t = (states[t - 1] if j else before), states[t]
                dyc = _pair_rows(dycols[:, 128 * j:128 * (j + 1)], first)
                vc = _pair_rows(vcols[:, 128 * j:128 * (j + 1)], first)
                ds = ds + dyc * rr
                rows["dr"][j] = _colsum(s_t * dyc)
                rows["dw"][j] = _colsum(ds * s_prev)
                du = _pair_bcast(_head_sums(ds * br, first), first)
                rows["db"][j] = _colsum(ds * _pair_rows(ucols[:, 128 * j:128 * (j + 1)], first))
                rows["dk"][j] = _colsum(ds * vc)
                prod[HS * j:HS * (j + 1), :] = ds * kr
                rows["dz"][j] = _colsum(s_prev * du)
                ds = ds * wr + du * zr
            for n, ref in (("dr", dr_ref), ("dw", dw_ref), ("dk", dk_ref), ("dz", dz_ref), ("db", db_ref)):
                ref[pl.ds(base, 8), :] = jnp.concatenate(rows[n], axis=0)
            dv_ref[pl.ds(base, 8), :] = _rows8(prod[...], bones_ref[...], dm_ref[...])
            return ds

        ds_ref[...] = lax.fori_loop(0, REC_TILE // 8, bgroup, ds_ref[...])
        pl.when(i == nt - 1)(finish)

    ck, states, u = saved
    row = pl.BlockSpec((REC_TILE, DR), lambda i: (nt - 1 - i, 0))
    outs = pl.pallas_call(
        kern, grid=(nt,),
        in_specs=[row] * 8 + [pl.BlockSpec((1, HS, DR), lambda i: (nt - 1 - i, 0, 0)),
                              pl.BlockSpec((REC_TILE, HS, DR), lambda i: (nt - 1 - i, 0, 0))]
        + [_const_spec(c) for c in consts] + [ANY] * ns,
        out_specs=[row] * 6 + [ANY] * ns, out_shape=[jax.ShapeDtypeStruct((t_len, DR), F32)] * 6 + _owner_shapes(sums),
        scratch_shapes=[pltpu.VMEM((HS, DR), F32), pltpu.VMEM((8 * HS, DR), F32)] + _owner_sems(ns), name="rec_bwd",
        compiler_params=_params(("arbitrary",)))(r, w, k, z, b, v, dy, u, ck, states, *consts, *sums)
    return outs[:6], outs[6:]


def _prev_rows(a, tm, n):
    return (a, (n, a.shape[1]), lambda i: (jnp.maximum(i * (tm // n) - 1, 0), 0))


def _next_rows(a, tm, n):
    last = a.shape[0] // n - 1
    return (a, (n, a.shape[1]), lambda i: (jnp.minimum((i + 1) * (tm // n), last), 0))


def _shifted(x, prev8, i):
    rowid = lax.broadcasted_iota(jnp.int32, x.shape, 0)
    before = jnp.where(i == 0, 0.0, prev8[7:8, :])
    return jnp.where(rowid == 0, before, pltpu.roll(x, 1, 0))


def _shift_fwd(p, mu, tm):
    def fn(i, pv, prev8, muv):
        x = pv[:, :D_SHIFT]
        return (x + (_shifted(x, prev8[:, :D_SHIFT], i) - x) * muv,), ()
    return _rows_call("shift", fn, [p, _prev_rows(p, tm, 8)], [mu], [(D_SHIFT, F32)], [], tm)[0]


def _shift_bwd(dps, p, dpc, mu, tm):
    n_tiles = p.shape[0] // tm

    def fn(i, dv, next8, pv, prev8, dpcv, muv):
        x = pv[:, :D_SHIFT]
        xs = _shifted(x, prev8[:, :D_SHIFT], i)
        rowid = lax.broadcasted_iota(jnp.int32, dv.shape, 0)
        after = jnp.where(i == n_tiles - 1, 0.0, next8[0:1, :])
        dnext = jnp.where(rowid == tm - 1, after, pltpu.roll(dv, tm - 1, 0))
        dp_s = dv * (1.0 - muv) + dnext * muv
        return (jnp.concatenate([dp_s.astype(BF16), dpcv], axis=1),), (_colsum(dv * (xs - x)),)
    return _rows_call("shift_b", fn, [dps, _next_rows(dps, tm, 8), p, _prev_rows(p, tm, 8), dpc], [mu],
                      [(D_IN, BF16)], [(1, D_SHIFT)], tm)


def _glu(pc):
    return pc[:, :DR] * jax.nn.sigmoid(pc[:, DR:])


def _shift_copies(ext, shifted, tm):
    for s in range(1, 8):
        shifted[s - 1] = ext[s:s + tm + 24, :]


def _window(ext, shifted, off, tm):
    if off % 8 == 0:
        return ext[off:off + tm, :]
    return shifted[off % 8 - 1, off // 8 * 8:off // 8 * 8 + tm, :]


def _conv_fwd(p, dw32, cb, lw, lb, tm):
    t_len = p.shape[0]

    def kern(p_ref, ph_ref, dw_ref, cb_ref, lw_ref, lb_ref, glu_ref, c_ref, ob_ref, ext, shifted):
        i = pl.program_id(0)
        glu = _glu(p_ref[:, D_SHIFT:])
        ext[0:32, :] = jnp.where(i == 0, 0.0, _glu(ph_ref[:, D_SHIFT:]))
        ext[32:, :] = glu
        _shift_copies(ext, shifted, tm)
        acc = jnp.zeros((tm, DR), F32)
        for j in range(CW):
            acc = acc + _window(ext, shifted, 2 + j, tm) * dw_ref[j:j + 1, :]
        c = acc + cb_ref[...]
        glu_ref[...] = glu
        c_ref[...] = c
        ob_ref[...] = _ln_silu(c, lw_ref[...], lb_ref[...]).astype(BF16)

    tile = lambda w: pl.BlockSpec((tm, w), lambda i: (i, 0))
    const = lambda a: pl.BlockSpec(a.shape, lambda i: (0, 0))
    halo = pl.BlockSpec((32, D_IN), lambda i: (jnp.maximum(i * (tm // 32) - 1, 0), 0))
    return pl.pallas_call(
        kern, grid=(t_len // tm,), in_specs=[tile(D_IN), halo, const(dw32), const(cb), const(lw), const(lb)],
        out_specs=[tile(DR)] * 3,
        out_shape=[jax.ShapeDtypeStruct((t_len, DR), F32)] * 2 + [jax.ShapeDtypeStruct((t_len, DR), BF16)],
        scratch_shapes=[pltpu.VMEM((tm + 32, DR), F32), pltpu.VMEM((7, tm + 24, DR), F32)], name="conv_fwd",
        compiler_params=_params(("arbitrary",)))(p, p, dw32, cb, lw, lb)


def _conv_bwd(dc, glu, p, dw32, tm):
    t_len = p.shape[0]
    n_tiles = t_len // tm

    def kern(dc_ref, dcn_ref, glu_ref, gluh_ref, p_ref, dw_ref, dpc_ref, ddw_ref, ext_d, ext_g, shifted_d, shifted_g):
        i = pl.program_id(0)

        @pl.when(i == 0)
        def _():
            ddw_ref[...] = jnp.zeros(ddw_ref.shape, F32)

        dcv = dc_ref[...]
        ext_d[0:tm, :] = dcv
        ext_d[tm:, :] = jnp.where(i == n_tiles - 1, 0.0, dcn_ref[...])
        ext_g[0:32, :] = jnp.where(i == 0, 0.0, gluh_ref[...])
        ext_g[32:, :] = glu_ref[...]
        _shift_copies(ext_d, shifted_d, tm)
        _shift_copies(ext_g, shifted_g, tm)
        dglu = jnp.zeros((tm, DR), F32)
        for j in range(CW):
            dglu = dglu + _window(ext_d, shifted_d, 30 - j, tm) * dw_ref[j:j + 1, :]
            ddw_ref[j:j + 1, :] += _colsum(dcv * _window(ext_g, shifted_g, 2 + j, tm))
        pc = p_ref[:, D_SHIFT:]
        sg = jax.nn.sigmoid(pc[:, DR:])
        dpc_ref[...] = jnp.concatenate([dglu * sg, dglu * pc[:, :DR] * sg * (1.0 - sg)], axis=1).astype(BF16)

    tile = lambda w: pl.BlockSpec((tm, w), lambda i: (i, 0))
    nxt = pl.BlockSpec((32, DR), lambda i: (jnp.minimum((i + 1) * (tm // 32), t_len // 32 - 1), 0))
    prv = pl.BlockSpec((32, DR), lambda i: (jnp.maximum(i * (tm // 32) - 1, 0), 0))
    return pl.pallas_call(
        kern, grid=(n_tiles,), in_specs=[tile(DR), nxt, tile(DR), prv, tile(D_IN), pl.BlockSpec((32, DR), lambda i: (0, 0))],
        out_specs=[tile(D), pl.BlockSpec((32, DR), lambda i: (0, 0))],
        out_shape=[jax.ShapeDtypeStruct((t_len, D), BF16), jax.ShapeDtypeStruct((32, DR), F32)],
        scratch_shapes=[pltpu.VMEM((tm + 32, DR), F32)] * 2 + [pltpu.VMEM((7, tm + 24, DR), F32)] * 2, name="conv_bwd",
        compiler_params=_params(("arbitrary",)))(dc, dc, glu, glu, p, dw32)


def _place():
    x, y, c = lax.axis_index("x"), lax.axis_index("y"), lax.axis_index("c")
    chips = [(1 - x, y), (x, 1 - y), (1 - x, 1 - y)]
    return x, y, c, chips


def _gather_weights(shards):
    n = len(shards)

    def body(*refs):
        start, middle, finish = _gather_plan(refs[:n], refs[n:2 * n], *refs[2 * n:])
        start()
        middle()
        finish()

    got = pl.pallas_call(body, in_specs=[ANY] * n, out_specs=[ANY] * n, out_shape=_gathered_shapes(shards),
                         scratch_shapes=_gather_sems(n), name="gather_weights")(*shards)
    return _fill_own(got, shards)


def _gathered_shapes(shards):
    return [jax.ShapeDtypeStruct((4,) + s.shape, s.dtype) for s in shards]


def _gather_sems(n):
    return [pltpu.SemaphoreType.DMA((6 * n,)), pltpu.SemaphoreType.DMA((6 * n,))]


def _fill_own(got, shards):
    me = 2 * lax.axis_index("x") + lax.axis_index("y")
    return [lax.dynamic_update_slice(g, s[None], (me, 0, 0, 0)) for g, s in zip(got, shards)]


def _gather_plan(src, dst, send, recv):
    n = len(src)
    x, y, c, chips = _place()
    me, sib = 2 * x + y, (x, y, 1 - c)

    def rcopy(k, sem, s_ref, d_ref, to):
        return pltpu.make_async_remote_copy(src_ref=s_ref, dst_ref=d_ref, send_sem=send.at[6 * k + sem],
                                            recv_sem=recv.at[6 * k + sem], device_id=to, device_id_type=MESH)

    def landed(k, m, half):
        return dst[k].at[2 * chips[m][0] + chips[m][1], half]

    first = [rcopy(k, m, src[k].at[c], dst[k].at[me, c], (*chips[m], c)) for k in range(n) for m in range(3)]
    passed = [rcopy(k, 3 + m, landed(k, m, c), landed(k, m, c), sib) for k in range(n) for m in range(3)]

    def start():
        for cp in first:
            cp.start()

    def middle():
        for k in range(n):
            for m in range(3):
                rcopy(k, m, landed(k, m, c), landed(k, m, c), sib).wait_recv()
                passed[3 * k + m].start()

    def finish():
        for k in range(n):
            for m in range(3):
                rcopy(k, 3 + m, landed(k, m, 1 - c), landed(k, m, 1 - c), sib).wait_recv()
        for cp in first + passed:
            cp.wait_send()

    return start, middle, finish


def _swap_halves(name, give):
    n = len(give)

    def body(*refs):
        src, got = refs[:n], refs[n:2 * n]
        send, recv = refs[2 * n:]
        x, y, c, _ = _place()
        copies = []
        for k in range(n):
            for j in range(4):
                copies.append(pltpu.make_async_remote_copy(
                    src_ref=src[k].at[j], dst_ref=got[k].at[j], send_sem=send.at[4 * k + j], recv_sem=recv.at[4 * k + j],
                    device_id=(x, y, 1 - c), device_id_type=MESH))
                copies[-1].start()
        for cp in copies:
            cp.wait()

    out_shape = [jax.ShapeDtypeStruct(g.shape, g.dtype) for g in give]
    return pl.pallas_call(body, in_specs=[ANY] * n, out_specs=[ANY] * n, out_shape=out_shape,
                          scratch_shapes=[pltpu.SemaphoreType.DMA((4 * n,)), pltpu.SemaphoreType.DMA((4 * n,))],
                          name=name)(*give)


def _owner_shapes(sums):
    return [jax.ShapeDtypeStruct((3,) + s.shape[1:], s.dtype) for s in sums]


def _owner_sems(n):
    return [pltpu.SemaphoreType.DMA((3 * n,)), pltpu.SemaphoreType.DMA((3 * n,))]


def _owners_plan(src, dst, send, recv):
    x, y, c, chips = _place()
    copies = [pltpu.make_async_remote_copy(
        src_ref=src[k].at[2 * chip[0] + chip[1]], dst_ref=dst[k].at[m], send_sem=send.at[3 * k + m],
        recv_sem=recv.at[3 * k + m], device_id=(*chip, c), device_id_type=MESH)
        for k in range(len(src)) for m, chip in enumerate(chips)]

    def start():
        for cp in copies:
            cp.start()

    def finish():
        for cp in copies:
            cp.wait()

    return start, finish


def _join_halves(halves):
    n = len(halves)

    def body(*refs):
        src, dst = refs[:n], refs[n:2 * n]
        send, recv = refs[2 * n:]
        x, y, c, _ = _place()
        copies = []
        for k in range(n):
            copies.append(pltpu.make_async_remote_copy(src_ref=src[k], dst_ref=dst[k], send_sem=send.at[k],
                                                       recv_sem=recv.at[k], device_id=(x, y, 1 - c), device_id_type=MESH))
            copies[-1].start()
        for cp in copies:
            cp.wait()

    out_shape = [jax.ShapeDtypeStruct(h.shape, h.dtype) for h in halves]
    return pl.pallas_call(body, in_specs=[ANY] * n, out_specs=[ANY] * n, out_shape=out_shape,
                          scratch_shapes=[pltpu.SemaphoreType.DMA((n,)), pltpu.SemaphoreType.DMA((n,))],
                          name="join_halves")(*halves)


def _allreduce_small(v):
    rows, n = v.shape

    def body(v_ref, o_ref, buf, send, recv):
        x, y, c, _ = _place()
        me = 4 * x + 2 * y + c
        buf[me] = v_ref[...]
        copies = []
        for d in range(1, 8):
            peer = (x ^ (d >> 2), y ^ ((d >> 1) & 1), c ^ (d & 1))
            cp = pltpu.make_async_remote_copy(src_ref=v_ref, dst_ref=buf.at[me], send_sem=send.at[d], recv_sem=recv.at[d],
                                              device_id=peer, device_id_type=MESH)
            cp.start()
            copies.append(cp)
        for d in range(1, 8):
            peer = 4 * (x ^ (d >> 2)) + 2 * (y ^ ((d >> 1) & 1)) + (c ^ (d & 1))
            pltpu.make_async_remote_copy(src_ref=v_ref, dst_ref=buf.at[peer], send_sem=send.at[d], recv_sem=recv.at[d],
                                         device_id=(x, y, c), device_id_type=MESH).wait_recv()
        for cp in copies:
            cp.wait_send()
        acc = buf[0]
        for d in range(1, 8):
            acc = acc + buf[d]
        o_ref[...] = acc

    vm = pl.BlockSpec(memory_space=pltpu.VMEM)
    return pl.pallas_call(body, in_specs=[vm], out_specs=vm, out_shape=jax.ShapeDtypeStruct((rows, n), F32),
                          scratch_shapes=[pltpu.VMEM((8, rows, n), F32), pltpu.SemaphoreType.DMA((8,)),
                                          pltpu.SemaphoreType.DMA((8,))], name="allreduce_small")(v)


def _add_call(name, parts, tm, out_dtype=F32):
    def fn(i, *vals):
        acc = vals[0].astype(F32)
        for v in vals[1:]:
            acc = acc + v.astype(F32)
        return (acc,), ()
    return _rows_call(name, fn, list(parts), [], [(parts[0].shape[1], out_dtype)], [], tm)[0]


def _adamw(name, w, g, m, v):
    c1 = 1.0 / (1.0 - ADAM_B1 ** ADAM_STEP)
    c2 = 1.0 / (1.0 - ADAM_B2 ** ADAM_STEP)

    def fn(i, wv, gv, mv, vv):
        m2 = ADAM_B1 * mv + (1.0 - ADAM_B1) * gv
        v2 = ADAM_B2 * vv + (1.0 - ADAM_B2) * jnp.square(gv)
        delta = -ADAM_LR * ((m2 * c1) / (jnp.sqrt(v2 * c2) + ADAM_EPS) + ADAM_WD * wv)
        return (delta, m2, v2), ()
    cols = w.shape[1]
    tm = _tile(w.shape[0], (256, 176, 128, 64, 8))
    return _rows_call(name, fn, [w, g, m, v], [], [(cols, F32)] * 3, [], tm)


def _canon(name, a, transposed):
    return a[0].T if transposed else a[0]


def _pack_sharded(w_up, a_up, g_up, conv_dw):
    parts = [w_up[0].T, a_up[0].T, g_up[0].T, conv_dw[0].T]
    used = sum(p.shape[1] for p in parts)
    return jnp.concatenate(parts + [jnp.zeros((parts[0].shape[0], PACK_W - used), F32)], axis=1)


def _unpack_sharded(a):
    return [a[:, 0:64].T[None], a[:, 64:128].T[None], a[:, 128:256].T[None], a[:, 256:256 + CW].T[None]]


def _pack_small(vals):
    flat = jnp.concatenate([v.reshape(-1) for v in vals] + [jnp.zeros((SMALL_PAD - SMALL_N,), F32)])
    return flat.reshape(8, SMALL_PAD // 8)


def _unpack_small(a, like):
    flat, out, off = a.reshape(-1), [], 0
    for (_, n), ref in zip(SMALL, like):
        out.append(flat[off:off + n].reshape(ref.shape))
        off += n
    return out


def kernel(x, ffn1_norm_pre, ffn1_norm_post, ffn1_w_gu, ffn1_w_down, mix_norm_pre, mix_norm_post, w_in, shift_mu, w_up, w0, a_up, a0, g_up, k_k, k_a, r_k, gn_w, gn_b, conv_dw, conv_b, conv_ln_w, conv_ln_b, w_out, ffn2_norm_pre, ffn2_norm_post, ffn2_w_gu, ffn2_w_down, loss_target, m_ffn1_norm_pre, m_ffn1_norm_post, m_ffn1_w_gu, m_ffn1_w_down, m_mix_norm_pre, m_mix_norm_post, m_w_in, m_shift_mu, m_w_up, m_w0, m_a_up, m_a0, m_g_up, m_k_k, m_k_a, m_r_k, m_gn_w, m_gn_b, m_conv_dw, m_conv_b, m_conv_ln_w, m_conv_ln_b, m_w_out, m_ffn2_norm_pre, m_ffn2_norm_post, m_ffn2_w_gu, m_ffn2_w_down, v_ffn1_norm_pre, v_ffn1_norm_post, v_ffn1_w_gu, v_ffn1_w_down, v_mix_norm_pre, v_mix_norm_post, v_w_in, v_shift_mu, v_w_up, v_w0, v_a_up, v_a0, v_g_up, v_k_k, v_k_a, v_r_k, v_gn_w, v_gn_b, v_conv_dw, v_conv_b, v_conv_ln_w, v_conv_ln_b, v_w_out, v_ffn2_norm_pre, v_ffn2_norm_post, v_ffn2_w_gu, v_ffn2_w_down):
    w = dict(ffn1_norm_pre=ffn1_norm_pre, ffn1_norm_post=ffn1_norm_post, ffn1_w_gu=ffn1_w_gu, ffn1_w_down=ffn1_w_down, mix_norm_pre=mix_norm_pre, mix_norm_post=mix_norm_post, w_in=w_in, shift_mu=shift_mu, w_up=w_up, w0=w0, a_up=a_up, a0=a0, g_up=g_up, k_k=k_k, k_a=k_a, r_k=r_k, gn_w=gn_w, gn_b=gn_b, conv_dw=conv_dw, conv_b=conv_b, conv_ln_w=conv_ln_w, conv_ln_b=conv_ln_b, w_out=w_out, ffn2_norm_pre=ffn2_norm_pre, ffn2_norm_post=ffn2_norm_post, ffn2_w_gu=ffn2_w_gu, ffn2_w_down=ffn2_w_down)
    mom = dict(ffn1_norm_pre=m_ffn1_norm_pre, ffn1_norm_post=m_ffn1_norm_post, ffn1_w_gu=m_ffn1_w_gu, ffn1_w_down=m_ffn1_w_down, mix_norm_pre=m_mix_norm_pre, mix_norm_post=m_mix_norm_post, w_in=m_w_in, shift_mu=m_shift_mu, w_up=m_w_up, w0=m_w0, a_up=m_a_up, a0=m_a0, g_up=m_g_up, k_k=m_k_k, k_a=m_k_a, r_k=m_r_k, gn_w=m_gn_w, gn_b=m_gn_b, conv_dw=m_conv_dw, conv_b=m_conv_b, conv_ln_w=m_conv_ln_w, conv_ln_b=m_conv_ln_b, w_out=m_w_out, ffn2_norm_pre=m_ffn2_norm_pre, ffn2_norm_post=m_ffn2_norm_post, ffn2_w_gu=m_ffn2_w_gu, ffn2_w_down=m_ffn2_w_down)
    var = dict(ffn1_norm_pre=v_ffn1_norm_pre, ffn1_norm_post=v_ffn1_norm_post, ffn1_w_gu=v_ffn1_w_gu, ffn1_w_down=v_ffn1_w_down, mix_norm_pre=v_mix_norm_pre, mix_norm_post=v_mix_norm_post, w_in=v_w_in, shift_mu=v_shift_mu, w_up=v_w_up, w0=v_w0, a_up=v_a_up, a0=v_a0, g_up=v_g_up, k_k=v_k_k, k_a=v_k_a, r_k=v_r_k, gn_w=v_gn_w, gn_b=v_gn_b, conv_dw=v_conv_dw, conv_b=v_conv_b, conv_ln_w=v_conv_ln_w, conv_ln_b=v_conv_ln_b, w_out=v_w_out, ffn2_norm_pre=v_ffn2_norm_pre, ffn2_norm_post=v_ffn2_norm_post, ffn2_w_gu=v_ffn2_w_gu, ffn2_w_down=v_ffn2_w_down)
    names = list(w)

    x0 = x[0]
    tgt = loss_target[0]
    t_len = x0.shape[0]
    tm = _tile(t_len, (512, 256, 128))

    half = {n: rows // 8 for n, rows, _ in BIG}
    pack_local = _pack_sharded(w["w_up"], w["a_up"], w["g_up"], w["conv_dw"])
    shard = {n: _canon(n, w[n], tr).astype(BF16).reshape(2, half[n], D) for n, _, tr in BIG}
    rows_of = {n: rows for n, rows, _ in BIG}
    full = {"ffn1_w_gu": _gather_weights([shard["ffn1_w_gu"]])[0].reshape(2 * DFF, D)}

    h1, gu1, act1, f1, gathered = _ffn_fwd("ffn1", x0, w["ffn1_norm_pre"], full["ffn1_w_gu"], None, tm,
                                           gather=[shard[n] for n in SECOND] + [pack_local.reshape(2, 64, PACK_W)])
    full.update({n: g.reshape(rows_of[n], D) for n, g in zip(SECOND, gathered)})
    pack = gathered[-1].reshape(DR, PACK_W)
    p01, p2 = pack[:, 0:128], pack[:, 128:256]
    dw32 = jnp.concatenate([pack[:, 256:256 + CW].T, jnp.zeros((1, DR), F32)], axis=0)
    (x1,) = _rows_call("ffn1_resid", lambda i, xv, fv, g: ((xv + 0.5 * _rms(fv, g),), ()), [x0, f1],
                       [w["ffn1_norm_post"]], [(D, F32)], [], tm)
    (hm,) = _rows_call("mix_norm", lambda i, xv, g: ((_rms(xv, g),), ()), [x1], [w["mix_norm_pre"]], [(D, BF16)], [], tm)
    p = _matmul("mix_in", hm, full["w_in"], "nt")
    ps = _shift_fwd(p, w["shift_mu"], tm)
    rkc = w["r_k"].reshape(1, DR)
    prep_consts = [w["w0"], w["a0"], w["k_k"], w["k_a"], p01, p2]
    r_, dec, k2, v_, z_, b_, g_ = _rows_call(
        "prep", lambda i, psv, *cs: (_prep(psv, *cs), ()), [ps], prep_consts, [(DR, F32)] * 7, [], min(tm, 256))
    y, ck, late = _rec_fwd(r_, dec, k2, z_, b_, v_, [shard[n] for n in LATE])
    full.update({n: g.reshape(rows_of[n], D) for n, g in zip(LATE, late)})
    glu, cpre, ob = _conv_fwd(p, dw32, w["conv_b"], w["conv_ln_w"], w["conv_ln_b"], tm)
    post_consts = [w["gn_w"], w["gn_b"], rkc]
    (o,) = _rows_call(
        "post", lambda i, yv, rv, kv, vv, gv, obv, *cs: ((jnp.concatenate([_post(yv, rv, kv, vv, gv, *cs).astype(BF16), obv], axis=1),), ()),
        [y, r_, k2, v_, g_, ob], post_consts, [(D, BF16)], [], min(tm, 256))
    mo = _matmul("mix_out", o, full["w_out"], "nn")
    (x2,) = _rows_call("mix_resid", lambda i, xv, fv, g: ((xv + _rms(fv, g),), ()), [x1, mo],
                       [w["mix_norm_post"]], [(D, F32)], [], tm)
    h2, gu2, act2, f2, _ = _ffn_fwd("ffn2", x2, w["ffn2_norm_pre"], full["ffn2_w_gu"], full["ffn2_w_down"], tm)

    def loss_fn(i, xv, fv, tv, g):
        err = xv + 0.5 * _rms(fv, g) - tv
        part = 0.5 * jnp.sum(jnp.mean(err * err, axis=-1, keepdims=True), axis=0, keepdims=True)
        return (err * (1.0 / D),), (jnp.broadcast_to(part, (8, 128)),)
    dx3, loss_part = _rows_call("loss", loss_fn, [x2, f2, tgt], [w["ffn2_norm_post"]], [(D, F32)], [(8, 128)], tm)
    loss = lax.psum(loss_part[0, 0], ("x", "y", "c"))

    g_small = {}
    dx2, g_small["ffn2_norm_pre"], g_small["ffn2_norm_post"], dgu2_t, dwd2, _ = _ffn_bwd(
        "ffn2", x2, w["ffn2_norm_pre"], w["ffn2_norm_post"], 0.5, h2, gu2, act2, f2, dx3, full["ffn2_w_gu"], full["ffn2_w_down"], tm)

    def mix_resid_b(i, fv, dv, g):
        _, vjp = jax.vjp(_rms, fv, g)
        df, dg = vjp(dv)
        return (df,), (dg,)
    dmo, g_small["mix_norm_post"] = _rows_call("mix_resid_b", mix_resid_b, [mo, dx2], [w["mix_norm_post"]],
                                               [(D, BF16)], [(1, D)], tm)
    do = _matmul("mix_do", dmo, full["w_out"], "nt")
    dw_out = _matmul("mix_dwout", o, dmo, "tn")

    def post_b(i, yv, rv, kv, vv, gv, dov, *cs):
        _, vjp = jax.vjp(_post, yv, rv, kv, vv, gv, *cs)
        dy, dr, dk, dv, dg, dgw, dgb, drk = vjp(dov[:, :DR])
        return (dy, dr, dk, dv, dg), (_colsum(dgw), _colsum(dgb), _colsum(drk))
    dy, dr1, dk1, dv1, dg, g_small["gn_w"], g_small["gn_b"], g_small["r_k"] = _rows_call(
        "post_b", post_b, [y, r_, k2, v_, g_, do], post_consts, [(DR, F32)] * 5, [(1, DR)] * 3, min(tm, 256))
    my_c = lax.axis_index("c")
    my_chip = 2 * lax.axis_index("x") + lax.axis_index("y")
    g_big = dict(w_out=dw_out, ffn2_w_gu=dgu2_t, ffn2_w_down=dwd2)

    def halves_of(group, which):
        return jnp.concatenate([lax.dynamic_index_in_dim(g_big[n].reshape(4, 2, half[n], D), which, 1, keepdims=False)
                                for n in group], axis=1)

    def pair_sums(tag, group, extra_mine=(), extra_give=()):
        mine = halves_of(group, my_c)
        got, *got_x = _swap_halves("swap_halves_" + tag, [halves_of(group, 1 - my_c).astype(BF16)] + list(extra_give))
        rows = mine.shape[1]
        tile = _tile(rows, (352, 592, 16))
        travels = _add_call("add_pair_" + tag, [mine.reshape(4 * rows, D), got.reshape(4 * rows, D)], tile, BF16)
        return mine, got, travels.reshape(4, rows, D), got_x

    def owner_sum(tag, mine, got, parts):
        own = [lax.dynamic_index_in_dim(a, my_chip, 0, keepdims=False) for a in (mine, got)]
        return _add_call("add_chips_" + tag, own + [parts[m] for m in range(3)], _tile(mine.shape[1], (352, 592, 16)))

    mine_l, got_l, sum_l, _ = pair_sums("late", LATE)
    (dr2, ddec, dk2, dz, db, dv2), (parts_l,) = _rec_bwd(r_, dec, k2, z_, b_, v_, dy, ck, [sum_l])

    def prep_b(i, psv, a1, a2, c1, c2, e1, e2, dwv, dzv, dbv, dgv, *cs):
        _, vjp = jax.vjp(_prep, psv, *cs)
        dps, dw0, da0, dkk, dka, dp01, dp2 = vjp((a1 + a2, dwv, c1 + c2, e1 + e2, dzv, dbv, dgv))
        return (dps,), (_colsum(dw0), _colsum(da0), _colsum(dkk), _colsum(dka), dp01, dp2)
    dps, g_small["w0"], g_small["a0"], g_small["k_k"], g_small["k_a"], dp01, dp2 = _rows_call(
        "prep_b", prep_b, [ps, dr1, dr2, dk1, dk2, dv1, dv2, ddec, dz, db, dg], prep_consts, [(D_SHIFT, F32)],
        [(1, DR)] * 4 + [(DR, 128)] * 2, min(tm, 256))

    def convln_b(i, cv, dov, lw, lb):
        _, vjp = jax.vjp(_ln_silu, cv, lw, lb)
        dc, dlw, dlb = vjp(dov[:, DR:])
        return (dc,), (_colsum(dc), _colsum(dlw), _colsum(dlb))
    dc, g_small["conv_b"], g_small["conv_ln_w"], g_small["conv_ln_b"] = _rows_call(
        "convln_b", convln_b, [cpre, do], [w["conv_ln_w"], w["conv_ln_b"]], [(DR, F32)], [(1, DR)] * 3, tm)
    dpc, ddw32 = _conv_bwd(dc, glu, p, dw32, tm)
    dp, g_small["shift_mu"] = _shift_bwd(dps, p, dpc, w["shift_mu"], tm)
    dhm = _matmul("mix_dh", dp, full["w_in"], "nn")
    dw_in_t = _matmul("mix_dwin", dp, hm, "tn")

    def norm_b(i, xv, dhv, dv, g):
        _, vjp = jax.vjp(_rms, xv, g)
        dx, dg_ = vjp(dhv)
        return (dx + dv,), (dg_,)
    dx1, g_small["mix_norm_pre"] = _rows_call("mix_norm_b", norm_b, [x1, dhm, dx2], [w["mix_norm_pre"]],
                                              [(D, F32)], [(1, D)], tm)
    kept = {}

    def second_sums(dwd1):
        g_big.update(ffn1_w_down=dwd1, w_in=dw_in_t)
        pack_grads = jnp.concatenate([dp01, dp2, ddw32.T, jnp.zeros((DR, PACK_W - 288), F32)], axis=1).reshape(4, 2, 64, PACK_W)
        mine_p, give_p = (lax.dynamic_index_in_dim(pack_grads, which, 1, keepdims=False) for which in (my_c, 1 - my_c))
        mine, got, travels, (got_p,) = pair_sums("second", SECOND, extra_give=[give_p])
        sum_p = _add_call("add_pair_pack", [mine_p.reshape(256, PACK_W), got_p.reshape(256, PACK_W)], 256).reshape(4, 64, PACK_W)
        kept.update(second=(mine, got), sum_p=sum_p)
        return [travels, sum_p]

    def first_sums(dgu1_t):
        g_big.update(ffn1_w_gu=dgu1_t)
        mine, got, travels, _ = pair_sums("first", ("ffn1_w_gu",))
        kept.update(first=(mine, got))
        return [travels]

    grad_x, g_small["ffn1_norm_pre"], g_small["ffn1_norm_post"], _, _, ((parts_s, parts_p), (parts_f,)) = _ffn_bwd(
        "ffn1", x0, w["ffn1_norm_pre"], w["ffn1_norm_post"], 0.5, h1, gu1, act1, f1, dx1, full["ffn1_w_gu"], full["ffn1_w_down"], tm,
        after_down=second_sums, after_gu=first_sums)

    sum_p = kept["sum_p"]
    fin_s = owner_sum("second", *kept["second"], parts_s)
    fin_f = owner_sum("first", *kept["first"], parts_f)
    fin_l = owner_sum("late", mine_l, got_l, parts_l)
    fin_p = _add_call("add_chips_pack", [lax.dynamic_index_in_dim(sum_p, my_chip, 0, keepdims=False)] + [parts_p[m] for m in range(3)], 64)
    fins = [fin_f, fin_s, fin_l, fin_p]
    red_f, red_s, red_l, red_p = [jnp.where(my_c == 0, jnp.stack([f, s]), jnp.stack([s, f])) for f, s in zip(fins, _join_halves(fins))]
    small_sum = _allreduce_small(_pack_small([g_small[n] for n, _ in SMALL]))

    grads, delta, new_m, new_v = {}, {}, {}, {}
    reduced = {}
    for group, red in ((("ffn1_w_gu",), red_f), (SECOND, red_s), (LATE, red_l)):
        off = 0
        for n in group:
            reduced[n] = red[:, off:off + half[n], :].reshape(rows_of[n] // 4, D)
            off += half[n]
    for n, rows, tr in BIG:
        g = reduced[n]
        g = (g.T if tr else g)[None]
        grads[n] = g
        d_, m_, v2_ = _adamw("adamw_" + n, w[n][0], g[0], mom[n][0], var[n][0])
        delta[n], new_m[n], new_v[n] = d_[None], m_[None], v2_[None]
    sh = ("w_up", "a_up", "g_up", "conv_dw")
    g_pack = red_p.reshape(128, PACK_W)
    d_, m_, v2_ = _adamw("adamw_pack", pack_local, g_pack, _pack_sharded(*[mom[n] for n in sh]), _pack_sharded(*[var[n] for n in sh]))
    for dst, src in ((grads, g_pack), (delta, d_), (new_m, m_), (new_v, v2_)):
        for n, a in zip(sh, _unpack_sharded(src)):
            dst[n] = a
    sm = [n for n, _ in SMALL]
    d_, m_, v2_ = _adamw("adamw_small", _pack_small([w[n] for n in sm]), small_sum, _pack_small([mom[n] for n in sm]),
                         _pack_small([var[n] for n in sm]))
    like = [w[n] for n in sm]
    for dst, src in ((grads, small_sum), (delta, d_), (new_m, m_), (new_v, v2_)):
        for n, a in zip(sm, _unpack_small(src, like)):
            dst[n] = a

    wn = names
    return (loss, grad_x[None], *[grads[n] for n in wn], *[delta[n] for n in wn], *[new_m[n] for n in wn],
            *[new_v[n] for n in wn])
```

```python
import functools
import math

import jax
import jax.numpy as jnp
from jax import lax
from jax.experimental import pallas as pl
from jax.experimental.pallas import tpu as pltpu

F32 = jnp.float32
BF16 = jnp.bfloat16

D = 1024
DFF = 2816
DR = 512
HS = 64
D_SHIFT = 1792
D_IN = 2816
CW = 31
RMS_EPS = 1e-6
GN_EPS = 64e-5
LN_EPS = 1e-5
DECAY_SCALE = math.exp(-0.5)
ADAM_LR, ADAM_B1, ADAM_B2, ADAM_EPS, ADAM_WD, ADAM_STEP = 0.001, 0.9, 0.999, 1e-8, 0.01, 10

REC_TILE = 128
VMEM_LIMIT = 56 * 1024 * 1024
MESH = pl.DeviceIdType.MESH
ANY = pl.BlockSpec(memory_space=pl.ANY)

BIG = (("ffn1_w_gu", 2 * DFF, True), ("ffn1_w_down", DFF, False), ("w_in", D_IN, True),
       ("w_out", D, False), ("ffn2_w_gu", 2 * DFF, True), ("ffn2_w_down", DFF, False))
SECOND = ("ffn1_w_down", "w_in")
LATE = ("w_out", "ffn2_w_gu", "ffn2_w_down")
PACK_W = 384
SMALL = (("ffn1_norm_pre", D), ("ffn1_norm_post", D), ("mix_norm_pre", D), ("mix_norm_post", D),
         ("shift_mu", D_SHIFT), ("w0", DR), ("a0", DR), ("k_k", DR), ("k_a", DR), ("r_k", DR),
         ("gn_w", DR), ("gn_b", DR), ("conv_b", DR), ("conv_ln_w", DR), ("conv_ln_b", DR),
         ("ffn2_norm_pre", D), ("ffn2_norm_post", D))
SMALL_N = sum(n for _, n in SMALL)
SMALL_PAD = 8 * 1664


def _params(sem):
    return pltpu.CompilerParams(dimension_semantics=sem, vmem_limit_bytes=VMEM_LIMIT)


def _tile(n, prefs):
    for p in prefs:
        if n % p == 0:
            return p
    return n


def _rows_call(name, fn, rows, consts, row_outs, acc_outs, tm):
    specs, arrs = [], []
    for r in rows:
        if isinstance(r, tuple):
            a, bs, im = r
            specs.append(pl.BlockSpec(bs, im))
        else:
            a = r
            specs.append(pl.BlockSpec((tm, a.shape[1]), lambda i: (i, 0)))
        arrs.append(a)
    t_rows = arrs[0].shape[0]
    for c in consts:
        specs.append(pl.BlockSpec(c.shape, functools.partial(lambda i, n: (0,) * n, n=c.ndim)))
        arrs.append(c)
    n_in, n_o, n_a = len(arrs), len(row_outs), len(acc_outs)

    def kern(*refs):
        i = pl.program_id(0)
        vals = [r[...] for r in refs[:n_in]]
        ro, ao = fn(i, *vals)
        outs = refs[n_in:]
        for k in range(n_o):
            outs[k][...] = ro[k].astype(outs[k].dtype)
        if n_a:
            @pl.when(i == 0)
            def _():
                for k in range(n_a):
                    outs[n_o + k][...] = jnp.zeros(outs[n_o + k].shape, F32)
            for k in range(n_a):
                outs[n_o + k][...] += ao[k]

    out_specs = [pl.BlockSpec((tm, w), lambda i: (i, 0)) for (w, _) in row_outs]
    out_specs += [pl.BlockSpec(s, functools.partial(lambda i, n: (0,) * n, n=len(s))) for s in acc_outs]
    out_shape = [jax.ShapeDtypeStruct((t_rows, w), dt) for (w, dt) in row_outs]
    out_shape += [jax.ShapeDtypeStruct(s, F32) for s in acc_outs]
    return pl.pallas_call(kern, grid=(t_rows // tm,), in_specs=specs, out_specs=out_specs, out_shape=out_shape,
                          name=name, compiler_params=_params(("arbitrary",)))(*arrs)


_DIMS = {"nn": (((1,), (0,)), ((), ())), "nt": (((1,), (1,)), ((), ())), "tn": (((0,), (0,)), ((), ()))}
_LANE_TILES = (1408, 1024, 512, 384, 256, 128)


def _matmul(name, a, b, mode, out_dtype=F32, owners=None, gather=None):
    if mode == "nn":
        (m, k), (_, n) = a.shape, b.shape
    elif mode == "nt":
        (m, k), (n, _) = a.shape, b.shape
    else:
        (k, m), (_, n) = a.shape, b.shape
    if mode == "tn":
        tm, tk = _tile(m, _LANE_TILES), _tile(k, (512, 256, 128))
    else:
        tm, tk = _tile(m, (1024, 512, 256, 128)), _tile(k, _LANE_TILES)
    tn = _tile(n, _LANE_TILES)
    nk = k // tk
    assert out_dtype == F32 or nk == 1

    owners, gather = list(owners or ()), list(gather or ())
    assert not (owners and gather)
    riders = owners + gather
    ns = len(riders)
    grid = (m // tm, n // tn, nk)
    steps = grid[0] * grid[1] * nk

    def kern(a_ref, b_ref, *rest):
        o_ref = rest[ns]
        if ns:
            step = (pl.program_id(0) * grid[1] + pl.program_id(1)) * nk + pl.program_id(2)
            if owners:
                start, finish = _owners_plan(rest[:ns], rest[ns + 1:2 * ns + 1], *rest[2 * ns + 1:])
                pl.when(step == 0)(start)
            else:
                start, middle, finish = _gather_plan(rest[:ns], rest[ns + 1:2 * ns + 1], *rest[2 * ns + 1:])
                pl.when(step == 0)(start)
                pl.when(step == steps // 2)(middle)

        def part():
            return lax.dot_general(a_ref[...].astype(BF16), b_ref[...].astype(BF16), _DIMS[mode], preferred_element_type=F32)

        if nk == 1:
            o_ref[...] = part().astype(o_ref.dtype)
        else:
            @pl.when(pl.program_id(2) == 0)
            def _():
                o_ref[...] = jnp.zeros(o_ref.shape, F32)

            o_ref[...] += part()
        if ns:
            pl.when(step == steps - 1)(finish)

    a_spec = pl.BlockSpec((tk, tm), lambda i, j, q: (q, i)) if mode == "tn" else pl.BlockSpec((tm, tk), lambda i, j, q: (i, q))
    b_spec = pl.BlockSpec((tn, tk), lambda i, j, q: (j, q)) if mode == "nt" else pl.BlockSpec((tk, tn), lambda i, j, q: (q, j))
    extra_shapes = _owner_shapes(owners) if owners else _gathered_shapes(gather)
    extra_sems = [] if not ns else _owner_sems(ns) if owners else _gather_sems(ns)
    out, *arrived = pl.pallas_call(
        kern, grid=grid, in_specs=[a_spec, b_spec] + [ANY] * ns,
        out_specs=[pl.BlockSpec((tm, tn), lambda i, j, q: (i, j))] + [ANY] * ns,
        out_shape=[jax.ShapeDtypeStruct((m, n), out_dtype)] + extra_shapes,
        scratch_shapes=extra_sems, name=name,
        compiler_params=_params(("arbitrary", "arbitrary", "arbitrary")))(a, b, *riders)
    if gather:
        arrived = _fill_own(arrived, gather)
    return (out, arrived) if ns else out


def _rms(x, g):
    return x * lax.rsqrt(jnp.mean(x * x, axis=-1, keepdims=True) + RMS_EPS) * g


def _silu(x):
    return x * jax.nn.sigmoid(x)


def _bones(n):
    r = lax.broadcasted_iota(jnp.int32, (n, n), 0) // HS
    c = lax.broadcasted_iota(jnp.int32, (n, n), 1) // HS
    return (r == c).astype(BF16)


@jax.custom_vjp
def _segsum(x):
    bones = _bones(x.shape[1])
    hi = x.astype(BF16)
    lo = (x - hi.astype(F32)).astype(BF16)
    return jnp.dot(hi, bones, preferred_element_type=F32) + jnp.dot(lo, bones, preferred_element_type=F32)


_segsum.defvjp(lambda x: (_segsum(x), None), lambda _, ct: (_segsum(ct),))


@jax.custom_vjp
def _dot_nt(x, w):
    return lax.dot_general(x.astype(BF16), w.astype(BF16), _DIMS["nt"], preferred_element_type=F32)


def _dot_nt_bwd(res, ct):
    x, w = res
    ctb = ct.astype(BF16)
    dx = lax.dot_general(ctb, w.astype(BF16), _DIMS["nn"], preferred_element_type=F32)
    dw = lax.dot_general(ctb, x.astype(BF16), _DIMS["tn"], preferred_element_type=F32)
    return dx, dw


_dot_nt.defvjp(lambda x, w: (_dot_nt(x, w), (x, w)), _dot_nt_bwd)


def _prep(ps, w0, a0, k_k, k_a, p01, p2):
    r, k, v = ps[:, :DR], ps[:, DR:2 * DR], ps[:, 2 * DR:3 * DR]
    wa, xg = ps[:, 3 * DR:3 * DR + 128], ps[:, 3 * DR + 128:]
    first = lax.broadcasted_iota(jnp.int32, (1, 128), 1) < 64
    d = w0 + _dot_nt(jnp.where(first, jnp.tanh(wa), 0.0), p01)
    decay = jnp.exp(-DECAY_SCALE * jax.nn.sigmoid(d))
    a = jax.nn.sigmoid(a0 + _dot_nt(jnp.where(first, 0.0, wa), p01))
    g = _dot_nt(jax.nn.sigmoid(xg), p2)
    kk = k * k_k
    kk = kk * lax.rsqrt(jnp.maximum(_segsum(kk * kk), 1e-12))
    k2 = k * (1.0 + (a - 1.0) * k_a)
    return r, decay, k2, v, -kk, kk * a, g


def _post(y, r, k, v, g, gn_w, gn_b, r_k):
    mu = _segsum(y) * (1.0 / HS)
    yc = y - mu
    var = _segsum(yc * yc) * (1.0 / HS)
    yn = yc * lax.rsqrt(var + GN_EPS) * gn_w + gn_b
    return (yn + _segsum(r * k * r_k) * v) * g


def _ln_silu(c, w, b):
    mu = jnp.mean(c, axis=-1, keepdims=True)
    var = jnp.mean(jnp.square(c - mu), axis=-1, keepdims=True)
    return _silu((c - mu) * lax.rsqrt(var + LN_EPS) * w + b)


def _colsum(x):
    return jnp.sum(x, axis=0, keepdims=True)


def _gu_swiglu(name, h, w_gu_t, gather=None):
    m = h.shape[0]
    tm, tn = _tile(m, (512, 256, 128)), 1408
    nj = DFF // tn
    gather = list(gather or ())
    ns = len(gather)
    steps = (m // tm) * nj

    def kern(a_ref, bg_ref, bu_ref, *rest):
        gate_ref, up_ref, act_ref = rest[ns:ns + 3]
        if ns:
            step = pl.program_id(0) * nj + pl.program_id(1)
            start, middle, finish = _gather_plan(rest[:ns], rest[ns + 3:2 * ns + 3], *rest[2 * ns + 3:])
            pl.when(step == 0)(start)
            pl.when(step == steps // 2)(middle)
        a = a_ref[...]
        gate = lax.dot_general(a, bg_ref[...], _DIMS["nt"], preferred_element_type=F32)
        up = lax.dot_general(a, bu_ref[...], _DIMS["nt"], preferred_element_type=F32)
        gate_ref[...] = gate
        up_ref[...] = up
        act_ref[...] = (_silu(gate) * up).astype(BF16)
        if ns:
            pl.when(step == steps - 1)(finish)

    tile = pl.BlockSpec((tm, tn), lambda i, j: (i, j))
    gate, up, act, *got = pl.pallas_call(
        kern, grid=(m // tm, nj),
        in_specs=[pl.BlockSpec((tm, D), lambda i, j: (i, 0)), pl.BlockSpec((tn, D), lambda i, j: (j, 0)),
                  pl.BlockSpec((tn, D), lambda i, j: (j + nj, 0))] + [ANY] * ns,
        out_specs=[tile] * 3 + [ANY] * ns,
        out_shape=[jax.ShapeDtypeStruct((m, DFF), F32)] * 2 + [jax.ShapeDtypeStruct((m, DFF), BF16)] + _gathered_shapes(gather),
        scratch_shapes=_gather_sems(ns) if ns else [], name=name,
        compiler_params=_params(("arbitrary", "arbitrary")))(h, w_gu_t, w_gu_t, *gather)
    return gate, up, act, _fill_own(got, gather)


def _dact_swiglu_b(name, df, w_down, gate, up):
    m = df.shape[0]
    tm, tn = _tile(m, (512, 256, 128)), 1408
    nj = DFF // tn

    def kern(df_ref, wd_ref, gate_ref, up_ref, o_ref):
        dact = lax.dot_general(df_ref[...], wd_ref[...], _DIMS["nt"], preferred_element_type=F32)
        _, vjp = jax.vjp(lambda a, b: _silu(a) * b, gate_ref[...], up_ref[...])
        dgate, dup = vjp(dact)
        for j in range(nj):
            @pl.when(pl.program_id(1) == j)
            def _(j=j):
                o_ref[:, tn * j:tn * (j + 1)] = dgate.astype(BF16)
                o_ref[:, DFF + tn * j:DFF + tn * (j + 1)] = dup.astype(BF16)

    tile = pl.BlockSpec((tm, tn), lambda i, j: (i, j))
    return pl.pallas_call(
        kern, grid=(m // tm, nj),
        in_specs=[pl.BlockSpec((tm, D), lambda i, j: (i, 0)), pl.BlockSpec((tn, D), lambda i, j: (j, 0)), tile, tile],
        out_specs=pl.BlockSpec((tm, 2 * DFF), lambda i, j: (i, 0)),
        out_shape=jax.ShapeDtypeStruct((m, 2 * DFF), BF16), name=name,
        compiler_params=_params(("arbitrary", "arbitrary")))(df, w_down, gate, up)


def _ffn_fwd(tag, x, pre, w_gu_t, w_down, tm, gather=None, h=None):
    if h is None:
        (h,) = _rows_call(tag + "_norm", lambda i, xv, g: ((_rms(xv, g),), ()), [x], [pre], [(D, BF16)], [], tm)
    gate, up, act, gathered = _gu_swiglu(tag + "_gu", h, w_gu_t, gather)
    if gather:
        w_down = gathered[0].reshape(DFF, D)
    f = _matmul(tag + "_down", act, w_down, "nn")
    return h, (gate, up), act, f, gathered


def _ffn_bwd(tag, x, pre, post, scale, h, gu, act, f, dxo, w_gu_t, w_down, tm, after_down=None, after_gu=None):
    def resid_b(i, fv, dv, g):
        _, vjp = jax.vjp(lambda a, b: scale * _rms(a, b), fv, g)
        df, dg = vjp(dv)
        return (df,), (dg,)
    df, dpost = _rows_call(tag + "_resid_b", resid_b, [f, dxo], [post], [(D, BF16)], [(1, D)], tm)
    dgu = _dact_swiglu_b(tag + "_dact", df, w_down, *gu)
    dw_down = _matmul(tag + "_dwdown", act, df, "tn")
    sums = after_down(dw_down) if after_down else []
    dw_gu_t = _matmul(tag + "_dwgu", dgu, h, "tn", owners=sums)
    dw_gu_t, parts_down = dw_gu_t if sums else (dw_gu_t, [])
    sums = after_gu(dw_gu_t) if after_gu else []
    dh = _matmul(tag + "_dh", dgu, w_gu_t, "nn", owners=sums)
    dh, parts_gu = dh if sums else (dh, [])
    parts = (parts_down, parts_gu)

    def norm_b(i, xv, dhv, dv, g):
        _, vjp = jax.vjp(_rms, xv, g)
        dx, dg = vjp(dhv)
        return (dx + dv,), (dg,)
    dx, dpre = _rows_call(tag + "_norm_b", norm_b, [x, dh, dxo], [pre], [(D, F32)], [(1, D)], tm)
    return dx, dpre, dpost, dw_gu_t, dw_down, parts


def _pair_bcast(cols, first):
    return jnp.concatenate([jnp.where(first, cols[2 * p], cols[2 * p + 1]) for p in range(4)], axis=1)


def _head_sums(x, first):
    cols = []
    for p in range(4):
        xp = x[:, 128 * p:128 * (p + 1)]
        cols.append(jnp.sum(jnp.where(first, xp, 0.0), axis=1, keepdims=True))
        cols.append(jnp.sum(jnp.where(first, 0.0, xp), axis=1, keepdims=True))
    return cols


def _split16(x8):
    hi = x8.astype(BF16).astype(F32)
    return jnp.concatenate([hi, x8 - hi], axis=0).astype(BF16)


def _cols8(x8, e16):
    return lax.dot_general(_split16(x8), e16, _DIMS["tn"], preferred_element_type=F32)


def _pair_rows(c, first):
    return jnp.concatenate([jnp.where(first, c[128 * p:128 * p + HS], c[128 * p + HS:128 * (p + 1)]) for p in range(4)], axis=1)


def _rows8(prod, bones, dmask):
    x = prod.astype(BF16)
    full = jnp.concatenate([jnp.dot(x[:, 256 * q:256 * (q + 1)], bones, preferred_element_type=F32) for q in range(2)], axis=1)
    return jnp.concatenate([_colsum(full[HS * j:HS * (j + 1)] * dmask) for j in range(8)], axis=0)


def _rec_step(s, wr, zr, br, kr, vc, first):
    u = _pair_bcast(_head_sums(s * zr, first), first)
    return s * wr + u * br + vc * kr, u


def _rec_consts():
    e16 = (lax.broadcasted_iota(jnp.int32, (16, 1024), 0) % 8 == lax.broadcasted_iota(jnp.int32, (16, 1024), 1) // 128)
    dmask = lax.broadcasted_iota(jnp.int32, (HS, DR), 0) == lax.broadcasted_iota(jnp.int32, (HS, DR), 1) % HS
    return e16.astype(BF16), _bones(256), dmask.astype(F32)


def _const_spec(a):
    return pl.BlockSpec(a.shape, functools.partial(lambda i, n: (0,) * n, n=a.ndim))


def _rec_fwd(r, w, k, z, b, v, shards):
    t_len = r.shape[0]
    nt = t_len // REC_TILE
    ns = len(shards)
    consts = _rec_consts()

    def kern(r_ref, w_ref, k_ref, z_ref, b_ref, v_ref, e_ref, bones_ref, dm_ref, *rest):
        y_ref, ck_ref, states, u_ref = rest[ns:ns + 4]
        s_ref, prod = rest[2 * ns + 4:2 * ns + 6]
        start, middle, finish = _gather_plan(rest[:ns], rest[ns + 4:2 * ns + 4], *rest[2 * ns + 6:])
        i = pl.program_id(0)

        @pl.when(i == 0)
        def _():
            s_ref[...] = jnp.zeros(s_ref.shape, F32)
            start()

        pl.when(i == nt // 2)(middle)
        ck_ref[0] = s_ref[...]
        first = lax.broadcasted_iota(jnp.int32, (1, 128), 1) < HS

        def group(g8, s):
            base = pl.multiple_of(g8 * 8, 8)
            r8, w8, k8, z8, b8, v8 = (ref[pl.ds(base, 8), :] for ref in (r_ref, w_ref, k_ref, z_ref, b_ref, v_ref))
            vcols = _cols8(v8, e_ref[...])
            urows = []
            for j in range(8):
                vc = _pair_rows(vcols[:, 128 * j:128 * (j + 1)], first)
                s, u = _rec_step(s, w8[j:j + 1], z8[j:j + 1], b8[j:j + 1], k8[j:j + 1], vc, first)
                states[base + j] = s
                urows.append(_colsum(u * dm_ref[...]))
                prod[HS * j:HS * (j + 1), :] = s * r8[j:j + 1]
            y_ref[pl.ds(base, 8), :] = _rows8(prod[...], bones_ref[...], dm_ref[...])
            u_ref[pl.ds(base, 8), :] = jnp.concatenate(urows, axis=0)
            return s

        s_ref[...] = lax.fori_loop(0, REC_TILE // 8, group, s_ref[...])
        pl.when(i == nt - 1)(finish)

    row = pl.BlockSpec((REC_TILE, DR), lambda i: (i, 0))
    y, ck, states, u, *got = pl.pallas_call(
        kern, grid=(nt,), in_specs=[row] * 6 + [_const_spec(c) for c in consts] + [ANY] * ns,
        out_specs=[row, pl.BlockSpec((1, HS, DR), lambda i: (i, 0, 0)), pl.BlockSpec((REC_TILE, HS, DR), lambda i: (i, 0, 0)), row]
        + [ANY] * ns,
        out_shape=[jax.ShapeDtypeStruct((t_len, DR), F32), jax.ShapeDtypeStruct((nt, HS, DR), F32),
                   jax.ShapeDtypeStruct((t_len, HS, DR), F32), jax.ShapeDtypeStruct((t_len, DR), F32)] + _gathered_shapes(shards),
        scratch_shapes=[pltpu.VMEM((HS, DR), F32), pltpu.VMEM((8 * HS, DR), F32)] + _gather_sems(ns), name="rec_fwd",
        compiler_params=_params(("arbitrary",)))(r, w, k, z, b, v, *consts, *shards)
    return y, (ck, states, u), _fill_own(got, shards)


def _rec_bwd(r, w, k, z, b, v, dy, saved, sums):
    t_len = r.shape[0]
    nt = t_len // REC_TILE
    ns = len(sums)
    consts = _rec_consts()

    def kern(r_ref, w_ref, k_ref, z_ref, b_ref, v_ref, dy_ref, u_ref, ck_ref, states, e_ref, bones_ref, dm_ref, *rest):
        dr_ref, dw_ref, dk_ref, dz_ref, db_ref, dv_ref = rest[ns:ns + 6]
        ds_ref, prod = rest[2 * ns + 6:2 * ns + 8]
        start, finish = _owners_plan(rest[:ns], rest[ns + 6:2 * ns + 6], *rest[2 * ns + 8:])
        i = pl.program_id(0)

        @pl.when(i == 0)
        def _():
            ds_ref[...] = jnp.zeros(ds_ref.shape, F32)
            start()

        first = lax.broadcasted_iota(jnp.int32, (1, 128), 1) < HS

        def bgroup(gg, ds):
            base = pl.multiple_of((REC_TILE // 8 - 1 - gg) * 8, 8)
            r8, w8, k8, z8, b8, v8, dy8, u8 = (ref[pl.ds(base, 8), :]
                                               for ref in (r_ref, w_ref, k_ref, z_ref, b_ref, v_ref, dy_ref, u_ref))
            vcols = _cols8(v8, e_ref[...])
            dycols = _cols8(dy8, e_ref[...])
            ucols = _cols8(u8, e_ref[...])
            before = jnp.where(base == 0, ck_ref[0], states[jnp.maximum(base - 1, 0)])
            rows = {n: [None] * 8 for n in ("dr", "dw", "dk", "dz", "db")}
            for j in range(7, -1, -1):
                t = base + j
                rr, wr, kr, zr, br = (x[j:j + 1] for x in (r8, w8, k8, z8, b8))
                s_prev, s_t = (states[t - 1] if j else before), states[t]
                dyc = _pair_rows(dycols[:, 128 * j:128 * (j + 1)], first)
                vc = _pair_rows(vcols[:, 128 * j:128 * (j + 1)], first)
                ds = ds + dyc * rr
                rows["dr"][j] = _colsum(s_t * dyc)
                rows["dw"][j] = _colsum(ds * s_prev)
                du = _pair_bcast(_head_sums(ds * br, first), first)
                rows["db"][j] = _colsum(ds * _pair_rows(ucols[:, 128 * j:128 * (j + 1)], first))
                rows["dk"][j] = _colsum(ds * vc)
                prod[HS * j:HS * (j + 1), :] = ds * kr
                rows["dz"][j] = _colsum(s_prev * du)
                ds = ds * wr + du * zr
            for n, ref in (("dr", dr_ref), ("dw", dw_ref), ("dk", dk_ref), ("dz", dz_ref), ("db", db_ref)):
                ref[pl.ds(base, 8), :] = jnp.concatenate(rows[n], axis=0)
            dv_ref[pl.ds(base, 8), :] = _rows8(prod[...], bones_ref[...], dm_ref[...])
            return ds

        ds_ref[...] = lax.fori_loop(0, REC_TILE // 8, bgroup, ds_ref[...])
        pl.when(i == nt - 1)(finish)

    ck, states, u = saved
    row = pl.BlockSpec((REC_TILE, DR), lambda i: (nt - 1 - i, 0))
    outs = pl.pallas_call(
        kern, grid=(nt,),
        in_specs=[row] * 8 + [pl.BlockSpec((1, HS, DR), lambda i: (nt - 1 - i, 0, 0)),
                              pl.BlockSpec((REC_TILE, HS, DR), lambda i: (nt - 1 - i, 0, 0))]
        + [_const_spec(c) for c in consts] + [ANY] * ns,
        out_specs=[row] * 6 + [ANY] * ns, out_shape=[jax.ShapeDtypeStruct((t_len, DR), F32)] * 6 + _owner_shapes(sums),
        scratch_shapes=[pltpu.VMEM((HS, DR), F32), pltpu.VMEM((8 * HS, DR), F32)] + _owner_sems(ns), name="rec_bwd",
        compiler_params=_params(("arbitrary",)))(r, w, k, z, b, v, dy, u, ck, states, *consts, *sums)
    return outs[:6], outs[6:]


def _prev_rows(a, tm, n):
    return (a, (n, a.shape[1]), lambda i: (jnp.maximum(i * (tm // n) - 1, 0), 0))


def _next_rows(a, tm, n):
    last = a.shape[0] // n - 1
    return (a, (n, a.shape[1]), lambda i: (jnp.minimum((i + 1) * (tm // n), last), 0))


def _shifted(x, prev8, i):
    rowid = lax.broadcasted_iota(jnp.int32, x.shape, 0)
    before = jnp.where(i == 0, 0.0, prev8[7:8, :])
    return jnp.where(rowid == 0, before, pltpu.roll(x, 1, 0))


def _shift_fwd(p, mu, tm):
    def fn(i, pv, prev8, muv):
        x = pv[:, :D_SHIFT]
        return (x + (_shifted(x, prev8[:, :D_SHIFT], i) - x) * muv,), ()
    return _rows_call("shift", fn, [p, _prev_rows(p, tm, 8)], [mu], [(D_SHIFT, F32)], [], tm)[0]


def _shift_bwd(dps, p, dpc, mu, tm):
    n_tiles = p.shape[0] // tm

    def fn(i, dv, next8, pv, prev8, dpcv, muv):
        x = pv[:, :D_SHIFT]
        xs = _shifted(x, prev8[:, :D_SHIFT], i)
        rowid = lax.broadcasted_iota(jnp.int32, dv.shape, 0)
        after = jnp.where(i == n_tiles - 1, 0.0, next8[0:1, :])
        dnext = jnp.where(rowid == tm - 1, after, pltpu.roll(dv, tm - 1, 0))
        dp_s = dv * (1.0 - muv) + dnext * muv
        return (jnp.concatenate([dp_s.astype(BF16), dpcv], axis=1),), (_colsum(dv * (xs - x)),)
    return _rows_call("shift_b", fn, [dps, _next_rows(dps, tm, 8), p, _prev_rows(p, tm, 8), dpc], [mu],
                      [(D_IN, BF16)], [(1, D_SHIFT)], tm)


def _glu(pc):
    return pc[:, :DR] * jax.nn.sigmoid(pc[:, DR:])


def _shift_copies(ext, shifted, tm):
    for s in range(1, 8):
        shifted[s - 1] = ext[s:s + tm + 24, :]


def _window(ext, shifted, off, tm):
    if off % 8 == 0:
        return ext[off:off + tm, :]
    return shifted[off % 8 - 1, off // 8 * 8:off // 8 * 8 + tm, :]


def _conv_fwd(p, dw32, cb, lw, lb, tm):
    t_len = p.shape[0]

    def kern(p_ref, ph_ref, dw_ref, cb_ref, lw_ref, lb_ref, glu_ref, c_ref, ob_ref, ext, shifted):
        i = pl.program_id(0)
        glu = _glu(p_ref[:, D_SHIFT:])
        ext[0:32, :] = jnp.where(i == 0, 0.0, _glu(ph_ref[:, D_SHIFT:]))
        ext[32:, :] = glu
        _shift_copies(ext, shifted, tm)
        acc = jnp.zeros((tm, DR), F32)
        for j in range(CW):
            acc = acc + _window(ext, shifted, 2 + j, tm) * dw_ref[j:j + 1, :]
        c = acc + cb_ref[...]
        glu_ref[...] = glu
        c_ref[...] = c
        ob_ref[...] = _ln_silu(c, lw_ref[...], lb_ref[...]).astype(BF16)

    tile = lambda w: pl.BlockSpec((tm, w), lambda i: (i, 0))
    const = lambda a: pl.BlockSpec(a.shape, lambda i: (0, 0))
    halo = pl.BlockSpec((32, D_IN), lambda i: (jnp.maximum(i * (tm // 32) - 1, 0), 0))
    return pl.pallas_call(
        kern, grid=(t_len // tm,), in_specs=[tile(D_IN), halo, const(dw32), const(cb), const(lw), const(lb)],
        out_specs=[tile(DR)] * 3,
        out_shape=[jax.ShapeDtypeStruct((t_len, DR), F32)] * 2 + [jax.ShapeDtypeStruct((t_len, DR), BF16)],
        scratch_shapes=[pltpu.VMEM((tm + 32, DR), F32), pltpu.VMEM((7, tm + 24, DR), F32)], name="conv_fwd",
        compiler_params=_params(("arbitrary",)))(p, p, dw32, cb, lw, lb)


def _conv_bwd(dc, glu, p, dw32, tm):
    t_len = p.shape[0]
    n_tiles = t_len // tm

    def kern(dc_ref, dcn_ref, glu_ref, gluh_ref, p_ref, dw_ref, dpc_ref, ddw_ref, ext_d, ext_g, shifted_d, shifted_g):
        i = pl.program_id(0)

        @pl.when(i == 0)
        def _():
            ddw_ref[...] = jnp.zeros(ddw_ref.shape, F32)

        dcv = dc_ref[...]
        ext_d[0:tm, :] = dcv
        ext_d[tm:, :] = jnp.where(i == n_tiles - 1, 0.0, dcn_ref[...])
        ext_g[0:32, :] = jnp.where(i == 0, 0.0, gluh_ref[...])
        ext_g[32:, :] = glu_ref[...]
        _shift_copies(ext_d, shifted_d, tm)
        _shift_copies(ext_g, shifted_g, tm)
        dglu = jnp.zeros((tm, DR), F32)
        for j in range(CW):
            dglu = dglu + _window(ext_d, shifted_d, 30 - j, tm) * dw_ref[j:j + 1, :]
            ddw_ref[j:j + 1, :] += _colsum(dcv * _window(ext_g, shifted_g, 2 + j, tm))
        pc = p_ref[:, D_SHIFT:]
        sg = jax.nn.sigmoid(pc[:, DR:])
        dpc_ref[...] = jnp.concatenate([dglu * sg, dglu * pc[:, :DR] * sg * (1.0 - sg)], axis=1).astype(BF16)

    tile = lambda w: pl.BlockSpec((tm, w), lambda i: (i, 0))
    nxt = pl.BlockSpec((32, DR), lambda i: (jnp.minimum((i + 1) * (tm // 32), t_len // 32 - 1), 0))
    prv = pl.BlockSpec((32, DR), lambda i: (jnp.maximum(i * (tm // 32) - 1, 0), 0))
    return pl.pallas_call(
        kern, grid=(n_tiles,), in_specs=[tile(DR), nxt, tile(DR), prv, tile(D_IN), pl.BlockSpec((32, DR), lambda i: (0, 0))],
        out_specs=[tile(D), pl.BlockSpec((32, DR), lambda i: (0, 0))],
        out_shape=[jax.ShapeDtypeStruct((t_len, D), BF16), jax.ShapeDtypeStruct((32, DR), F32)],
        scratch_shapes=[pltpu.VMEM((tm + 32, DR), F32)] * 2 + [pltpu.VMEM((7, tm + 24, DR), F32)] * 2, name="conv_bwd",
        compiler_params=_params(("arbitrary",)))(dc, dc, glu, glu, p, dw32)


def _place():
    x, y, c = lax.axis_index("x"), lax.axis_index("y"), lax.axis_index("c")
    chips = [(1 - x, y), (x, 1 - y), (1 - x, 1 - y)]
    return x, y, c, chips


def _gather_weights(shards):
    n = len(shards)

    def body(*refs):
        start, middle, finish = _gather_plan(refs[:n], refs[n:2 * n], *refs[2 * n:])
        start()
        middle()
        finish()

    got = pl.pallas_call(body, in_specs=[ANY] * n, out_specs=[ANY] * n, out_shape=_gathered_shapes(shards),
                         scratch_shapes=_gather_sems(n), name="gather_weights")(*shards)
    return _fill_own(got, shards)


def _gathered_shapes(shards):
    return [jax.ShapeDtypeStruct((4,) + s.shape, s.dtype) for s in shards]


def _gather_sems(n):
    return [pltpu.SemaphoreType.DMA((6 * n,)), pltpu.SemaphoreType.DMA((6 * n,))]


def _fill_own(got, shards):
    me = 2 * lax.axis_index("x") + lax.axis_index("y")
    return [lax.dynamic_update_slice(g, s[None], (me, 0, 0, 0)) for g, s in zip(got, shards)]


def _gather_plan(src, dst, send, recv):
    n = len(src)
    x, y, c, chips = _place()
    me, sib = 2 * x + y, (x, y, 1 - c)

    def rcopy(k, sem, s_ref, d_ref, to):
        return pltpu.make_async_remote_copy(src_ref=s_ref, dst_ref=d_ref, send_sem=send.at[6 * k + sem],
                                            recv_sem=recv.at[6 * k + sem], device_id=to, device_id_type=MESH)

    def landed(k, m, half):
        return dst[k].at[2 * chips[m][0] + chips[m][1], half]

    first = [rcopy(k, m, src[k].at[c], dst[k].at[me, c], (*chips[m], c)) for k in range(n) for m in range(3)]
    passed = [rcopy(k, 3 + m, landed(k, m, c), landed(k, m, c), sib) for k in range(n) for m in range(3)]

    def start():
        for cp in first:
            cp.start()

    def middle():
        for k in range(n):
            for m in range(3):
                rcopy(k, m, landed(k, m, c), landed(k, m, c), sib).wait_recv()
                passed[3 * k + m].start()

    def finish():
        for k in range(n):
            for m in range(3):
                rcopy(k, 3 + m, landed(k, m, 1 - c), landed(k, m, 1 - c), sib).wait_recv()
        for cp in first + passed:
            cp.wait_send()

    return start, middle, finish


def _swap_halves(name, give):
    n = len(give)

    def body(*refs):
        src, got = refs[:n], refs[n:2 * n]
        send, recv = refs[2 * n:]
        x, y, c, _ = _place()
        copies = []
        for k in range(n):
            for j in range(4):
                copies.append(pltpu.make_async_remote_copy(
                    src_ref=src[k].at[j], dst_ref=got[k].at[j], send_sem=send.at[4 * k + j], recv_sem=recv.at[4 * k + j],
                    device_id=(x, y, 1 - c), device_id_type=MESH))
                copies[-1].start()
        for cp in copies:
            cp.wait()

    out_shape = [jax.ShapeDtypeStruct(g.shape, g.dtype) for g in give]
    return pl.pallas_call(body, in_specs=[ANY] * n, out_specs=[ANY] * n, out_shape=out_shape,
                          scratch_shapes=[pltpu.SemaphoreType.DMA((4 * n,)), pltpu.SemaphoreType.DMA((4 * n,))],
                          name=name)(*give)


def _owner_shapes(sums):
    return [jax.ShapeDtypeStruct((3,) + s.shape[1:], s.dtype) for s in sums]


def _owner_sems(n):
    return [pltpu.SemaphoreType.DMA((3 * n,)), pltpu.SemaphoreType.DMA((3 * n,))]


def _owners_plan(src, dst, send, recv):
    x, y, c, chips = _place()
    copies = [pltpu.make_async_remote_copy(
        src_ref=src[k].at[2 * chip[0] + chip[1]], dst_ref=dst[k].at[m], send_sem=send.at[3 * k + m],
        recv_sem=recv.at[3 * k + m], device_id=(*chip, c), device_id_type=MESH)
        for k in range(len(src)) for m, chip in enumerate(chips)]

    def start():
        for cp in copies:
            cp.start()

    def finish():
        for cp in copies:
            cp.wait()

    return start, finish


def _join_halves(halves):
    n = len(halves)

    def body(*refs):
        src, dst = refs[:n], refs[n:2 * n]
        send, recv = refs[2 * n:]
        x, y, c, _ = _place()
        copies = []
        for k in range(n):
            copies.append(pltpu.make_async_remote_copy(src_ref=src[k], dst_ref=dst[k], send_sem=send.at[k],
                                                       recv_sem=recv.at[k], device_id=(x, y, 1 - c), device_id_type=MESH))
            copies[-1].start()
        for cp in copies:
            cp.wait()

    out_shape = [jax.ShapeDtypeStruct(h.shape, h.dtype) for h in halves]
    return pl.pallas_call(body, in_specs=[ANY] * n, out_specs=[ANY] * n, out_shape=out_shape,
                          scratch_shapes=[pltpu.SemaphoreType.DMA((n,)), pltpu.SemaphoreType.DMA((n,))],
                          name="join_halves")(*halves)


def _allreduce_small(v):
    rows, n = v.shape

    def body(v_ref, o_ref, buf, send, recv):
        x, y, c, _ = _place()
        me = 4 * x + 2 * y + c
        buf[me] = v_ref[...]
        copies = []
        for d in range(1, 8):
            peer = (x ^ (d >> 2), y ^ ((d >> 1) & 1), c ^ (d & 1))
            cp = pltpu.make_async_remote_copy(src_ref=v_ref, dst_ref=buf.at[me], send_sem=send.at[d], recv_sem=recv.at[d],
                                              device_id=peer, device_id_type=MESH)
            cp.start()
            copies.append(cp)
        for d in range(1, 8):
            peer = 4 * (x ^ (d >> 2)) + 2 * (y ^ ((d >> 1) & 1)) + (c ^ (d & 1))
            pltpu.make_async_remote_copy(src_ref=v_ref, dst_ref=buf.at[peer], send_sem=send.at[d], recv_sem=recv.at[d],
                                         device_id=(x, y, c), device_id_type=MESH).wait_recv()
        for cp in copies:
            cp.wait_send()
        acc = buf[0]
        for d in range(1, 8):
            acc = acc + buf[d]
        o_ref[...] = acc

    vm = pl.BlockSpec(memory_space=pltpu.VMEM)
    return pl.pallas_call(body, in_specs=[vm], out_specs=vm, out_shape=jax.ShapeDtypeStruct((rows, n), F32),
                          scratch_shapes=[pltpu.VMEM((8, rows, n), F32), pltpu.SemaphoreType.DMA((8,)),
                                          pltpu.SemaphoreType.DMA((8,))], name="allreduce_small")(v)


def _add_call(name, parts, tm, out_dtype=F32):
    def fn(i, *vals):
        acc = vals[0].astype(F32)
        for v in vals[1:]:
            acc = acc + v.astype(F32)
        return (acc,), ()
    return _rows_call(name, fn, list(parts), [], [(parts[0].shape[1], out_dtype)], [], tm)[0]


def _adamw(name, w, g, m, v):
    c1 = 1.0 / (1.0 - ADAM_B1 ** ADAM_STEP)
    c2 = 1.0 / (1.0 - ADAM_B2 ** ADAM_STEP)

    def fn(i, wv, gv, mv, vv):
        m2 = ADAM_B1 * mv + (1.0 - ADAM_B1) * gv
        v2 = ADAM_B2 * vv + (1.0 - ADAM_B2) * jnp.square(gv)
        delta = -ADAM_LR * ((m2 * c1) / (jnp.sqrt(v2 * c2) + ADAM_EPS) + ADAM_WD * wv)
        return (delta, m2, v2), ()
    cols = w.shape[1]
    tm = _tile(w.shape[0], (256, 176, 128, 64, 8))
    return _rows_call(name, fn, [w, g, m, v], [], [(cols, F32)] * 3, [], tm)


def _canon(name, a, transposed):
    return a[0].T if transposed else a[0]


def _pack_sharded(w_up, a_up, g_up, conv_dw):
    parts = [w_up[0].T, a_up[0].T, g_up[0].T, conv_dw[0].T]
    used = sum(p.shape[1] for p in parts)
    return jnp.concatenate(parts + [jnp.zeros((parts[0].shape[0], PACK_W - used), F32)], axis=1)


def _unpack_sharded(a):
    return [a[:, 0:64].T[None], a[:, 64:128].T[None], a[:, 128:256].T[None], a[:, 256:256 + CW].T[None]]


def _pack_small(vals):
    flat = jnp.concatenate([v.reshape(-1) for v in vals] + [jnp.zeros((SMALL_PAD - SMALL_N,), F32)])
    return flat.reshape(8, SMALL_PAD // 8)


def _unpack_small(a, like):
    flat, out, off = a.reshape(-1), [], 0
    for (_, n), ref in zip(SMALL, like):
        out.append(flat[off:off + n].reshape(ref.shape))
        off += n
    return out


def kernel(x, ffn1_norm_pre, ffn1_norm_post, ffn1_w_gu, ffn1_w_down, mix_norm_pre, mix_norm_post, w_in, shift_mu, w_up, w0, a_up, a0, g_up, k_k, k_a, r_k, gn_w, gn_b, conv_dw, conv_b, conv_ln_w, conv_ln_b, w_out, ffn2_norm_pre, ffn2_norm_post, ffn2_w_gu, ffn2_w_down, loss_target, m_ffn1_norm_pre, m_ffn1_norm_post, m_ffn1_w_gu, m_ffn1_w_down, m_mix_norm_pre, m_mix_norm_post, m_w_in, m_shift_mu, m_w_up, m_w0, m_a_up, m_a0, m_g_up, m_k_k, m_k_a, m_r_k, m_gn_w, m_gn_b, m_conv_dw, m_conv_b, m_conv_ln_w, m_conv_ln_b, m_w_out, m_ffn2_norm_pre, m_ffn2_norm_post, m_ffn2_w_gu, m_ffn2_w_down, v_ffn1_norm_pre, v_ffn1_norm_post, v_ffn1_w_gu, v_ffn1_w_down, v_mix_norm_pre, v_mix_norm_post, v_w_in, v_shift_mu, v_w_up, v_w0, v_a_up, v_a0, v_g_up, v_k_k, v_k_a, v_r_k, v_gn_w, v_gn_b, v_conv_dw, v_conv_b, v_conv_ln_w, v_conv_ln_b, v_w_out, v_ffn2_norm_pre, v_ffn2_norm_post, v_ffn2_w_gu, v_ffn2_w_down):
    w = dict(ffn1_norm_pre=ffn1_norm_pre, ffn1_norm_post=ffn1_norm_post, ffn1_w_gu=ffn1_w_gu, ffn1_w_down=ffn1_w_down, mix_norm_pre=mix_norm_pre, mix_norm_post=mix_norm_post, w_in=w_in, shift_mu=shift_mu, w_up=w_up, w0=w0, a_up=a_up, a0=a0, g_up=g_up, k_k=k_k, k_a=k_a, r_k=r_k, gn_w=gn_w, gn_b=gn_b, conv_dw=conv_dw, conv_b=conv_b, conv_ln_w=conv_ln_w, conv_ln_b=conv_ln_b, w_out=w_out, ffn2_norm_pre=ffn2_norm_pre, ffn2_norm_post=ffn2_norm_post, ffn2_w_gu=ffn2_w_gu, ffn2_w_down=ffn2_w_down)
    mom = dict(ffn1_norm_pre=m_ffn1_norm_pre, ffn1_norm_post=m_ffn1_norm_post, ffn1_w_gu=m_ffn1_w_gu, ffn1_w_down=m_ffn1_w_down, mix_norm_pre=m_mix_norm_pre, mix_norm_post=m_mix_norm_post, w_in=m_w_in, shift_mu=m_shift_mu, w_up=m_w_up, w0=m_w0, a_up=m_a_up, a0=m_a0, g_up=m_g_up, k_k=m_k_k, k_a=m_k_a, r_k=m_r_k, gn_w=m_gn_w, gn_b=m_gn_b, conv_dw=m_conv_dw, conv_b=m_conv_b, conv_ln_w=m_conv_ln_w, conv_ln_b=m_conv_ln_b, w_out=m_w_out, ffn2_norm_pre=m_ffn2_norm_pre, ffn2_norm_post=m_ffn2_norm_post, ffn2_w_gu=m_ffn2_w_gu, ffn2_w_down=m_ffn2_w_down)
    var = dict(ffn1_norm_pre=v_ffn1_norm_pre, ffn1_norm_post=v_ffn1_norm_post, ffn1_w_gu=v_ffn1_w_gu, ffn1_w_down=v_ffn1_w_down, mix_norm_pre=v_mix_norm_pre, mix_norm_post=v_mix_norm_post, w_in=v_w_in, shift_mu=v_shift_mu, w_up=v_w_up, w0=v_w0, a_up=v_a_up, a0=v_a0, g_up=v_g_up, k_k=v_k_k, k_a=v_k_a, r_k=v_r_k, gn_w=v_gn_w, gn_b=v_gn_b, conv_dw=v_conv_dw, conv_b=v_conv_b, conv_ln_w=v_conv_ln_w, conv_ln_b=v_conv_ln_b, w_out=v_w_out, ffn2_norm_pre=v_ffn2_norm_pre, ffn2_norm_post=v_ffn2_norm_post, ffn2_w_gu=v_ffn2_w_gu, ffn2_w_down=v_ffn2_w_down)
    names = list(w)

    x0 = x[0]
    tgt = loss_target[0]
    t_len = x0.shape[0]
    tm = _tile(t_len, (512, 256, 128))

    half = {n: rows // 8 for n, rows, _ in BIG}
    pack_local = _pack_sharded(w["w_up"], w["a_up"], w["g_up"], w["conv_dw"])
    shard = {n: _canon(n, w[n], tr).astype(BF16).reshape(2, half[n], D) for n, _, tr in BIG}
    rows_of = {n: rows for n, rows, _ in BIG}
    full = {"ffn1_w_gu": _gather_weights([shard["ffn1_w_gu"]])[0].reshape(2 * DFF, D)}

    h1, gu1, act1, f1, gathered = _ffn_fwd("ffn1", x0, w["ffn1_norm_pre"], full["ffn1_w_gu"], None, tm,
                                           gather=[shard[n] for n in SECOND] + [pack_local.reshape(2, 64, PACK_W)])
    full.update({n: g.reshape(rows_of[n], D) for n, g in zip(SECOND, gathered)})
    pack = gathered[-1].reshape(DR, PACK_W)
    p01, p2 = pack[:, 0:128], pack[:, 128:256]
    dw32 = jnp.concatenate([pack[:, 256:256 + CW].T, jnp.zeros((1, DR), F32)], axis=0)
    def resid_norm(scale):
        def fn(i, xv, fv, g_post, g_pre):
            xn = xv + scale * _rms(fv, g_post)
            return (xn, _rms(xn, g_pre)), ()
        return fn
    x1, hm = _rows_call("ffn1_resid", resid_norm(0.5), [x0, f1], [w["ffn1_norm_post"], w["mix_norm_pre"]],
                        [(D, F32), (D, BF16)], [], tm)
    p = _matmul("mix_in", hm, full["w_in"], "nt")
    ps = _shift_fwd(p, w["shift_mu"], tm)
    rkc = w["r_k"].reshape(1, DR)
    prep_consts = [w["w0"], w["a0"], w["k_k"], w["k_a"], p01, p2]
    r_, dec, k2, v_, z_, b_, g_ = _rows_call(
        "prep", lambda i, psv, *cs: (_prep(psv, *cs), ()), [ps], prep_consts, [(DR, F32)] * 7, [], min(tm, 256))
    y, ck, late = _rec_fwd(r_, dec, k2, z_, b_, v_, [shard[n] for n in LATE])
    full.update({n: g.reshape(rows_of[n], D) for n, g in zip(LATE, late)})
    glu, cpre, ob = _conv_fwd(p, dw32, w["conv_b"], w["conv_ln_w"], w["conv_ln_b"], tm)
    post_consts = [w["gn_w"], w["gn_b"], rkc]
    (o,) = _rows_call(
        "post", lambda i, yv, rv, kv, vv, gv, obv, *cs: ((jnp.concatenate([_post(yv, rv, kv, vv, gv, *cs).astype(BF16), obv], axis=1),), ()),
        [y, r_, k2, v_, g_, ob], post_consts, [(D, BF16)], [], min(tm, 256))
    mo = _matmul("mix_out", o, full["w_out"], "nn")
    x2, h2 = _rows_call("mix_resid", resid_norm(1.0), [x1, mo], [w["mix_norm_post"], w["ffn2_norm_pre"]],
                        [(D, F32), (D, BF16)], [], tm)
    h2, gu2, act2, f2, _ = _ffn_fwd("ffn2", x2, w["ffn2_norm_pre"], full["ffn2_w_gu"], full["ffn2_w_down"], tm, h=h2)

    def loss_fn(i, xv, fv, tv, g):
        err = xv + 0.5 * _rms(fv, g) - tv
        part = 0.5 * jnp.sum(jnp.mean(err * err, axis=-1, keepdims=True), axis=0, keepdims=True)
        return (err * (1.0 / D),), (jnp.broadcast_to(part, (8, 128)),)
    dx3, loss_part = _rows_call("loss", loss_fn, [x2, f2, tgt], [w["ffn2_norm_post"]], [(D, F32)], [(8, 128)], tm)
    loss = lax.psum(loss_part[0, 0], ("x", "y", "c"))

    g_small = {}
    dx2, g_small["ffn2_norm_pre"], g_small["ffn2_norm_post"], dgu2_t, dwd2, _ = _ffn_bwd(
        "ffn2", x2, w["ffn2_norm_pre"], w["ffn2_norm_post"], 0.5, h2, gu2, act2, f2, dx3, full["ffn2_w_gu"], full["ffn2_w_down"], tm)

    def mix_resid_b(i, fv, dv, g):
        _, vjp = jax.vjp(_rms, fv, g)
        df, dg = vjp(dv)
        return (df,), (dg,)
    dmo, g_small["mix_norm_post"] = _rows_call("mix_resid_b", mix_resid_b, [mo, dx2], [w["mix_norm_post"]],
                                               [(D, BF16)], [(1, D)], tm)
    do = _matmul("mix_do", dmo, full["w_out"], "nt")
    dw_out = _matmul("mix_dwout", o, dmo, "tn")

    def post_b(i, yv, rv, kv, vv, gv, dov, *cs):
        _, vjp = jax.vjp(_post, yv, rv, kv, vv, gv, *cs)
        dy, dr, dk, dv, dg, dgw, dgb, drk = vjp(dov[:, :DR])
        return (dy, dr, dk, dv, dg), (_colsum(dgw), _colsum(dgb), _colsum(drk))
    dy, dr1, dk1, dv1, dg, g_small["gn_w"], g_small["gn_b"], g_small["r_k"] = _rows_call(
        "post_b", post_b, [y, r_, k2, v_, g_, do], post_consts, [(DR, F32)] * 5, [(1, DR)] * 3, min(tm, 256))
    my_c = lax.axis_index("c")
    my_chip = 2 * lax.axis_index("x") + lax.axis_index("y")
    g_big = dict(w_out=dw_out, ffn2_w_gu=dgu2_t, ffn2_w_down=dwd2)

    def halves_of(group, which):
        return jnp.concatenate([lax.dynamic_index_in_dim(g_big[n].reshape(4, 2, half[n], D), which, 1, keepdims=False)
                                for n in group], axis=1)

    def pair_sums(tag, group, extra_mine=(), extra_give=()):
        mine = halves_of(group, my_c)
        got, *got_x = _swap_halves("swap_halves_" + tag, [halves_of(group, 1 - my_c).astype(BF16)] + list(extra_give))
        rows = mine.shape[1]
        tile = _tile(rows, (352, 592, 16))
        travels = _add_call("add_pair_" + tag, [mine.reshape(4 * rows, D), got.reshape(4 * rows, D)], tile, BF16)
        return mine, got, travels.reshape(4, rows, D), got_x

    def owner_sum(tag, mine, got, parts):
        own = [lax.dynamic_index_in_dim(a, my_chip, 0, keepdims=False) for a in (mine, got)]
        return _add_call("add_chips_" + tag, own + [parts[m] for m in range(3)], _tile(mine.shape[1], (352, 592, 16)))

    mine_l, got_l, sum_l, _ = pair_sums("late", LATE)
    (dr2, ddec, dk2, dz, db, dv2), (parts_l,) = _rec_bwd(r_, dec, k2, z_, b_, v_, dy, ck, [sum_l])

    def prep_b(i, psv, a1, a2, c1, c2, e1, e2, dwv, dzv, dbv, dgv, *cs):
        _, vjp = jax.vjp(_prep, psv, *cs)
        dps, dw0, da0, dkk, dka, dp01, dp2 = vjp((a1 + a2, dwv, c1 + c2, e1 + e2, dzv, dbv, dgv))
        return (dps,), (_colsum(dw0), _colsum(da0), _colsum(dkk), _colsum(dka), dp01, dp2)
    dps, g_small["w0"], g_small["a0"], g_small["k_k"], g_small["k_a"], dp01, dp2 = _rows_call(
        "prep_b", prep_b, [ps, dr1, dr2, dk1, dk2, dv1, dv2, ddec, dz, db, dg], prep_consts, [(D_SHIFT, F32)],
        [(1, DR)] * 4 + [(DR, 128)] * 2, min(tm, 256))

    def convln_b(i, cv, dov, lw, lb):
        _, vjp = jax.vjp(_ln_silu, cv, lw, lb)
        dc, dlw, dlb = vjp(dov[:, DR:])
        return (dc,), (_colsum(dc), _colsum(dlw), _colsum(dlb))
    dc, g_small["conv_b"], g_small["conv_ln_w"], g_small["conv_ln_b"] = _rows_call(
        "convln_b", convln_b, [cpre, do], [w["conv_ln_w"], w["conv_ln_b"]], [(DR, F32)], [(1, DR)] * 3, tm)
    dpc, ddw32 = _conv_bwd(dc, glu, p, dw32, tm)
    dp, g_small["shift_mu"] = _shift_bwd(dps, p, dpc, w["shift_mu"], tm)
    dhm = _matmul("mix_dh", dp, full["w_in"], "nn")
    dw_in_t = _matmul("mix_dwin", dp, hm, "tn")

    def norm_b(i, xv, dhv, dv, g):
        _, vjp = jax.vjp(_rms, xv, g)
        dx, dg_ = vjp(dhv)
        return (dx + dv,), (dg_,)
    dx1, g_small["mix_norm_pre"] = _rows_call("mix_norm_b", norm_b, [x1, dhm, dx2], [w["mix_norm_pre"]],
                                              [(D, F32)], [(1, D)], tm)
    kept = {}

    def second_sums(dwd1):
        g_big.update(ffn1_w_down=dwd1, w_in=dw_in_t)
        pack_grads = jnp.concatenate([dp01, dp2, ddw32.T, jnp.zeros((DR, PACK_W - 288), F32)], axis=1).reshape(4, 2, 64, PACK_W)
        mine_p, give_p = (lax.dynamic_index_in_dim(pack_grads, which, 1, keepdims=False) for which in (my_c, 1 - my_c))
        mine, got, travels, (got_p,) = pair_sums("second", SECOND, extra_give=[give_p])
        sum_p = _add_call("add_pair_pack", [mine_p.reshape(256, PACK_W), got_p.reshape(256, PACK_W)], 256).reshape(4, 64, PACK_W)
        kept.update(second=(mine, got), sum_p=sum_p)
        return [travels, sum_p]

    def first_sums(dgu1_t):
        g_big.update(ffn1_w_gu=dgu1_t)
        mine, got, travels, _ = pair_sums("first", ("ffn1_w_gu",))
        kept.update(first=(mine, got))
        return [travels]

    grad_x, g_small["ffn1_norm_pre"], g_small["ffn1_norm_post"], _, _, ((parts_s, parts_p), (parts_f,)) = _ffn_bwd(
        "ffn1", x0, w["ffn1_norm_pre"], w["ffn1_norm_post"], 0.5, h1, gu1, act1, f1, dx1, full["ffn1_w_gu"], full["ffn1_w_down"], tm,
        after_down=second_sums, after_gu=first_sums)

    sum_p = kept["sum_p"]
    fin_s = owner_sum("second", *kept["second"], parts_s)
    fin_f = owner_sum("first", *kept["first"], parts_f)
    fin_l = owner_sum("late", mine_l, got_l, parts_l)
    fin_p = _add_call("add_chips_pack", [lax.dynamic_index_in_dim(sum_p, my_chip, 0, keepdims=False)] + [parts_p[m] for m in range(3)], 64)
    fins = [fin_f, fin_s, fin_l, fin_p]
    red_f, red_s, red_l, red_p = [jnp.where(my_c == 0, jnp.stack([f, s]), jnp.stack([s, f])) for f, s in zip(fins, _join_halves(fins))]
    small_sum = _allreduce_small(_pack_small([g_small[n] for n, _ in SMALL]))

    grads, delta, new_m, new_v = {}, {}, {}, {}
    reduced = {}
    for group, red in ((("ffn1_w_gu",), red_f), (SECOND, red_s), (LATE, red_l)):
        off = 0
        for n in group:
            reduced[n] = red[:, off:off + half[n], :].reshape(rows_of[n] // 4, D)
            off += half[n]
    for n, rows, tr in BIG:
        g = reduced[n]
        g = (g.T if tr else g)[None]
        grads[n] = g
        d_, m_, v2_ = _adamw("adamw_" + n, w[n][0], g[0], mom[n][0], var[n][0])
        delta[n], new_m[n], new_v[n] = d_[None], m_[None], v2_[None]
    sh = ("w_up", "a_up", "g_up", "conv_dw")
    g_pack = red_p.reshape(128, PACK_W)
    d_, m_, v2_ = _adamw("adamw_pack", pack_local, g_pack, _pack_sharded(*[mom[n] for n in sh]), _pack_sharded(*[var[n] for n in sh]))
    for dst, src in ((grads, g_pack), (delta, d_), (new_m, m_), (new_v, v2_)):
        for n, a in zip(sh, _unpack_sharded(src)):
            dst[n] = a
    sm = [n for n, _ in SMALL]
    d_, m_, v2_ = _adamw("adamw_small", _pack_small([w[n] for n in sm]), small_sum, _pack_small([mom[n] for n in sm]),
                         _pack_small([var[n] for n in sm]))
    like = [w[n] for n in sm]
    for dst, src in ((grads, small_sum), (delta, d_), (new_m, m_), (new_v, v2_)):
        for n, a in zip(sm, _unpack_small(src, like)):
            dst[n] = a

    wn = names
    return (loss, grad_x[None], *[grads[n] for n in wn], *[delta[n] for n in wn], *[new_m[n] for n in wn],
            *[new_v[n] for n in wn])
```

```python
import functools
import math

import jax
import jax.numpy as jnp
from jax import lax
from jax.experimental import pallas as pl
from jax.experimental.pallas import tpu as pltpu

F32 = jnp.float32
BF16 = jnp.bfloat16

D = 1024
DFF = 2816
DR = 512
HS = 64
D_SHIFT = 1792
D_IN = 2816
CW = 31
RMS_EPS = 1e-6
GN_EPS = 64e-5
LN_EPS = 1e-5
DECAY_SCALE = math.exp(-0.5)
ADAM_LR, ADAM_B1, ADAM_B2, ADAM_EPS, ADAM_WD, ADAM_STEP = 0.001, 0.9, 0.999, 1e-8, 0.01, 10

REC_TILE = 128
VMEM_LIMIT = 56 * 1024 * 1024
MESH = pl.DeviceIdType.MESH
ANY = pl.BlockSpec(memory_space=pl.ANY)

BIG = (("ffn1_w_gu", 2 * DFF, True), ("ffn1_w_down", DFF, False), ("w_in", D_IN, True),
       ("w_out", D, False), ("ffn2_w_gu", 2 * DFF, True), ("ffn2_w_down", DFF, False))
SECOND = ("ffn1_w_down", "w_in")
LATE = ("w_out", "ffn2_w_gu", "ffn2_w_down")
PACK_W = 384
SMALL = (("ffn1_norm_pre", D), ("ffn1_norm_post", D), ("mix_norm_pre", D), ("mix_norm_post", D),
         ("shift_mu", D_SHIFT), ("w0", DR), ("a0", DR), ("k_k", DR), ("k_a", DR), ("r_k", DR),
         ("gn_w", DR), ("gn_b", DR), ("conv_b", DR), ("conv_ln_w", DR), ("conv_ln_b", DR),
         ("ffn2_norm_pre", D), ("ffn2_norm_post", D))
SMALL_N = sum(n for _, n in SMALL)
SMALL_PAD = 8 * 1664


def _params(sem):
    return pltpu.CompilerParams(dimension_semantics=sem, vmem_limit_bytes=VMEM_LIMIT)


def _tile(n, prefs):
    for p in prefs:
        if n % p == 0:
            return p
    return n


def _rows_call(name, fn, rows, consts, row_outs, acc_outs, tm):
    specs, arrs = [], []
    for r in rows:
        if isinstance(r, tuple):
            a, bs, im = r
            specs.append(pl.BlockSpec(bs, im))
        else:
            a = r
            specs.append(pl.BlockSpec((tm, a.shape[1]), lambda i: (i, 0)))
        arrs.append(a)
    t_rows = arrs[0].shape[0]
    for c in consts:
        specs.append(pl.BlockSpec(c.shape, functools.partial(lambda i, n: (0,) * n, n=c.ndim)))
        arrs.append(c)
    n_in, n_o, n_a = len(arrs), len(row_outs), len(acc_outs)

    def kern(*refs):
        i = pl.program_id(0)
        vals = [r[...] for r in refs[:n_in]]
        ro, ao = fn(i, *vals)
        outs = refs[n_in:]
        for k in range(n_o):
            outs[k][...] = ro[k].astype(outs[k].dtype)
        if n_a:
            @pl.when(i == 0)
            def _():
                for k in range(n_a):
                    outs[n_o + k][...] = jnp.zeros(outs[n_o + k].shape, F32)
            for k in range(n_a):
                outs[n_o + k][...] += ao[k]

    out_specs = [pl.BlockSpec((tm, w), lambda i: (i, 0)) for (w, _) in row_outs]
    out_specs += [pl.BlockSpec(s, functools.partial(lambda i, n: (0,) * n, n=len(s))) for s in acc_outs]
    out_shape = [jax.ShapeDtypeStruct((t_rows, w), dt) for (w, dt) in row_outs]
    out_shape += [jax.ShapeDtypeStruct(s, F32) for s in acc_outs]
    return pl.pallas_call(kern, grid=(t_rows // tm,), in_specs=specs, out_specs=out_specs, out_shape=out_shape,
                          name=name, compiler_params=_params(("arbitrary",)))(*arrs)


_DIMS = {"nn": (((1,), (0,)), ((), ())), "nt": (((1,), (1,)), ((), ())), "tn": (((0,), (0,)), ((), ()))}
_LANE_TILES = (1408, 1024, 512, 384, 256, 128)


def _matmul(name, a, b, mode, out_dtype=F32, owners=None, gather=None):
    if mode == "nn":
        (m, k), (_, n) = a.shape, b.shape
    elif mode == "nt":
        (m, k), (n, _) = a.shape, b.shape
    else:
        (k, m), (_, n) = a.shape, b.shape
    if mode == "tn":
        tm, tk = _tile(m, _LANE_TILES), _tile(k, (512, 256, 128))
    else:
        tm, tk = _tile(m, (1024, 512, 256, 128)), _tile(k, _LANE_TILES)
    tn = _tile(n, _LANE_TILES)
    nk = k // tk
    assert out_dtype == F32 or nk == 1

    owners, gather = list(owners or ()), list(gather or ())
    assert not (owners and gather)
    riders = owners + gather
    ns = len(riders)
    grid = (m // tm, n // tn, nk)
    steps = grid[0] * grid[1] * nk

    def kern(a_ref, b_ref, *rest):
        o_ref = rest[ns]
        if ns:
            step = (pl.program_id(0) * grid[1] + pl.program_id(1)) * nk + pl.program_id(2)
            if owners:
                start, finish = _owners_plan(rest[:ns], rest[ns + 1:2 * ns + 1], *rest[2 * ns + 1:])
                pl.when(step == 0)(start)
            else:
                start, middle, finish = _gather_plan(rest[:ns], rest[ns + 1:2 * ns + 1], *rest[2 * ns + 1:])
                pl.when(step == 0)(start)
                pl.when(step == steps // 2)(middle)

        def part():
            return lax.dot_general(a_ref[...].astype(BF16), b_ref[...].astype(BF16), _DIMS[mode], preferred_element_type=F32)

        if nk == 1:
            o_ref[...] = part().astype(o_ref.dtype)
        else:
            @pl.when(pl.program_id(2) == 0)
            def _():
                o_ref[...] = jnp.zeros(o_ref.shape, F32)

            o_ref[...] += part()
        if ns:
            pl.when(step == steps - 1)(finish)

    a_spec = pl.BlockSpec((tk, tm), lambda i, j, q: (q, i)) if mode == "tn" else pl.BlockSpec((tm, tk), lambda i, j, q: (i, q))
    b_spec = pl.BlockSpec((tn, tk), lambda i, j, q: (j, q)) if mode == "nt" else pl.BlockSpec((tk, tn), lambda i, j, q: (q, j))
    extra_shapes = _owner_shapes(owners) if owners else _gathered_shapes(gather)
    extra_sems = [] if not ns else _owner_sems(ns) if owners else _gather_sems(ns)
    out, *arrived = pl.pallas_call(
        kern, grid=grid, in_specs=[a_spec, b_spec] + [ANY] * ns,
        out_specs=[pl.BlockSpec((tm, tn), lambda i, j, q: (i, j))] + [ANY] * ns,
        out_shape=[jax.ShapeDtypeStruct((m, n), out_dtype)] + extra_shapes,
        scratch_shapes=extra_sems, name=name,
        compiler_params=_params(("arbitrary", "arbitrary", "arbitrary")))(a, b, *riders)
    if gather:
        arrived = _fill_own(arrived, gather)
    return (out, arrived) if ns else out


def _rms(x, g):
    return x * lax.rsqrt(jnp.mean(x * x, axis=-1, keepdims=True) + RMS_EPS) * g


def _silu(x):
    return x * jax.nn.sigmoid(x)


def _bones(n):
    r = lax.broadcasted_iota(jnp.int32, (n, n), 0) // HS
    c = lax.broadcasted_iota(jnp.int32, (n, n), 1) // HS
    return (r == c).astype(BF16)


@jax.custom_vjp
def _segsum(x):
    bones = _bones(x.shape[1])
    hi = x.astype(BF16)
    lo = (x - hi.astype(F32)).astype(BF16)
    return jnp.dot(hi, bones, preferred_element_type=F32) + jnp.dot(lo, bones, preferred_element_type=F32)


_segsum.defvjp(lambda x: (_segsum(x), None), lambda _, ct: (_segsum(ct),))


@jax.custom_vjp
def _dot_nt(x, w):
    return lax.dot_general(x.astype(BF16), w.astype(BF16), _DIMS["nt"], preferred_element_type=F32)


def _dot_nt_bwd(res, ct):
    x, w = res
    ctb = ct.astype(BF16)
    dx = lax.dot_general(ctb, w.astype(BF16), _DIMS["nn"], preferred_element_type=F32)
    dw = lax.dot_general(ctb, x.astype(BF16), _DIMS["tn"], preferred_element_type=F32)
    return dx, dw


_dot_nt.defvjp(lambda x, w: (_dot_nt(x, w), (x, w)), _dot_nt_bwd)


def _prep(ps, w0, a0, k_k, k_a, p01, p2):
    r, k, v = ps[:, :DR], ps[:, DR:2 * DR], ps[:, 2 * DR:3 * DR]
    wa, xg = ps[:, 3 * DR:3 * DR + 128], ps[:, 3 * DR + 128:]
    first = lax.broadcasted_iota(jnp.int32, (1, 128), 1) < 64
    d = w0 + _dot_nt(jnp.where(first, jnp.tanh(wa), 0.0), p01)
    decay = jnp.exp(-DECAY_SCALE * jax.nn.sigmoid(d))
    a = jax.nn.sigmoid(a0 + _dot_nt(jnp.where(first, 0.0, wa), p01))
    g = _dot_nt(jax.nn.sigmoid(xg), p2)
    kk = k * k_k
    kk = kk * lax.rsqrt(jnp.maximum(_segsum(kk * kk), 1e-12))
    k2 = k * (1.0 + (a - 1.0) * k_a)
    return r, decay, k2, v, -kk, kk * a, g


def _post(y, r, k, v, g, gn_w, gn_b, r_k):
    mu = _segsum(y) * (1.0 / HS)
    yc = y - mu
    var = _segsum(yc * yc) * (1.0 / HS)
    yn = yc * lax.rsqrt(var + GN_EPS) * gn_w + gn_b
    return (yn + _segsum(r * k * r_k) * v) * g


def _ln_silu(c, w, b):
    mu = jnp.mean(c, axis=-1, keepdims=True)
    var = jnp.mean(jnp.square(c - mu), axis=-1, keepdims=True)
    return _silu((c - mu) * lax.rsqrt(var + LN_EPS) * w + b)


def _colsum(x):
    return jnp.sum(x, axis=0, keepdims=True)


def _gu_swiglu(name, h, w_gu_t, gather=None):
    m = h.shape[0]
    tm, tn = _tile(m, (512, 256, 128)), 1408
    nj = DFF // tn
    gather = list(gather or ())
    ns = len(gather)
    steps = (m // tm) * nj

    def kern(a_ref, bg_ref, bu_ref, *rest):
        gate_ref, up_ref, act_ref = rest[ns:ns + 3]
        if ns:
            step = pl.program_id(0) * nj + pl.program_id(1)
            start, middle, finish = _gather_plan(rest[:ns], rest[ns + 3:2 * ns + 3], *rest[2 * ns + 3:])
            pl.when(step == 0)(start)
            pl.when(step == steps // 2)(middle)
        a = a_ref[...]
        gate = lax.dot_general(a, bg_ref[...], _DIMS["nt"], preferred_element_type=F32)
        up = lax.dot_general(a, bu_ref[...], _DIMS["nt"], preferred_element_type=F32)
        gate_ref[...] = gate
        up_ref[...] = up
        act_ref[...] = (_silu(gate) * up).astype(BF16)
        if ns:
            pl.when(step == steps - 1)(finish)

    tile = pl.BlockSpec((tm, tn), lambda i, j: (i, j))
    gate, up, act, *got = pl.pallas_call(
        kern, grid=(m // tm, nj),
        in_specs=[pl.BlockSpec((tm, D), lambda i, j: (i, 0)), pl.BlockSpec((tn, D), lambda i, j: (j, 0)),
                  pl.BlockSpec((tn, D), lambda i, j: (j + nj, 0))] + [ANY] * ns,
        out_specs=[tile] * 3 + [ANY] * ns,
        out_shape=[jax.ShapeDtypeStruct((m, DFF), F32)] * 2 + [jax.ShapeDtypeStruct((m, DFF), BF16)] + _gathered_shapes(gather),
        scratch_shapes=_gather_sems(ns) if ns else [], name=name,
        compiler_params=_params(("arbitrary", "arbitrary")))(h, w_gu_t, w_gu_t, *gather)
    return gate, up, act, _fill_own(got, gather)


def _dact_swiglu_b(name, df, w_down, gate, up):
    m = df.shape[0]
    tm, tn = _tile(m, (512, 256, 128)), 1408
    nj = DFF // tn

    def kern(df_ref, wd_ref, gate_ref, up_ref, o_ref):
        dact = lax.dot_general(df_ref[...], wd_ref[...], _DIMS["nt"], preferred_element_type=F32)
        _, vjp = jax.vjp(lambda a, b: _silu(a) * b, gate_ref[...], up_ref[...])
        dgate, dup = vjp(dact)
        for j in range(nj):
            @pl.when(pl.program_id(1) == j)
            def _(j=j):
                o_ref[:, tn * j:tn * (j + 1)] = dgate.astype(BF16)
                o_ref[:, DFF + tn * j:DFF + tn * (j + 1)] = dup.astype(BF16)

    tile = pl.BlockSpec((tm, tn), lambda i, j: (i, j))
    return pl.pallas_call(
        kern, grid=(m // tm, nj),
        in_specs=[pl.BlockSpec((tm, D), lambda i, j: (i, 0)), pl.BlockSpec((tn, D), lambda i, j: (j, 0)), tile, tile],
        out_specs=pl.BlockSpec((tm, 2 * DFF), lambda i, j: (i, 0)),
        out_shape=jax.ShapeDtypeStruct((m, 2 * DFF), BF16), name=name,
        compiler_params=_params(("arbitrary", "arbitrary")))(df, w_down, gate, up)


def _ffn_fwd(tag, x, pre, w_gu_t, w_down, tm, gather=None, h=None):
    if h is None:
        (h,) = _rows_call(tag + "_norm", lambda i, xv, g: ((_rms(xv, g),), ()), [x], [pre], [(D, BF16)], [], tm)
    gate, up, act, gathered = _gu_swiglu(tag + "_gu", h, w_gu_t, gather)
    if gather:
        w_down = gathered[0].reshape(DFF, D)
    f = _matmul(tag + "_down", act, w_down, "nn")
    return h, (gate, up), act, f, gathered


def _norm_resid_b(name, x, dh, dxo, pre, tm, below=None):
    def fn(i, *vals):
        xv, dhv, dv = vals[:3]
        _, vjp = jax.vjp(_rms, xv, vals[-1] if below is None else vals[-2])
        dx, dg = vjp(dhv)
        dx = dx + dv
        if below is None:
            return (dx,), (dg,)
        _, vjp = jax.vjp(lambda a, b: below[2] * _rms(a, b), vals[3], vals[-1])
        df, dgp = vjp(dx)
        return (dx, df), (dg, dgp)
    if below is None:
        return _rows_call(name, fn, [x, dh, dxo], [pre], [(D, F32)], [(1, D)], tm)
    return _rows_call(name, fn, [x, dh, dxo, below[0]], [pre, below[1]], [(D, F32), (D, BF16)], [(1, D)] * 2, tm)


def _ffn_bwd(tag, x, pre, h, gu, act, dxo, df, w_gu_t, w_down, tm, after_down=None, after_gu=None, below=None):
    dgu = _dact_swiglu_b(tag + "_dact", df, w_down, *gu)
    dw_down = _matmul(tag + "_dwdown", act, df, "tn")
    sums = after_down(dw_down) if after_down else []
    dw_gu_t = _matmul(tag + "_dwgu", dgu, h, "tn", owners=sums)
    dw_gu_t, parts_down = dw_gu_t if sums else (dw_gu_t, [])
    sums = after_gu(dw_gu_t) if after_gu else []
    dh = _matmul(tag + "_dh", dgu, w_gu_t, "nn", owners=sums)
    dh, parts_gu = dh if sums else (dh, [])
    parts = (parts_down, parts_gu)
    return _norm_resid_b(tag + "_norm_b", x, dh, dxo, pre, tm, below), dw_gu_t, dw_down, parts


def _pair_bcast(cols, first):
    return jnp.concatenate([jnp.where(first, cols[2 * p], cols[2 * p + 1]) for p in range(4)], axis=1)


def _head_sums(x, first):
    cols = []
    for p in range(4):
        xp = x[:, 128 * p:128 * (p + 1)]
        cols.append(jnp.sum(jnp.where(first, xp, 0.0), axis=1, keepdims=True))
        cols.append(jnp.sum(jnp.where(first, 0.0, xp), axis=1, keepdims=True))
    return cols


def _split16(x8):
    hi = x8.astype(BF16).astype(F32)
    return jnp.concatenate([hi, x8 - hi], axis=0).astype(BF16)


def _cols8(x8, e16):
    return lax.dot_general(_split16(x8), e16, _DIMS["tn"], preferred_element_type=F32)


def _pair_rows(c, first):
    return jnp.concatenate([jnp.where(first, c[128 * p:128 * p + HS], c[128 * p + HS:128 * (p + 1)]) for p in range(4)], axis=1)


def _rows8(prod, bones, dmask):
    x = prod.astype(BF16)
    full = jnp.concatenate([jnp.dot(x[:, 256 * q:256 * (q + 1)], bones, preferred_element_type=F32) for q in range(2)], axis=1)
    return jnp.concatenate([_colsum(full[HS * j:HS * (j + 1)] * dmask) for j in range(8)], axis=0)


def _rec_step(s, wr, zr, br, kr, vc, first):
    u = _pair_bcast(_head_sums(s * zr, first), first)
    return s * wr + u * br + vc * kr, u


def _rec_consts():
    e16 = (lax.broadcasted_iota(jnp.int32, (16, 1024), 0) % 8 == lax.broadcasted_iota(jnp.int32, (16, 1024), 1) // 128)
    dmask = lax.broadcasted_iota(jnp.int32, (HS, DR), 0) == lax.broadcasted_iota(jnp.int32, (HS, DR), 1) % HS
    return e16.astype(BF16), _bones(256), dmask.astype(F32)


def _const_spec(a):
    return pl.BlockSpec(a.shape, functools.partial(lambda i, n: (0,) * n, n=a.ndim))


def _rec_fwd(r, w, k, z, b, v, shards):
    t_len = r.shape[0]
    nt = t_len // REC_TILE
    ns = len(shards)
    consts = _rec_consts()

    def kern(r_ref, w_ref, k_ref, z_ref, b_ref, v_ref, e_ref, bones_ref, dm_ref, *rest):
        y_ref, ck_ref, states, u_ref = rest[ns:ns + 4]
        s_ref, prod = rest[2 * ns + 4:2 * ns + 6]
        start, middle, finish = _gather_plan(rest[:ns], rest[ns + 4:2 * ns + 4], *rest[2 * ns + 6:])
        i = pl.program_id(0)

        @pl.when(i == 0)
        def _():
            s_ref[...] = jnp.zeros(s_ref.shape, F32)
            start()

        pl.when(i == nt // 2)(middle)
        ck_ref[0] = s_ref[...]
        first = lax.broadcasted_iota(jnp.int32, (1, 128), 1) < HS

        def group(g8, s):
            base = pl.multiple_of(g8 * 8, 8)
            r8, w8, k8, z8, b8, v8 = (ref[pl.ds(base, 8), :] for ref in (r_ref, w_ref, k_ref, z_ref, b_ref, v_ref))
            vcols = _cols8(v8, e_ref[...])
            urows = []
            for j in range(8):
                vc = _pair_rows(vcols[:, 128 * j:128 * (j + 1)], first)
                s, u = _rec_step(s, w8[j:j + 1], z8[j:j + 1], b8[j:j + 1], k8[j:j + 1], vc, first)
                states[base + j] = s
                urows.append(_colsum(u * dm_ref[...]))
                prod[HS * j:HS * (j + 1), :] = s * r8[j:j + 1]
            y_ref[pl.ds(base, 8), :] = _rows8(prod[...], bones_ref[...], dm_ref[...])
            u_ref[pl.ds(base, 8), :] = jnp.concatenate(urows, axis=0)
            return s

        s_ref[...] = lax.fori_loop(0, REC_TILE // 8, group, s_ref[...])
        pl.when(i == nt - 1)(finish)

    row = pl.BlockSpec((REC_TILE, DR), lambda i: (i, 0))
    y, ck, states, u, *got = pl.pallas_call(
        kern, grid=(nt,), in_specs=[row] * 6 + [_const_spec(c) for c in consts] + [ANY] * ns,
        out_specs=[row, pl.BlockSpec((1, HS, DR), lambda i: (i, 0, 0)), pl.BlockSpec((REC_TILE, HS, DR), lambda i: (i, 0, 0)), row]
        + [ANY] * ns,
        out_shape=[jax.ShapeDtypeStruct((t_len, DR), F32), jax.ShapeDtypeStruct((nt, HS, DR), F32),
                   jax.ShapeDtypeStruct((t_len, HS, DR), F32), jax.ShapeDtypeStruct((t_len, DR), F32)] + _gathered_shapes(shards),
        scratch_shapes=[pltpu.VMEM((HS, DR), F32), pltpu.VMEM((8 * HS, DR), F32)] + _gather_sems(ns), name="rec_fwd",
        compiler_params=_params(("arbitrary",)))(r, w, k, z, b, v, *consts, *shards)
    return y, (ck, states, u), _fill_own(got, shards)


def _rec_bwd(r, w, k, z, b, v, dy, saved, sums):
    t_len = r.shape[0]
    nt = t_len // REC_TILE
    ns = len(sums)
    consts = _rec_consts()

    def kern(r_ref, w_ref, k_ref, z_ref, b_ref, v_ref, dy_ref, u_ref, ck_ref, states, e_ref, bones_ref, dm_ref, *rest):
        dr_ref, dw_ref, dk_ref, dz_ref, db_ref, dv_ref = rest[ns:ns + 6]
        ds_ref, prod = rest[2 * ns + 6:2 * ns + 8]
        start, finish = _owners_plan(rest[:ns], rest[ns + 6:2 * ns + 6], *rest[2 * ns + 8:])
        i = pl.program_id(0)

        @pl.when(i == 0)
        def _():
            ds_ref[...] = jnp.zeros(ds_ref.shape, F32)
            start()

        first = lax.broadcasted_iota(jnp.int32, (1, 128), 1) < HS

        def bgroup(gg, ds):
            base = pl.multiple_of((REC_TILE // 8 - 1 - gg) * 8, 8)
            r8, w8, k8, z8, b8, v8, dy8, u8 = (ref[pl.ds(base, 8), :]
                                               for ref in (r_ref, w_ref, k_ref, z_ref, b_ref, v_ref, dy_ref, u_ref))
            vcols = _cols8(v8, e_ref[...])
            dycols = _cols8(dy8, e_ref[...])
            ucols = _cols8(u8, e_ref[...])
            before = jnp.where(base == 0, ck_ref[0], states[jnp.maximum(base - 1, 0)])
            rows = {n: [None] * 8 for n in ("dr", "dw", "dk", "dz", "db")}
            for j in range(7, -1, -1):
                t = base + j
                rr, wr, kr, zr, br = (x[j:j + 1] for x in (r8, w8, k8, z8, b8))
                s_prev, s_t = (states[t - 1] if j else before), states[t]
                dyc = _pair_rows(dycols[:, 128 * j:128 * (j + 1)], first)
                vc = _pair_rows(vcols[:, 128 * j:128 * (j + 1)], first)
                ds = ds + dyc * rr
                rows["dr"][j] = _colsum(s_t * dyc)
                rows["dw"][j] = _colsum(ds * s_prev)
                du = _pair_bcast(_head_sums(ds * br, first), first)
                rows["db"][j] = _colsum(ds * _pair_rows(ucols[:, 128 * j:128 * (j + 1)], first))
                rows["dk"][j] = _colsum(ds * vc)
                prod[HS * j:HS * (j + 1), :] = ds * kr
                rows["dz"][j] = _colsum(s_prev * du)
                ds = ds * wr + du * zr
            for n, ref in (("dr", dr_ref), ("dw", dw_ref), ("dk", dk_ref), ("dz", dz_ref), ("db", db_ref)):
                ref[pl.ds(base, 8), :] = jnp.concatenate(rows[n], axis=0)
            dv_ref[pl.ds(base, 8), :] = _rows8(prod[...], bones_ref[...], dm_ref[...])
            return ds

        ds_ref[...] = lax.fori_loop(0, REC_TILE // 8, bgroup, ds_ref[...])
        pl.when(i == nt - 1)(finish)

    ck, states, u = saved
    row = pl.BlockSpec((REC_TILE, DR), lambda i: (nt - 1 - i, 0))
    outs = pl.pallas_call(
        kern, grid=(nt,),
        in_specs=[row] * 8 + [pl.BlockSpec((1, HS, DR), lambda i: (nt - 1 - i, 0, 0)),
                              pl.BlockSpec((REC_TILE, HS, DR), lambda i: (nt - 1 - i, 0, 0))]
        + [_const_spec(c) for c in consts] + [ANY] * ns,
        out_specs=[row] * 6 + [ANY] * ns, out_shape=[jax.ShapeDtypeStruct((t_len, DR), F32)] * 6 + _owner_shapes(sums),
        scratch_shapes=[pltpu.VMEM((HS, DR), F32), pltpu.VMEM((8 * HS, DR), F32)] + _owner_sems(ns), name="rec_bwd",
        compiler_params=_params(("arbitrary",)))(r, w, k, z, b, v, dy, u, ck, states, *consts, *sums)
    return outs[:6], outs[6:]


def _prev_rows(a, tm, n):
    return (a, (n, a.shape[1]), lambda i: (jnp.maximum(i * (tm // n) - 1, 0), 0))


def _next_rows(a, tm, n):
    last = a.shape[0] // n - 1
    return (a, (n, a.shape[1]), lambda i: (jnp.minimum((i + 1) * (tm // n), last), 0))


def _shifted(x, prev8, i):
    rowid = lax.broadcasted_iota(jnp.int32, x.shape, 0)
    before = jnp.where(i == 0, 0.0, prev8[7:8, :])
    return jnp.where(rowid == 0, before, pltpu.roll(x, 1, 0))


def _shift_fwd(p, mu, tm):
    def fn(i, pv, prev8, muv):
        x = pv[:, :D_SHIFT]
        return (x + (_shifted(x, prev8[:, :D_SHIFT], i) - x) * muv,), ()
    return _rows_call("shift", fn, [p, _prev_rows(p, tm, 8)], [mu], [(D_SHIFT, F32)], [], tm)[0]


def _shift_bwd(dps, p, dpc, mu, tm):
    n_tiles = p.shape[0] // tm

    def fn(i, dv, next8, pv, prev8, dpcv, muv):
        x = pv[:, :D_SHIFT]
        xs = _shifted(x, prev8[:, :D_SHIFT], i)
        rowid = lax.broadcasted_iota(jnp.int32, dv.shape, 0)
        after = jnp.where(i == n_tiles - 1, 0.0, next8[0:1, :])
        dnext = jnp.where(rowid == tm - 1, after, pltpu.roll(dv, tm - 1, 0))
        dp_s = dv * (1.0 - muv) + dnext * muv
        return (jnp.concatenate([dp_s.astype(BF16), dpcv], axis=1),), (_colsum(dv * (xs - x)),)
    return _rows_call("shift_b", fn, [dps, _next_rows(dps, tm, 8), p, _prev_rows(p, tm, 8), dpc], [mu],
                      [(D_IN, BF16)], [(1, D_SHIFT)], tm)


def _glu(pc):
    return pc[:, :DR] * jax.nn.sigmoid(pc[:, DR:])


def _shift_copies(ext, shifted, tm):
    for s in range(1, 8):
        shifted[s - 1] = ext[s:s + tm + 24, :]


def _window(ext, shifted, off, tm):
    if off % 8 == 0:
        return ext[off:off + tm, :]
    return shifted[off % 8 - 1, off // 8 * 8:off // 8 * 8 + tm, :]


def _conv_fwd(p, dw32, cb, lw, lb, tm):
    t_len = p.shape[0]

    def kern(p_ref, ph_ref, dw_ref, cb_ref, lw_ref, lb_ref, glu_ref, c_ref, ob_ref, ext, shifted):
        i = pl.program_id(0)
        glu = _glu(p_ref[:, D_SHIFT:])
        ext[0:32, :] = jnp.where(i == 0, 0.0, _glu(ph_ref[:, D_SHIFT:]))
        ext[32:, :] = glu
        _shift_copies(ext, shifted, tm)
        acc = jnp.zeros((tm, DR), F32)
        for j in range(CW):
            acc = acc + _window(ext, shifted, 2 + j, tm) * dw_ref[j:j + 1, :]
        c = acc + cb_ref[...]
        glu_ref[...] = glu
        c_ref[...] = c
        ob_ref[...] = _ln_silu(c, lw_ref[...], lb_ref[...]).astype(BF16)

    tile = lambda w: pl.BlockSpec((tm, w), lambda i: (i, 0))
    const = lambda a: pl.BlockSpec(a.shape, lambda i: (0, 0))
    halo = pl.BlockSpec((32, D_IN), lambda i: (jnp.maximum(i * (tm // 32) - 1, 0), 0))
    return pl.pallas_call(
        kern, grid=(t_len // tm,), in_specs=[tile(D_IN), halo, const(dw32), const(cb), const(lw), const(lb)],
        out_specs=[tile(DR)] * 3,
        out_shape=[jax.ShapeDtypeStruct((t_len, DR), F32)] * 2 + [jax.ShapeDtypeStruct((t_len, DR), BF16)],
        scratch_shapes=[pltpu.VMEM((tm + 32, DR), F32), pltpu.VMEM((7, tm + 24, DR), F32)], name="conv_fwd",
        compiler_params=_params(("arbitrary",)))(p, p, dw32, cb, lw, lb)


def _conv_bwd(dc, glu, p, dw32, tm):
    t_len = p.shape[0]
    n_tiles = t_len // tm

    def kern(dc_ref, dcn_ref, glu_ref, gluh_ref, p_ref, dw_ref, dpc_ref, ddw_ref, ext_d, ext_g, shifted_d, shifted_g):
        i = pl.program_id(0)

        @pl.when(i == 0)
        def _():
            ddw_ref[...] = jnp.zeros(ddw_ref.shape, F32)

        dcv = dc_ref[...]
        ext_d[0:tm, :] = dcv
        ext_d[tm:, :] = jnp.where(i == n_tiles - 1, 0.0, dcn_ref[...])
        ext_g[0:32, :] = jnp.where(i == 0, 0.0, gluh_ref[...])
        ext_g[32:, :] = glu_ref[...]
        _shift_copies(ext_d, shifted_d, tm)
        _shift_copies(ext_g, shifted_g, tm)
        dglu = jnp.zeros((tm, DR), F32)
        for j in range(CW):
            dglu = dglu + _window(ext_d, shifted_d, 30 - j, tm) * dw_ref[j:j + 1, :]
            ddw_ref[j:j + 1, :] += _colsum(dcv * _window(ext_g, shifted_g, 2 + j, tm))
        pc = p_ref[:, D_SHIFT:]
        sg = jax.nn.sigmoid(pc[:, DR:])
        dpc_ref[...] = jnp.concatenate([dglu * sg, dglu * pc[:, :DR] * sg * (1.0 - sg)], axis=1).astype(BF16)

    tile = lambda w: pl.BlockSpec((tm, w), lambda i: (i, 0))
    nxt = pl.BlockSpec((32, DR), lambda i: (jnp.minimum((i + 1) * (tm // 32), t_len // 32 - 1), 0))
    prv = pl.BlockSpec((32, DR), lambda i: (jnp.maximum(i * (tm // 32) - 1, 0), 0))
    return pl.pallas_call(
        kern, grid=(n_tiles,), in_specs=[tile(DR), nxt, tile(DR), prv, tile(D_IN), pl.BlockSpec((32, DR), lambda i: (0, 0))],
        out_specs=[tile(D), pl.BlockSpec((32, DR), lambda i: (0, 0))],
        out_shape=[jax.ShapeDtypeStruct((t_len, D), BF16), jax.ShapeDtypeStruct((32, DR), F32)],
        scratch_shapes=[pltpu.VMEM((tm + 32, DR), F32)] * 2 + [pltpu.VMEM((7, tm + 24, DR), F32)] * 2, name="conv_bwd",
        compiler_params=_params(("arbitrary",)))(dc, dc, glu, glu, p, dw32)


def _place():
    x, y, c = lax.axis_index("x"), lax.axis_index("y"), lax.axis_index("c")
    chips = [(1 - x, y), (x, 1 - y), (1 - x, 1 - y)]
    return x, y, c, chips


def _gather_weights(shards):
    n = len(shards)

    def body(*refs):
        start, middle, finish = _gather_plan(refs[:n], refs[n:2 * n], *refs[2 * n:])
        start()
        middle()
        finish()

    got = pl.pallas_call(body, in_specs=[ANY] * n, out_specs=[ANY] * n, out_shape=_gathered_shapes(shards),
                         scratch_shapes=_gather_sems(n), name="gather_weights")(*shards)
    return _fill_own(got, shards)


def _gathered_shapes(shards):
    return [jax.ShapeDtypeStruct((4,) + s.shape, s.dtype) for s in shards]


def _gather_sems(n):
    return [pltpu.SemaphoreType.DMA((6 * n,)), pltpu.SemaphoreType.DMA((6 * n,))]


def _fill_own(got, shards):
    me = 2 * lax.axis_index("x") + lax.axis_index("y")
    return [lax.dynamic_update_slice(g, s[None], (me, 0, 0, 0)) for g, s in zip(got, shards)]


def _gather_plan(src, dst, send, recv):
    n = len(src)
    x, y, c, chips = _place()
    me, sib = 2 * x + y, (x, y, 1 - c)

    def rcopy(k, sem, s_ref, d_ref, to):
        return pltpu.make_async_remote_copy(src_ref=s_ref, dst_ref=d_ref, send_sem=send.at[6 * k + sem],
                                            recv_sem=recv.at[6 * k + sem], device_id=to, device_id_type=MESH)

    def landed(k, m, half):
        return dst[k].at[2 * chips[m][0] + chips[m][1], half]

    first = [rcopy(k, m, src[k].at[c], dst[k].at[me, c], (*chips[m], c)) for k in range(n) for m in range(3)]
    passed = [rcopy(k, 3 + m, landed(k, m, c), landed(k, m, c), sib) for k in range(n) for m in range(3)]

    def start():
        for cp in first:
            cp.start()

    def middle():
        for k in range(n):
            for m in range(3):
                rcopy(k, m, landed(k, m, c), landed(k, m, c), sib).wait_recv()
                passed[3 * k + m].start()

    def finish():
        for k in range(n):
            for m in range(3):
                rcopy(k, 3 + m, landed(k, m, 1 - c), landed(k, m, 1 - c), sib).wait_recv()
        for cp in first + passed:
            cp.wait_send()

    return start, middle, finish


def _swap_halves(name, give):
    n = len(give)

    def body(*refs):
        src, got = refs[:n], refs[n:2 * n]
        send, recv = refs[2 * n:]
        x, y, c, _ = _place()
        copies = []
        for k in range(n):
            for j in range(4):
                copies.append(pltpu.make_async_remote_copy(
                    src_ref=src[k].at[j], dst_ref=got[k].at[j], send_sem=send.at[4 * k + j], recv_sem=recv.at[4 * k + j],
                    device_id=(x, y, 1 - c), device_id_type=MESH))
                copies[-1].start()
        for cp in copies:
            cp.wait()

    out_shape = [jax.ShapeDtypeStruct(g.shape, g.dtype) for g in give]
    return pl.pallas_call(body, in_specs=[ANY] * n, out_specs=[ANY] * n, out_shape=out_shape,
                          scratch_shapes=[pltpu.SemaphoreType.DMA((4 * n,)), pltpu.SemaphoreType.DMA((4 * n,))],
                          name=name)(*give)


def _owner_shapes(sums):
    return [jax.ShapeDtypeStruct((3,) + s.shape[1:], s.dtype) for s in sums]


def _owner_sems(n):
    return [pltpu.SemaphoreType.DMA((3 * n,)), pltpu.SemaphoreType.DMA((3 * n,))]


def _owners_plan(src, dst, send, recv):
    x, y, c, chips = _place()
    copies = [pltpu.make_async_remote_copy(
        src_ref=src[k].at[2 * chip[0] + chip[1]], dst_ref=dst[k].at[m], send_sem=send.at[3 * k + m],
        recv_sem=recv.at[3 * k + m], device_id=(*chip, c), device_id_type=MESH)
        for k in range(len(src)) for m, chip in enumerate(chips)]

    def start():
        for cp in copies:
            cp.start()

    def finish():
        for cp in copies:
            cp.wait()

    return start, finish


def _join_halves(halves):
    n = len(halves)

    def body(*refs):
        src, dst = refs[:n], refs[n:2 * n]
        send, recv = refs[2 * n:]
        x, y, c, _ = _place()
        copies = []
        for k in range(n):
            copies.append(pltpu.make_async_remote_copy(src_ref=src[k], dst_ref=dst[k], send_sem=send.at[k],
                                                       recv_sem=recv.at[k], device_id=(x, y, 1 - c), device_id_type=MESH))
            copies[-1].start()
        for cp in copies:
            cp.wait()

    out_shape = [jax.ShapeDtypeStruct(h.shape, h.dtype) for h in halves]
    return pl.pallas_call(body, in_specs=[ANY] * n, out_specs=[ANY] * n, out_shape=out_shape,
                          scratch_shapes=[pltpu.SemaphoreType.DMA((n,)), pltpu.SemaphoreType.DMA((n,))],
                          name="join_halves")(*halves)


def _allreduce_small(v):
    rows, n = v.shape

    def body(v_ref, o_ref, buf, send, recv):
        x, y, c, _ = _place()
        me = 4 * x + 2 * y + c
        buf[me] = v_ref[...]
        copies = []
        for d in range(1, 8):
            peer = (x ^ (d >> 2), y ^ ((d >> 1) & 1), c ^ (d & 1))
            cp = pltpu.make_async_remote_copy(src_ref=v_ref, dst_ref=buf.at[me], send_sem=send.at[d], recv_sem=recv.at[d],
                                              device_id=peer, device_id_type=MESH)
            cp.start()
            copies.append(cp)
        for d in range(1, 8):
            peer = 4 * (x ^ (d >> 2)) + 2 * (y ^ ((d >> 1) & 1)) + (c ^ (d & 1))
            pltpu.make_async_remote_copy(src_ref=v_ref, dst_ref=buf.at[peer], send_sem=send.at[d], recv_sem=recv.at[d],
                                         device_id=(x, y, c), device_id_type=MESH).wait_recv()
        for cp in copies:
            cp.wait_send()
        acc = buf[0]
        for d in range(1, 8):
            acc = acc + buf[d]
        o_ref[...] = acc

    vm = pl.BlockSpec(memory_space=pltpu.VMEM)
    return pl.pallas_call(body, in_specs=[vm], out_specs=vm, out_shape=jax.ShapeDtypeStruct((rows, n), F32),
                          scratch_shapes=[pltpu.VMEM((8, rows, n), F32), pltpu.SemaphoreType.DMA((8,)),
                                          pltpu.SemaphoreType.DMA((8,))], name="allreduce_small")(v)


def _add_call(name, parts, tm, out_dtype=F32):
    def fn(i, *vals):
        acc = vals[0].astype(F32)
        for v in vals[1:]:
            acc = acc + v.astype(F32)
        return (acc,), ()
    return _rows_call(name, fn, list(parts), [], [(parts[0].shape[1], out_dtype)], [], tm)[0]


def _adamw(name, w, g, m, v):
    c1 = 1.0 / (1.0 - ADAM_B1 ** ADAM_STEP)
    c2 = 1.0 / (1.0 - ADAM_B2 ** ADAM_STEP)

    def fn(i, wv, gv, mv, vv):
        m2 = ADAM_B1 * mv + (1.0 - ADAM_B1) * gv
        v2 = ADAM_B2 * vv + (1.0 - ADAM_B2) * jnp.square(gv)
        delta = -ADAM_LR * ((m2 * c1) / (jnp.sqrt(v2 * c2) + ADAM_EPS) + ADAM_WD * wv)
        return (delta, m2, v2), ()
    cols = w.shape[1]
    tm = _tile(w.shape[0], (256, 176, 128, 64, 8))
    return _rows_call(name, fn, [w, g, m, v], [], [(cols, F32)] * 3, [], tm)


def _canon(name, a, transposed):
    return a[0].T if transposed else a[0]


def _pack_sharded(w_up, a_up, g_up, conv_dw):
    parts = [w_up[0].T, a_up[0].T, g_up[0].T, conv_dw[0].T]
    used = sum(p.shape[1] for p in parts)
    return jnp.concatenate(parts + [jnp.zeros((parts[0].shape[0], PACK_W - used), F32)], axis=1)


def _unpack_sharded(a):
    return [a[:, 0:64].T[None], a[:, 64:128].T[None], a[:, 128:256].T[None], a[:, 256:256 + CW].T[None]]


def _pack_small(vals):
    flat = jnp.concatenate([v.reshape(-1) for v in vals] + [jnp.zeros((SMALL_PAD - SMALL_N,), F32)])
    return flat.reshape(8, SMALL_PAD // 8)


def _unpack_small(a, like):
    flat, out, off = a.reshape(-1), [], 0
    for (_, n), ref in zip(SMALL, like):
        out.append(flat[off:off + n].reshape(ref.shape))
        off += n
    return out


def kernel(x, ffn1_norm_pre, ffn1_norm_post, ffn1_w_gu, ffn1_w_down, mix_norm_pre, mix_norm_post, w_in, shift_mu, w_up, w0, a_up, a0, g_up, k_k, k_a, r_k, gn_w, gn_b, conv_dw, conv_b, conv_ln_w, conv_ln_b, w_out, ffn2_norm_pre, ffn2_norm_post, ffn2_w_gu, ffn2_w_down, loss_target, m_ffn1_norm_pre, m_ffn1_norm_post, m_ffn1_w_gu, m_ffn1_w_down, m_mix_norm_pre, m_mix_norm_post, m_w_in, m_shift_mu, m_w_up, m_w0, m_a_up, m_a0, m_g_up, m_k_k, m_k_a, m_r_k, m_gn_w, m_gn_b, m_conv_dw, m_conv_b, m_conv_ln_w, m_conv_ln_b, m_w_out, m_ffn2_norm_pre, m_ffn2_norm_post, m_ffn2_w_gu, m_ffn2_w_down, v_ffn1_norm_pre, v_ffn1_norm_post, v_ffn1_w_gu, v_ffn1_w_down, v_mix_norm_pre, v_mix_norm_post, v_w_in, v_shift_mu, v_w_up, v_w0, v_a_up, v_a0, v_g_up, v_k_k, v_k_a, v_r_k, v_gn_w, v_gn_b, v_conv_dw, v_conv_b, v_conv_ln_w, v_conv_ln_b, v_w_out, v_ffn2_norm_pre, v_ffn2_norm_post, v_ffn2_w_gu, v_ffn2_w_down):
    w = dict(ffn1_norm_pre=ffn1_norm_pre, ffn1_norm_post=ffn1_norm_post, ffn1_w_gu=ffn1_w_gu, ffn1_w_down=ffn1_w_down, mix_norm_pre=mix_norm_pre, mix_norm_post=mix_norm_post, w_in=w_in, shift_mu=shift_mu, w_up=w_up, w0=w0, a_up=a_up, a0=a0, g_up=g_up, k_k=k_k, k_a=k_a, r_k=r_k, gn_w=gn_w, gn_b=gn_b, conv_dw=conv_dw, conv_b=conv_b, conv_ln_w=conv_ln_w, conv_ln_b=conv_ln_b, w_out=w_out, ffn2_norm_pre=ffn2_norm_pre, ffn2_norm_post=ffn2_norm_post, ffn2_w_gu=ffn2_w_gu, ffn2_w_down=ffn2_w_down)
    mom = dict(ffn1_norm_pre=m_ffn1_norm_pre, ffn1_norm_post=m_ffn1_norm_post, ffn1_w_gu=m_ffn1_w_gu, ffn1_w_down=m_ffn1_w_down, mix_norm_pre=m_mix_norm_pre, mix_norm_post=m_mix_norm_post, w_in=m_w_in, shift_mu=m_shift_mu, w_up=m_w_up, w0=m_w0, a_up=m_a_up, a0=m_a0, g_up=m_g_up, k_k=m_k_k, k_a=m_k_a, r_k=m_r_k, gn_w=m_gn_w, gn_b=m_gn_b, conv_dw=m_conv_dw, conv_b=m_conv_b, conv_ln_w=m_conv_ln_w, conv_ln_b=m_conv_ln_b, w_out=m_w_out, ffn2_norm_pre=m_ffn2_norm_pre, ffn2_norm_post=m_ffn2_norm_post, ffn2_w_gu=m_ffn2_w_gu, ffn2_w_down=m_ffn2_w_down)
    var = dict(ffn1_norm_pre=v_ffn1_norm_pre, ffn1_norm_post=v_ffn1_norm_post, ffn1_w_gu=v_ffn1_w_gu, ffn1_w_down=v_ffn1_w_down, mix_norm_pre=v_mix_norm_pre, mix_norm_post=v_mix_norm_post, w_in=v_w_in, shift_mu=v_shift_mu, w_up=v_w_up, w0=v_w0, a_up=v_a_up, a0=v_a0, g_up=v_g_up, k_k=v_k_k, k_a=v_k_a, r_k=v_r_k, gn_w=v_gn_w, gn_b=v_gn_b, conv_dw=v_conv_dw, conv_b=v_conv_b, conv_ln_w=v_conv_ln_w, conv_ln_b=v_conv_ln_b, w_out=v_w_out, ffn2_norm_pre=v_ffn2_norm_pre, ffn2_norm_post=v_ffn2_norm_post, ffn2_w_gu=v_ffn2_w_gu, ffn2_w_down=v_ffn2_w_down)
    names = list(w)

    x0 = x[0]
    tgt = loss_target[0]
    t_len = x0.shape[0]
    tm = _tile(t_len, (512, 256, 128))

    half = {n: rows // 8 for n, rows, _ in BIG}
    pack_local = _pack_sharded(w["w_up"], w["a_up"], w["g_up"], w["conv_dw"])
    shard = {n: _canon(n, w[n], tr).astype(BF16).reshape(2, half[n], D) for n, _, tr in BIG}
    rows_of = {n: rows for n, rows, _ in BIG}
    full = {"ffn1_w_gu": _gather_weights([shard["ffn1_w_gu"]])[0].reshape(2 * DFF, D)}

    h1, gu1, act1, f1, gathered = _ffn_fwd("ffn1", x0, w["ffn1_norm_pre"], full["ffn1_w_gu"], None, tm,
                                           gather=[shard[n] for n in SECOND] + [pack_local.reshape(2, 64, PACK_W)])
    full.update({n: g.reshape(rows_of[n], D) for n, g in zip(SECOND, gathered)})
    pack = gathered[-1].reshape(DR, PACK_W)
    p01, p2 = pack[:, 0:128], pack[:, 128:256]
    dw32 = jnp.concatenate([pack[:, 256:256 + CW].T, jnp.zeros((1, DR), F32)], axis=0)
    def resid_norm(scale):
        def fn(i, xv, fv, g_post, g_pre):
            xn = xv + scale * _rms(fv, g_post)
            return (xn, _rms(xn, g_pre)), ()
        return fn
    x1, hm = _rows_call("ffn1_resid", resid_norm(0.5), [x0, f1], [w["ffn1_norm_post"], w["mix_norm_pre"]],
                        [(D, F32), (D, BF16)], [], tm)
    p = _matmul("mix_in", hm, full["w_in"], "nt")
    ps = _shift_fwd(p, w["shift_mu"], tm)
    rkc = w["r_k"].reshape(1, DR)
    prep_consts = [w["w0"], w["a0"], w["k_k"], w["k_a"], p01, p2]
    r_, dec, k2, v_, z_, b_, g_ = _rows_call(
        "prep", lambda i, psv, *cs: (_prep(psv, *cs), ()), [ps], prep_consts, [(DR, F32)] * 7, [], min(tm, 256))
    y, ck, late = _rec_fwd(r_, dec, k2, z_, b_, v_, [shard[n] for n in LATE])
    full.update({n: g.reshape(rows_of[n], D) for n, g in zip(LATE, late)})
    glu, cpre, ob = _conv_fwd(p, dw32, w["conv_b"], w["conv_ln_w"], w["conv_ln_b"], tm)
    post_consts = [w["gn_w"], w["gn_b"], rkc]
    (o,) = _rows_call(
        "post", lambda i, yv, rv, kv, vv, gv, obv, *cs: ((jnp.concatenate([_post(yv, rv, kv, vv, gv, *cs).astype(BF16), obv], axis=1),), ()),
        [y, r_, k2, v_, g_, ob], post_consts, [(D, BF16)], [], min(tm, 256))
    mo = _matmul("mix_out", o, full["w_out"], "nn")
    x2, h2 = _rows_call("mix_resid", resid_norm(1.0), [x1, mo], [w["mix_norm_post"], w["ffn2_norm_pre"]],
                        [(D, F32), (D, BF16)], [], tm)
    h2, gu2, act2, f2, _ = _ffn_fwd("ffn2", x2, w["ffn2_norm_pre"], full["ffn2_w_gu"], full["ffn2_w_down"], tm, h=h2)

    g_small = {}

    def loss_fn(i, xv, fv, tv, g):
        branch, vjp = jax.vjp(lambda a, b: 0.5 * _rms(a, b), fv, g)
        err = xv + branch - tv
        part = 0.5 * jnp.sum(jnp.mean(err * err, axis=-1, keepdims=True), axis=0, keepdims=True)
        dx = err * (1.0 / D)
        df, dg = vjp(dx)
        return (dx, df), (jnp.broadcast_to(part, (8, 128)), dg)
    dx3, df2, loss_part, g_small["ffn2_norm_post"] = _rows_call(
        "loss", loss_fn, [x2, f2, tgt], [w["ffn2_norm_post"]], [(D, F32), (D, BF16)], [(8, 128), (1, D)], tm)
    loss = lax.psum(loss_part[0, 0], ("x", "y", "c"))

    (dx2, dmo, g_small["ffn2_norm_pre"], g_small["mix_norm_post"]), dgu2_t, dwd2, _ = _ffn_bwd(
        "ffn2", x2, w["ffn2_norm_pre"], h2, gu2, act2, dx3, df2, full["ffn2_w_gu"], full["ffn2_w_down"], tm,
        below=(mo, w["mix_norm_post"], 1.0))
    do = _matmul("mix_do", dmo, full["w_out"], "nt")
    dw_out = _matmul("mix_dwout", o, dmo, "tn")

    def post_b(i, yv, rv, kv, vv, gv, dov, *cs):
        _, vjp = jax.vjp(_post, yv, rv, kv, vv, gv, *cs)
        dy, dr, dk, dv, dg, dgw, dgb, drk = vjp(dov[:, :DR])
        return (dy, dr, dk, dv, dg), (_colsum(dgw), _colsum(dgb), _colsum(drk))
    dy, dr1, dk1, dv1, dg, g_small["gn_w"], g_small["gn_b"], g_small["r_k"] = _rows_call(
        "post_b", post_b, [y, r_, k2, v_, g_, do], post_consts, [(DR, F32)] * 5, [(1, DR)] * 3, min(tm, 256))
    my_c = lax.axis_index("c")
    my_chip = 2 * lax.axis_index("x") + lax.axis_index("y")
    g_big = dict(w_out=dw_out, ffn2_w_gu=dgu2_t, ffn2_w_down=dwd2)

    def halves_of(group, which):
        return jnp.concatenate([lax.dynamic_index_in_dim(g_big[n].reshape(4, 2, half[n], D), which, 1, keepdims=False)
                                for n in group], axis=1)

    def pair_sums(tag, group, extra_mine=(), extra_give=()):
        mine = halves_of(group, my_c)
        got, *got_x = _swap_halves("swap_halves_" + tag, [halves_of(group, 1 - my_c).astype(BF16)] + list(extra_give))
        rows = mine.shape[1]
        tile = _tile(rows, (352, 592, 16))
        travels = _add_call("add_pair_" + tag, [mine.reshape(4 * rows, D), got.reshape(4 * rows, D)], tile, BF16)
        return mine, got, travels.reshape(4, rows, D), got_x

    def owner_sum(tag, mine, got, parts):
        own = [lax.dynamic_index_in_dim(a, my_chip, 0, keepdims=False) for a in (mine, got)]
        return _add_call("add_chips_" + tag, own + [parts[m] for m in range(3)], _tile(mine.shape[1], (352, 592, 16)))

    mine_l, got_l, sum_l, _ = pair_sums("late", LATE)
    (dr2, ddec, dk2, dz, db, dv2), (parts_l,) = _rec_bwd(r_, dec, k2, z_, b_, v_, dy, ck, [sum_l])

    def prep_b(i, psv, a1, a2, c1, c2, e1, e2, dwv, dzv, dbv, dgv, *cs):
        _, vjp = jax.vjp(_prep, psv, *cs)
        dps, dw0, da0, dkk, dka, dp01, dp2 = vjp((a1 + a2, dwv, c1 + c2, e1 + e2, dzv, dbv, dgv))
        return (dps,), (_colsum(dw0), _colsum(da0), _colsum(dkk), _colsum(dka), dp01, dp2)
    dps, g_small["w0"], g_small["a0"], g_small["k_k"], g_small["k_a"], dp01, dp2 = _rows_call(
        "prep_b", prep_b, [ps, dr1, dr2, dk1, dk2, dv1, dv2, ddec, dz, db, dg], prep_consts, [(D_SHIFT, F32)],
        [(1, DR)] * 4 + [(DR, 128)] * 2, min(tm, 256))

    def convln_b(i, cv, dov, lw, lb):
        _, vjp = jax.vjp(_ln_silu, cv, lw, lb)
        dc, dlw, dlb = vjp(dov[:, DR:])
        return (dc,), (_colsum(dc), _colsum(dlw), _colsum(dlb))
    dc, g_small["conv_b"], g_small["conv_ln_w"], g_small["conv_ln_b"] = _rows_call(
        "convln_b", convln_b, [cpre, do], [w["conv_ln_w"], w["conv_ln_b"]], [(DR, F32)], [(1, DR)] * 3, tm)
    dpc, ddw32 = _conv_bwd(dc, glu, p, dw32, tm)
    dp, g_small["shift_mu"] = _shift_bwd(dps, p, dpc, w["shift_mu"], tm)
    dhm = _matmul("mix_dh", dp, full["w_in"], "nn")
    dw_in_t = _matmul("mix_dwin", dp, hm, "tn")

    dx1, df1, g_small["mix_norm_pre"], g_small["ffn1_norm_post"] = _norm_resid_b(
        "mix_norm_b", x1, dhm, dx2, w["mix_norm_pre"], tm, below=(f1, w["ffn1_norm_post"], 0.5))
    kept = {}

    def second_sums(dwd1):
        g_big.update(ffn1_w_down=dwd1, w_in=dw_in_t)
        pack_grads = jnp.concatenate([dp01, dp2, ddw32.T, jnp.zeros((DR, PACK_W - 288), F32)], axis=1).reshape(4, 2, 64, PACK_W)
        mine_p, give_p = (lax.dynamic_index_in_dim(pack_grads, which, 1, keepdims=False) for which in (my_c, 1 - my_c))
        mine, got, travels, (got_p,) = pair_sums("second", SECOND, extra_give=[give_p])
        sum_p = _add_call("add_pair_pack", [mine_p.reshape(256, PACK_W), got_p.reshape(256, PACK_W)], 256).reshape(4, 64, PACK_W)
        kept.update(second=(mine, got), sum_p=sum_p)
        return [travels, sum_p]

    def first_sums(dgu1_t):
        g_big.update(ffn1_w_gu=dgu1_t)
        mine, got, travels, _ = pair_sums("first", ("ffn1_w_gu",))
        kept.update(first=(mine, got))
        return [travels]

    (grad_x, g_small["ffn1_norm_pre"]), _, _, ((parts_s, parts_p), (parts_f,)) = _ffn_bwd(
        "ffn1", x0, w["ffn1_norm_pre"], h1, gu1, act1, dx1, df1, full["ffn1_w_gu"], full["ffn1_w_down"], tm,
        after_down=second_sums, after_gu=first_sums)

    sum_p = kept["sum_p"]
    fin_s = owner_sum("second", *kept["second"], parts_s)
    fin_f = owner_sum("first", *kept["first"], parts_f)
    fin_l = owner_sum("late", mine_l, got_l, parts_l)
    fin_p = _add_call("add_chips_pack", [lax.dynamic_index_in_dim(sum_p, my_chip, 0, keepdims=False)] + [parts_p[m] for m in range(3)], 64)
    fins = [fin_f, fin_s, fin_l, fin_p]
    red_f, red_s, red_l, red_p = [jnp.where(my_c == 0, jnp.stack([f, s]), jnp.stack([s, f])) for f, s in zip(fins, _join_halves(fins))]
    small_sum = _allreduce_small(_pack_small([g_small[n] for n, _ in SMALL]))

    grads, delta, new_m, new_v = {}, {}, {}, {}
    reduced = {}
    for group, red in ((("ffn1_w_gu",), red_f), (SECOND, red_s), (LATE, red_l)):
        off = 0
        for n in group:
            reduced[n] = red[:, off:off + half[n], :].reshape(rows_of[n] // 4, D)
            off += half[n]
    for n, rows, tr in BIG:
        g = reduced[n]
        g = (g.T if tr else g)[None]
        grads[n] = g
        d_, m_, v2_ = _adamw("adamw_" + n, w[n][0], g[0], mom[n][0], var[n][0])
        delta[n], new_m[n], new_v[n] = d_[None], m_[None], v2_[None]
    sh = ("w_up", "a_up", "g_up", "conv_dw")
    g_pack = red_p.reshape(128, PACK_W)
    d_, m_, v2_ = _adamw("adamw_pack", pack_local, g_pack, _pack_sharded(*[mom[n] for n in sh]), _pack_sharded(*[var[n] for n in sh]))
    for dst, src in ((grads, g_pack), (delta, d_), (new_m, m_), (new_v, v2_)):
        for n, a in zip(sh, _unpack_sharded(src)):
            dst[n] = a
    sm = [n for n, _ in SMALL]
    d_, m_, v2_ = _adamw("adamw_small", _pack_small([w[n] for n in sm]), small_sum, _pack_small([mom[n] for n in sm]),
                         _pack_small([var[n] for n in sm]))
    like = [w[n] for n in sm]
    for dst, src in ((grads, small_sum), (delta, d_), (new_m, m_), (new_v, v2_)):
        for n, a in zip(sm, _unpack_small(src, like)):
            dst[n] = a

    wn = names
    return (loss, grad_x[None], *[grads[n] for n in wn], *[delta[n] for n in wn], *[new_m[n] for n in wn],
            *[new_v[n] for n in wn])
```

```python
import functools
import math

import jax
import jax.numpy as jnp
from jax import lax
from jax.experimental import pallas as pl
from jax.experimental.pallas import tpu as pltpu

F32 = jnp.float32
BF16 = jnp.bfloat16

D = 1024
DFF = 2816
HID_TILE = DFF // 2
DR = 512
HS = 64
D_SHIFT = 1792
D_IN = 2816
CW = 31
RMS_EPS = 1e-6
GN_EPS = 64e-5
LN_EPS = 1e-5
DECAY_SCALE = math.exp(-0.5)
ADAM_LR, ADAM_B1, ADAM_B2, ADAM_EPS, ADAM_WD, ADAM_STEP = 0.001, 0.9, 0.999, 1e-8, 0.01, 10

REC_TILE = 128
VMEM_LIMIT = 56 * 1024 * 1024
MESH = pl.DeviceIdType.MESH
ANY = pl.BlockSpec(memory_space=pl.ANY)

BIG = (("ffn1_w_gu", 2 * DFF, True), ("ffn1_w_down", DFF, False), ("w_in", D_IN, True),
       ("w_out", D, False), ("ffn2_w_gu", 2 * DFF, True), ("ffn2_w_down", DFF, False))
SECOND = ("ffn1_w_down", "w_in")
LATE = ("w_out", "ffn2_w_gu", "ffn2_w_down")
PACK_W = 384
SMALL = (("ffn1_norm_pre", D), ("ffn1_norm_post", D), ("mix_norm_pre", D), ("mix_norm_post", D),
         ("shift_mu", D_SHIFT), ("w0", DR), ("a0", DR), ("k_k", DR), ("k_a", DR), ("r_k", DR),
         ("gn_w", DR), ("gn_b", DR), ("conv_b", DR), ("conv_ln_w", DR), ("conv_ln_b", DR),
         ("ffn2_norm_pre", D), ("ffn2_norm_post", D))
SMALL_N = sum(n for _, n in SMALL)
SMALL_PAD = 8 * 1664


def _params(sem):
    return pltpu.CompilerParams(dimension_semantics=sem, vmem_limit_bytes=VMEM_LIMIT)


def _tile(n, prefs):
    for p in prefs:
        if n % p == 0:
            return p
    return n


def _rows_call(name, fn, rows, consts, row_outs, acc_outs, tm):
    specs, arrs = [], []
    for r in rows:
        if isinstance(r, tuple):
            a, bs, im = r
            specs.append(pl.BlockSpec(bs, im))
        else:
            a = r
            specs.append(pl.BlockSpec((tm, a.shape[1]), lambda i: (i, 0)))
        arrs.append(a)
    t_rows = arrs[0].shape[0]
    for c in consts:
        specs.append(pl.BlockSpec(c.shape, functools.partial(lambda i, n: (0,) * n, n=c.ndim)))
        arrs.append(c)
    n_in, n_o, n_a = len(arrs), len(row_outs), len(acc_outs)

    def kern(*refs):
        i = pl.program_id(0)
        vals = [r[...] for r in refs[:n_in]]
        ro, ao = fn(i, *vals)
        outs = refs[n_in:]
        for k in range(n_o):
            outs[k][...] = ro[k].astype(outs[k].dtype)
        if n_a:
            @pl.when(i == 0)
            def _():
                for k in range(n_a):
                    outs[n_o + k][...] = jnp.zeros(outs[n_o + k].shape, F32)
            for k in range(n_a):
                outs[n_o + k][...] += ao[k]

    out_specs = [pl.BlockSpec((tm, w), lambda i: (i, 0)) for (w, _) in row_outs]
    out_specs += [pl.BlockSpec(s, functools.partial(lambda i, n: (0,) * n, n=len(s))) for s in acc_outs]
    out_shape = [jax.ShapeDtypeStruct((t_rows, w), dt) for (w, dt) in row_outs]
    out_shape += [jax.ShapeDtypeStruct(s, F32) for s in acc_outs]
    return pl.pallas_call(kern, grid=(t_rows // tm,), in_specs=specs, out_specs=out_specs, out_shape=out_shape,
                          name=name, compiler_params=_params(("arbitrary",)))(*arrs)


_DIMS = {"nn": (((1,), (0,)), ((), ())), "nt": (((1,), (1,)), ((), ())), "tn": (((0,), (0,)), ((), ()))}
_LANE_TILES = (1408, 1024, 512, 384, 256, 128)


def _matmul(name, a, b, mode, out_dtype=F32, owners=None, gather=None):
    if mode == "nn":
        (m, k), (_, n) = a.shape, b.shape
    elif mode == "nt":
        (m, k), (n, _) = a.shape, b.shape
    else:
        (k, m), (_, n) = a.shape, b.shape
    if mode == "tn":
        tm, tk = _tile(m, _LANE_TILES), _tile(k, (512, 256, 128))
    else:
        tm, tk = _tile(m, (1024, 512, 256, 128)), _tile(k, _LANE_TILES)
    tn = _tile(n, _LANE_TILES)
    nk = k // tk
    assert out_dtype == F32 or nk == 1

    owners, gather = list(owners or ()), list(gather or ())
    assert not (owners and gather)
    riders = owners + gather
    ns = len(riders)
    grid = (m // tm, n // tn, nk)
    steps = grid[0] * grid[1] * nk

    def kern(a_ref, b_ref, *rest):
        o_ref = rest[ns]
        if ns:
            step = (pl.program_id(0) * grid[1] + pl.program_id(1)) * nk + pl.program_id(2)
            if owners:
                start, finish = _owners_plan(rest[:ns], rest[ns + 1:2 * ns + 1], *rest[2 * ns + 1:])
                pl.when(step == 0)(start)
            else:
                start, middle, finish = _gather_plan(rest[:ns], rest[ns + 1:2 * ns + 1], *rest[2 * ns + 1:])
                pl.when(step == 0)(start)
                pl.when(step == steps // 2)(middle)

        def part():
            return lax.dot_general(a_ref[...].astype(BF16), b_ref[...].astype(BF16), _DIMS[mode], preferred_element_type=F32)

        if nk == 1:
            o_ref[...] = part().astype(o_ref.dtype)
        else:
            @pl.when(pl.program_id(2) == 0)
            def _():
                o_ref[...] = jnp.zeros(o_ref.shape, F32)

            o_ref[...] += part()
        if ns:
            pl.when(step == steps - 1)(finish)

    a_spec = pl.BlockSpec((tk, tm), lambda i, j, q: (q, i)) if mode == "tn" else pl.BlockSpec((tm, tk), lambda i, j, q: (i, q))
    b_spec = pl.BlockSpec((tn, tk), lambda i, j, q: (j, q)) if mode == "nt" else pl.BlockSpec((tk, tn), lambda i, j, q: (q, j))
    extra_shapes = _owner_shapes(owners) if owners else _gathered_shapes(gather)
    extra_sems = [] if not ns else _owner_sems(ns) if owners else _gather_sems(ns)
    out, *arrived = pl.pallas_call(
        kern, grid=grid, in_specs=[a_spec, b_spec] + [ANY] * ns,
        out_specs=[pl.BlockSpec((tm, tn), lambda i, j, q: (i, j))] + [ANY] * ns,
        out_shape=[jax.ShapeDtypeStruct((m, n), out_dtype)] + extra_shapes,
        scratch_shapes=extra_sems, name=name,
        compiler_params=_params(("arbitrary", "arbitrary", "arbitrary")))(a, b, *riders)
    if gather:
        arrived = _fill_own(arrived, gather)
    return (out, arrived) if ns else out


def _rms(x, g):
    return x * lax.rsqrt(jnp.mean(x * x, axis=-1, keepdims=True) + RMS_EPS) * g


def _silu(x):
    return x * jax.nn.sigmoid(x)


def _bones(n):
    r = lax.broadcasted_iota(jnp.int32, (n, n), 0) // HS
    c = lax.broadcasted_iota(jnp.int32, (n, n), 1) // HS
    return (r == c).astype(BF16)


@jax.custom_vjp
def _segsum(x):
    bones = _bones(x.shape[1])
    hi = x.astype(BF16)
    lo = (x - hi.astype(F32)).astype(BF16)
    return jnp.dot(hi, bones, preferred_element_type=F32) + jnp.dot(lo, bones, preferred_element_type=F32)


_segsum.defvjp(lambda x: (_segsum(x), None), lambda _, ct: (_segsum(ct),))


@jax.custom_vjp
def _dot_nt(x, w):
    return lax.dot_general(x.astype(BF16), w.astype(BF16), _DIMS["nt"], preferred_element_type=F32)


def _dot_nt_bwd(res, ct):
    x, w = res
    ctb = ct.astype(BF16)
    dx = lax.dot_general(ctb, w.astype(BF16), _DIMS["nn"], preferred_element_type=F32)
    dw = lax.dot_general(ctb, x.astype(BF16), _DIMS["tn"], preferred_element_type=F32)
    return dx, dw


_dot_nt.defvjp(lambda x, w: (_dot_nt(x, w), (x, w)), _dot_nt_bwd)


def _prep(ps, w0, a0, k_k, k_a, p01, p2):
    r, k, v = ps[:, :DR], ps[:, DR:2 * DR], ps[:, 2 * DR:3 * DR]
    wa, xg = ps[:, 3 * DR:3 * DR + 128], ps[:, 3 * DR + 128:]
    first = lax.broadcasted_iota(jnp.int32, (1, 128), 1) < 64
    d = w0 + _dot_nt(jnp.where(first, jnp.tanh(wa), 0.0), p01)
    decay = jnp.exp(-DECAY_SCALE * jax.nn.sigmoid(d))
    a = jax.nn.sigmoid(a0 + _dot_nt(jnp.where(first, 0.0, wa), p01))
    g = _dot_nt(jax.nn.sigmoid(xg), p2)
    kk = k * k_k
    kk = kk * lax.rsqrt(jnp.maximum(_segsum(kk * kk), 1e-12))
    k2 = k * (1.0 + (a - 1.0) * k_a)
    return r, decay, k2, v, -kk, kk * a, g


def _post(y, r, k, v, g, gn_w, gn_b, r_k):
    mu = _segsum(y) * (1.0 / HS)
    yc = y - mu
    var = _segsum(yc * yc) * (1.0 / HS)
    yn = yc * lax.rsqrt(var + GN_EPS) * gn_w + gn_b
    return (yn + _segsum(r * k * r_k) * v) * g


def _ln_silu(c, w, b):
    mu = jnp.mean(c, axis=-1, keepdims=True)
    var = jnp.mean(jnp.square(c - mu), axis=-1, keepdims=True)
    return _silu((c - mu) * lax.rsqrt(var + LN_EPS) * w + b)


def _colsum(x):
    return jnp.sum(x, axis=0, keepdims=True)


def _gu_swiglu(name, h, w_gu_t, gather=None):
    m = h.shape[0]
    tm, tn = _tile(m, (512, 256, 128)), HID_TILE
    nj = DFF // tn
    gather = list(gather or ())
    ns = len(gather)
    steps = (m // tm) * nj

    def kern(a_ref, bg_ref, bu_ref, *rest):
        gate_ref, up_ref, act_ref = rest[ns:ns + 3]
        if ns:
            step = pl.program_id(0) * nj + pl.program_id(1)
            start, middle, finish = _gather_plan(rest[:ns], rest[ns + 3:2 * ns + 3], *rest[2 * ns + 3:])
            pl.when(step == 0)(start)
            pl.when(step == steps // 2)(middle)
        a = a_ref[...]
        gate = lax.dot_general(a, bg_ref[...], _DIMS["nt"], preferred_element_type=F32)
        up = lax.dot_general(a, bu_ref[...], _DIMS["nt"], preferred_element_type=F32)
        gate_ref[...] = gate
        up_ref[...] = up
        act_ref[...] = (_silu(gate) * up).astype(BF16)
        if ns:
            pl.when(step == steps - 1)(finish)

    tile = pl.BlockSpec((tm, tn), lambda i, j: (i, j))
    gate, up, act, *got = pl.pallas_call(
        kern, grid=(m // tm, nj),
        in_specs=[pl.BlockSpec((tm, D), lambda i, j: (i, 0)), pl.BlockSpec((tn, D), lambda i, j: (j, 0)),
                  pl.BlockSpec((tn, D), lambda i, j: (j + nj, 0))] + [ANY] * ns,
        out_specs=[tile] * 3 + [ANY] * ns,
        out_shape=[jax.ShapeDtypeStruct((m, DFF), F32)] * 2 + [jax.ShapeDtypeStruct((m, DFF), BF16)] + _gathered_shapes(gather),
        scratch_shapes=_gather_sems(ns) if ns else [], name=name,
        compiler_params=_params(("arbitrary", "arbitrary")))(h, w_gu_t, w_gu_t, *gather)
    return gate, up, act, _fill_own(got, gather)


def _dact_swiglu_b(name, df, w_down, gate, up):
    m = df.shape[0]
    tm, tn = _tile(m, (512, 256, 128)), HID_TILE
    nj = DFF // tn

    def kern(df_ref, wd_ref, gate_ref, up_ref, o_ref):
        dact = lax.dot_general(df_ref[...], wd_ref[...], _DIMS["nt"], preferred_element_type=F32)
        _, vjp = jax.vjp(lambda a, b: _silu(a) * b, gate_ref[...], up_ref[...])
        dgate, dup = vjp(dact)
        for j in range(nj):
            @pl.when(pl.program_id(1) == j)
            def _(j=j):
                o_ref[:, tn * j:tn * (j + 1)] = dgate.astype(BF16)
                o_ref[:, DFF + tn * j:DFF + tn * (j + 1)] = dup.astype(BF16)

    tile = pl.BlockSpec((tm, tn), lambda i, j: (i, j))
    return pl.pallas_call(
        kern, grid=(m // tm, nj),
        in_specs=[pl.BlockSpec((tm, D), lambda i, j: (i, 0)), pl.BlockSpec((tn, D), lambda i, j: (j, 0)), tile, tile],
        out_specs=pl.BlockSpec((tm, 2 * DFF), lambda i, j: (i, 0)),
        out_shape=jax.ShapeDtypeStruct((m, 2 * DFF), BF16), name=name,
        compiler_params=_params(("arbitrary", "arbitrary")))(df, w_down, gate, up)


def _ffn_fwd(tag, x, pre, w_gu_t, w_down, tm, gather=None, h=None):
    if h is None:
        (h,) = _rows_call(tag + "_norm", lambda i, xv, g: ((_rms(xv, g),), ()), [x], [pre], [(D, BF16)], [], tm)
    gate, up, act, gathered = _gu_swiglu(tag + "_gu", h, w_gu_t, gather)
    if gather:
        w_down = gathered[0].reshape(DFF, D)
    f = _matmul(tag + "_down", act, w_down, "nn")
    return h, (gate, up), act, f, gathered


def _norm_resid_b(name, x, dh, dxo, pre, tm, below=None):
    def fn(i, *vals):
        xv, dhv, dv = vals[:3]
        _, vjp = jax.vjp(_rms, xv, vals[-1] if below is None else vals[-2])
        dx, dg = vjp(dhv)
        dx = dx + dv
        if below is None:
            return (dx,), (dg,)
        _, vjp = jax.vjp(lambda a, b: below[2] * _rms(a, b), vals[3], vals[-1])
        df, dgp = vjp(dx)
        return (dx, df), (dg, dgp)
    if below is None:
        return _rows_call(name, fn, [x, dh, dxo], [pre], [(D, F32)], [(1, D)], tm)
    return _rows_call(name, fn, [x, dh, dxo, below[0]], [pre, below[1]], [(D, F32), (D, BF16)], [(1, D)] * 2, tm)


def _ffn_bwd(tag, x, pre, h, gu, act, dxo, df, w_gu_t, w_down, tm, after_down=None, after_gu=None, below=None):
    dgu = _dact_swiglu_b(tag + "_dact", df, w_down, *gu)
    dw_down = _matmul(tag + "_dwdown", act, df, "tn")
    sums = after_down(dw_down) if after_down else []
    dw_gu_t = _matmul(tag + "_dwgu", dgu, h, "tn", owners=sums)
    dw_gu_t, parts_down = dw_gu_t if sums else (dw_gu_t, [])
    sums = after_gu(dw_gu_t) if after_gu else []
    dh = _matmul(tag + "_dh", dgu, w_gu_t, "nn", owners=sums)
    dh, parts_gu = dh if sums else (dh, [])
    parts = (parts_down, parts_gu)
    return _norm_resid_b(tag + "_norm_b", x, dh, dxo, pre, tm, below), dw_gu_t, dw_down, parts


def _pair_bcast(cols, first):
    return jnp.concatenate([jnp.where(first, cols[2 * p], cols[2 * p + 1]) for p in range(4)], axis=1)


def _head_sums(x, first):
    cols = []
    for p in range(4):
        xp = x[:, 128 * p:128 * (p + 1)]
        cols.append(jnp.sum(jnp.where(first, xp, 0.0), axis=1, keepdims=True))
        cols.append(jnp.sum(jnp.where(first, 0.0, xp), axis=1, keepdims=True))
    return cols


def _split16(x8):
    hi = x8.astype(BF16).astype(F32)
    return jnp.concatenate([hi, x8 - hi], axis=0).astype(BF16)


def _cols8(x8, e16):
    return lax.dot_general(_split16(x8), e16, _DIMS["tn"], preferred_element_type=F32)


def _pair_rows(c, first):
    return jnp.concatenate([jnp.where(first, c[128 * p:128 * p + HS], c[128 * p + HS:128 * (p + 1)]) for p in range(4)], axis=1)


def _rows8(prod, bones, dmask):
    x = prod.astype(BF16)
    full = jnp.concatenate([jnp.dot(x[:, 256 * q:256 * (q + 1)], bones, preferred_element_type=F32) for q in range(2)], axis=1)
    return jnp.concatenate([_colsum(full[HS * j:HS * (j + 1)] * dmask) for j in range(8)], axis=0)


def _rec_step(s, wr, zr, br, kr, vc, first):
    u = _pair_bcast(_head_sums(s * zr, first), first)
    return s * wr + u * br + vc * kr, u


def _rec_consts():
    e16 = (lax.broadcasted_iota(jnp.int32, (16, 1024), 0) % 8 == lax.broadcasted_iota(jnp.int32, (16, 1024), 1) // 128)
    dmask = lax.broadcasted_iota(jnp.int32, (HS, DR), 0) == lax.broadcasted_iota(jnp.int32, (HS, DR), 1) % HS
    return e16.astype(BF16), _bones(256), dmask.astype(F32)


def _const_spec(a):
    return pl.BlockSpec(a.shape, functools.partial(lambda i, n: (0,) * n, n=a.ndim))


def _rec_fwd(r, w, k, z, b, v, shards):
    t_len = r.shape[0]
    nt = t_len // REC_TILE
    ns = len(shards)
    consts = _rec_consts()

    def kern(r_ref, w_ref, k_ref, z_ref, b_ref, v_ref, e_ref, bones_ref, dm_ref, *rest):
        y_ref, ck_ref, states, u_ref = rest[ns:ns + 4]
        s_ref, prod = rest[2 * ns + 4:2 * ns + 6]
        start, middle, finish = _gather_plan(rest[:ns], rest[ns + 4:2 * ns + 4], *rest[2 * ns + 6:])
        i = pl.program_id(0)

        @pl.when(i == 0)
        def _():
            s_ref[...] = jnp.zeros(s_ref.shape, F32)
            start()

        pl.when(i == nt // 2)(middle)
        ck_ref[0] = s_ref[...]
        first = lax.broadcasted_iota(jnp.int32, (1, 128), 1) < HS

        def group(g8, s):
            base = pl.multiple_of(g8 * 8, 8)
            r8, w8, k8, z8, b8, v8 = (ref[pl.ds(base, 8), :] for ref in (r_ref, w_ref, k_ref, z_ref, b_ref, v_ref))
            vcols = _cols8(v8, e_ref[...])
            urows = []
            for j in range(8):
                vc = _pair_rows(vcols[:, 128 * j:128 * (j + 1)], first)
                s, u = _rec_step(s, w8[j:j + 1], z8[j:j + 1], b8[j:j + 1], k8[j:j + 1], vc, first)
                states[base + j] = s
                urows.append(_colsum(u * dm_ref[...]))
                prod[HS * j:HS * (j + 1), :] = s * r8[j:j + 1]
            y_ref[pl.ds(base, 8), :] = _rows8(prod[...], bones_ref[...], dm_ref[...])
            u_ref[pl.ds(base, 8), :] = jnp.concatenate(urows, axis=0)
            return s

        s_ref[...] = lax.fori_loop(0, REC_TILE // 8, group, s_ref[...])
        pl.when(i == nt - 1)(finish)

    row = pl.BlockSpec((REC_TILE, DR), lambda i: (i, 0))
    y, ck, states, u, *got = pl.pallas_call(
        kern, grid=(nt,), in_specs=[row] * 6 + [_const_spec(c) for c in consts] + [ANY] * ns,
        out_specs=[row, pl.BlockSpec((1, HS, DR), lambda i: (i, 0, 0)), pl.BlockSpec((REC_TILE, HS, DR), lambda i: (i, 0, 0)), row]
        + [ANY] * ns,
        out_shape=[jax.ShapeDtypeStruct((t_len, DR), F32), jax.ShapeDtypeStruct((nt, HS, DR), F32),
                   jax.ShapeDtypeStruct((t_len, HS, DR), F32), jax.ShapeDtypeStruct((t_len, DR), F32)] + _gathered_shapes(shards),
        scratch_shapes=[pltpu.VMEM((HS, DR), F32), pltpu.VMEM((8 * HS, DR), F32)] + _gather_sems(ns), name="rec_fwd",
        compiler_params=_params(("arbitrary",)))(r, w, k, z, b, v, *consts, *shards)
    return y, (ck, states, u), _fill_own(got, shards)


def _rec_bwd(r, w, k, z, b, v, dy, saved, sums):
    t_len = r.shape[0]
    nt = t_len // REC_TILE
    ns = len(sums)
    consts = _rec_consts()

    def kern(r_ref, w_ref, k_ref, z_ref, b_ref, v_ref, dy_ref, u_ref, ck_ref, states, e_ref, bones_ref, dm_ref, *rest):
        dr_ref, dw_ref, dk_ref, dz_ref, db_ref, dv_ref = rest[ns:ns + 6]
        ds_ref, prod = rest[2 * ns + 6:2 * ns + 8]
        start, finish = _owners_plan(rest[:ns], rest[ns + 6:2 * ns + 6], *rest[2 * ns + 8:])
        i = pl.program_id(0)

        @pl.when(i == 0)
        def _():
            ds_ref[...] = jnp.zeros(ds_ref.shape, F32)
            start()

        first = lax.broadcasted_iota(jnp.int32, (1, 128), 1) < HS

        def bgroup(gg, ds):
            base = pl.multiple_of((REC_TILE // 8 - 1 - gg) * 8, 8)
            r8, w8, k8, z8, b8, v8, dy8, u8 = (ref[pl.ds(base, 8), :]
                                               for ref in (r_ref, w_ref, k_ref, z_ref, b_ref, v_ref, dy_ref, u_ref))
            vcols = _cols8(v8, e_ref[...])
            dycols = _cols8(dy8, e_ref[...])
            ucols = _cols8(u8, e_ref[...])
            before = jnp.where(base == 0, ck_ref[0], states[jnp.maximum(base - 1, 0)])
            rows = {n: [None] * 8 for n in ("dr", "dw", "dk", "dz", "db")}
            for j in range(7, -1, -1):
                t = base + j
                rr, wr, kr, zr, br = (x[j:j + 1] for x in (r8, w8, k8, z8, b8))
                s_prev, s_t = (states[t - 1] if j else before), states[t]
                dyc = _pair_rows(dycols[:, 128 * j:128 * (j + 1)], first)
                vc = _pair_rows(vcols[:, 128 * j:128 * (j + 1)], first)
                ds = ds + dyc * rr
                rows["dr"][j] = _colsum(s_t * dyc)
                rows["dw"][j] = _colsum(ds * s_prev)
                du = _pair_bcast(_head_sums(ds * br, first), first)
                rows["db"][j] = _colsum(ds * _pair_rows(ucols[:, 128 * j:128 * (j + 1)], first))
                rows["dk"][j] = _colsum(ds * vc)
                prod[HS * j:HS * (j + 1), :] = ds * kr
                rows["dz"][j] = _colsum(s_prev * du)
                ds = ds * wr + du * zr
            for n, ref in (("dr", dr_ref), ("dw", dw_ref), ("dk", dk_ref), ("dz", dz_ref), ("db", db_ref)):
                ref[pl.ds(base, 8), :] = jnp.concatenate(rows[n], axis=0)
            dv_ref[pl.ds(base, 8), :] = _rows8(prod[...], bones_ref[...], dm_ref[...])
            return ds

        ds_ref[...] = lax.fori_loop(0, REC_TILE // 8, bgroup, ds_ref[...])
        pl.when(i == nt - 1)(finish)

    ck, states, u = saved
    row = pl.BlockSpec((REC_TILE, DR), lambda i: (nt - 1 - i, 0))
    outs = pl.pallas_call(
        kern, grid=(nt,),
        in_specs=[row] * 8 + [pl.BlockSpec((1, HS, DR), lambda i: (nt - 1 - i, 0, 0)),
                              pl.BlockSpec((REC_TILE, HS, DR), lambda i: (nt - 1 - i, 0, 0))]
        + [_const_spec(c) for c in consts] + [ANY] * ns,
        out_specs=[row] * 6 + [ANY] * ns, out_shape=[jax.ShapeDtypeStruct((t_len, DR), F32)] * 6 + _owner_shapes(sums),
        scratch_shapes=[pltpu.VMEM((HS, DR), F32), pltpu.VMEM((8 * HS, DR), F32)] + _owner_sems(ns), name="rec_bwd",
        compiler_params=_params(("arbitrary",)))(r, w, k, z, b, v, dy, u, ck, states, *consts, *sums)
    return outs[:6], outs[6:]


def _prev_rows(a, tm, n):
    return (a, (n, a.shape[1]), lambda i: (jnp.maximum(i * (tm // n) - 1, 0), 0))


def _next_rows(a, tm, n):
    last = a.shape[0] // n - 1
    return (a, (n, a.shape[1]), lambda i: (jnp.minimum((i + 1) * (tm // n), last), 0))


def _shifted(x, prev8, i):
    rowid = lax.broadcasted_iota(jnp.int32, x.shape, 0)
    before = jnp.where(i == 0, 0.0, prev8[7:8, :])
    return jnp.where(rowid == 0, before, pltpu.roll(x, 1, 0))


def _shift_fwd(p, mu, tm):
    def fn(i, pv, prev8, muv):
        x = pv[:, :D_SHIFT]
        return (x + (_shifted(x, prev8[:, :D_SHIFT], i) - x) * muv,), ()
    return _rows_call("shift", fn, [p, _prev_rows(p, tm, 8)], [mu], [(D_SHIFT, F32)], [], tm)[0]


def _shift_bwd(dps, p, dpc, mu, tm):
    n_tiles = p.shape[0] // tm

    def fn(i, dv, next8, pv, prev8, dpcv, muv):
        x = pv[:, :D_SHIFT]
        xs = _shifted(x, prev8[:, :D_SHIFT], i)
        rowid = lax.broadcasted_iota(jnp.int32, dv.shape, 0)
        after = jnp.where(i == n_tiles - 1, 0.0, next8[0:1, :])
        dnext = jnp.where(rowid == tm - 1, after, pltpu.roll(dv, tm - 1, 0))
        dp_s = dv * (1.0 - muv) + dnext * muv
        return (jnp.concatenate([dp_s.astype(BF16), dpcv], axis=1),), (_colsum(dv * (xs - x)),)
    return _rows_call("shift_b", fn, [dps, _next_rows(dps, tm, 8), p, _prev_rows(p, tm, 8), dpc], [mu],
                      [(D_IN, BF16)], [(1, D_SHIFT)], tm)


def _glu(pc):
    return pc[:, :DR] * jax.nn.sigmoid(pc[:, DR:])


def _shift_copies(ext, shifted, tm):
    for s in range(1, 8):
        shifted[s - 1] = ext[s:s + tm + 24, :]


def _window(ext, shifted, off, tm):
    if off % 8 == 0:
        return ext[off:off + tm, :]
    return shifted[off % 8 - 1, off // 8 * 8:off // 8 * 8 + tm, :]


def _conv_fwd(p, dw32, cb, lw, lb, tm):
    t_len = p.shape[0]

    def kern(p_ref, ph_ref, dw_ref, cb_ref, lw_ref, lb_ref, glu_ref, c_ref, ob_ref, ext, shifted):
        i = pl.program_id(0)
        glu = _glu(p_ref[:, D_SHIFT:])
        ext[0:32, :] = jnp.where(i == 0, 0.0, _glu(ph_ref[:, D_SHIFT:]))
        ext[32:, :] = glu
        _shift_copies(ext, shifted, tm)
        acc = jnp.zeros((tm, DR), F32)
        for j in range(CW):
            acc = acc + _window(ext, shifted, 2 + j, tm) * dw_ref[j:j + 1, :]
        c = acc + cb_ref[...]
        glu_ref[...] = glu
        c_ref[...] = c
        ob_ref[...] = _ln_silu(c, lw_ref[...], lb_ref[...]).astype(BF16)

    tile = lambda w: pl.BlockSpec((tm, w), lambda i: (i, 0))
    const = lambda a: pl.BlockSpec(a.shape, lambda i: (0, 0))
    halo = pl.BlockSpec((32, D_IN), lambda i: (jnp.maximum(i * (tm // 32) - 1, 0), 0))
    return pl.pallas_call(
        kern, grid=(t_len // tm,), in_specs=[tile(D_IN), halo, const(dw32), const(cb), const(lw), const(lb)],
        out_specs=[tile(DR)] * 3,
        out_shape=[jax.ShapeDtypeStruct((t_len, DR), F32)] * 2 + [jax.ShapeDtypeStruct((t_len, DR), BF16)],
        scratch_shapes=[pltpu.VMEM((tm + 32, DR), F32), pltpu.VMEM((7, tm + 24, DR), F32)], name="conv_fwd",
        compiler_params=_params(("arbitrary",)))(p, p, dw32, cb, lw, lb)


def _conv_bwd(dc, glu, p, dw32, tm):
    t_len = p.shape[0]
    n_tiles = t_len // tm

    def kern(dc_ref, dcn_ref, glu_ref, gluh_ref, p_ref, dw_ref, dpc_ref, ddw_ref, ext_d, ext_g, shifted_d, shifted_g):
        i = pl.program_id(0)

        @pl.when(i == 0)
        def _():
            ddw_ref[...] = jnp.zeros(ddw_ref.shape, F32)

        dcv = dc_ref[...]
        ext_d[0:tm, :] = dcv
        ext_d[tm:, :] = jnp.where(i == n_tiles - 1, 0.0, dcn_ref[...])
        ext_g[0:32, :] = jnp.where(i == 0, 0.0, gluh_ref[...])
        ext_g[32:, :] = glu_ref[...]
        _shift_copies(ext_d, shifted_d, tm)
        _shift_copies(ext_g, shifted_g, tm)
        dglu = jnp.zeros((tm, DR), F32)
        for j in range(CW):
            dglu = dglu + _window(ext_d, shifted_d, 30 - j, tm) * dw_ref[j:j + 1, :]
            ddw_ref[j:j + 1, :] += _colsum(dcv * _window(ext_g, shifted_g, 2 + j, tm))
        pc = p_ref[:, D_SHIFT:]
        sg = jax.nn.sigmoid(pc[:, DR:])
        dpc_ref[...] = jnp.concatenate([dglu * sg, dglu * pc[:, :DR] * sg * (1.0 - sg)], axis=1).astype(BF16)

    tile = lambda w: pl.BlockSpec((tm, w), lambda i: (i, 0))
    nxt = pl.BlockSpec((32, DR), lambda i: (jnp.minimum((i + 1) * (tm // 32), t_len // 32 - 1), 0))
    prv = pl.BlockSpec((32, DR), lambda i: (jnp.maximum(i * (tm // 32) - 1, 0), 0))
    return pl.pallas_call(
        kern, grid=(n_tiles,), in_specs=[tile(DR), nxt, tile(DR), prv, tile(D_IN), pl.BlockSpec((32, DR), lambda i: (0, 0))],
        out_specs=[tile(D), pl.BlockSpec((32, DR), lambda i: (0, 0))],
        out_shape=[jax.ShapeDtypeStruct((t_len, D), BF16), jax.ShapeDtypeStruct((32, DR), F32)],
        scratch_shapes=[pltpu.VMEM((tm + 32, DR), F32)] * 2 + [pltpu.VMEM((7, tm + 24, DR), F32)] * 2, name="conv_bwd",
        compiler_params=_params(("arbitrary",)))(dc, dc, glu, glu, p, dw32)


def _place():
    x, y, c = lax.axis_index("x"), lax.axis_index("y"), lax.axis_index("c")
    chips = [(1 - x, y), (x, 1 - y), (1 - x, 1 - y)]
    return x, y, c, chips


def _gather_weights(shards):
    n = len(shards)

    def body(*refs):
        start, middle, finish = _gather_plan(refs[:n], refs[n:2 * n], *refs[2 * n:])
        start()
        middle()
        finish()

    got = pl.pallas_call(body, in_specs=[ANY] * n, out_specs=[ANY] * n, out_shape=_gathered_shapes(shards),
                         scratch_shapes=_gather_sems(n), name="gather_weights")(*shards)
    return _fill_own(got, shards)


def _gathered_shapes(shards):
    return [jax.ShapeDtypeStruct((4,) + s.shape, s.dtype) for s in shards]


def _gather_sems(n):
    return [pltpu.SemaphoreType.DMA((6 * n,)), pltpu.SemaphoreType.DMA((6 * n,))]


def _fill_own(got, shards):
    me = 2 * lax.axis_index("x") + lax.axis_index("y")
    return [lax.dynamic_update_slice(g, s[None], (me, 0, 0, 0)) for g, s in zip(got, shards)]


def _gather_plan(src, dst, send, recv):
    n = len(src)
    x, y, c, chips = _place()
    me, sib = 2 * x + y, (x, y, 1 - c)

    def rcopy(k, sem, s_ref, d_ref, to):
        return pltpu.make_async_remote_copy(src_ref=s_ref, dst_ref=d_ref, send_sem=send.at[6 * k + sem],
                                            recv_sem=recv.at[6 * k + sem], device_id=to, device_id_type=MESH)

    def landed(k, m, half):
        return dst[k].at[2 * chips[m][0] + chips[m][1], half]

    first = [rcopy(k, m, src[k].at[c], dst[k].at[me, c], (*chips[m], c)) for k in range(n) for m in range(3)]
    passed = [rcopy(k, 3 + m, landed(k, m, c), landed(k, m, c), sib) for k in range(n) for m in range(3)]

    def start():
        for cp in first:
            cp.start()

    def middle():
        for k in range(n):
            for m in range(3):
                rcopy(k, m, landed(k, m, c), landed(k, m, c), sib).wait_recv()
                passed[3 * k + m].start()

    def finish():
        for k in range(n):
            for m in range(3):
                rcopy(k, 3 + m, landed(k, m, 1 - c), landed(k, m, 1 - c), sib).wait_recv()
        for cp in first + passed:
            cp.wait_send()

    return start, middle, finish


def _swap_halves(name, give):
    n = len(give)

    def body(*refs):
        src, got = refs[:n], refs[n:2 * n]
        send, recv = refs[2 * n:]
        x, y, c, _ = _place()
        copies = []
        for k in range(n):
            for j in range(4):
                copies.append(pltpu.make_async_remote_copy(
                    src_ref=src[k].at[j], dst_ref=got[k].at[j], send_sem=send.at[4 * k + j], recv_sem=recv.at[4 * k + j],
                    device_id=(x, y, 1 - c), device_id_type=MESH))
                copies[-1].start()
        for cp in copies:
            cp.wait()

    out_shape = [jax.ShapeDtypeStruct(g.shape, g.dtype) for g in give]
    return pl.pallas_call(body, in_specs=[ANY] * n, out_specs=[ANY] * n, out_shape=out_shape,
                          scratch_shapes=[pltpu.SemaphoreType.DMA((4 * n,)), pltpu.SemaphoreType.DMA((4 * n,))],
                          name=name)(*give)


def _owner_shapes(sums):
    return [jax.ShapeDtypeStruct((3,) + s.shape[1:], s.dtype) for s in sums]


def _owner_sems(n):
    return [pltpu.SemaphoreType.DMA((3 * n,)), pltpu.SemaphoreType.DMA((3 * n,))]


def _owners_plan(src, dst, send, recv):
    x, y, c, chips = _place()
    copies = [pltpu.make_async_remote_copy(
        src_ref=src[k].at[2 * chip[0] + chip[1]], dst_ref=dst[k].at[m], send_sem=send.at[3 * k + m],
        recv_sem=recv.at[3 * k + m], device_id=(*chip, c), device_id_type=MESH)
        for k in range(len(src)) for m, chip in enumerate(chips)]

    def start():
        for cp in copies:
            cp.start()

    def finish():
        for cp in copies:
            cp.wait()

    return start, finish


def _join_halves(halves):
    n = len(halves)

    def body(*refs):
        src, dst = refs[:n], refs[n:2 * n]
        send, recv = refs[2 * n:]
        x, y, c, _ = _place()
        copies = []
        for k in range(n):
            copies.append(pltpu.make_async_remote_copy(src_ref=src[k], dst_ref=dst[k], send_sem=send.at[k],
                                                       recv_sem=recv.at[k], device_id=(x, y, 1 - c), device_id_type=MESH))
            copies[-1].start()
        for cp in copies:
            cp.wait()

    out_shape = [jax.ShapeDtypeStruct(h.shape, h.dtype) for h in halves]
    return pl.pallas_call(body, in_specs=[ANY] * n, out_specs=[ANY] * n, out_shape=out_shape,
                          scratch_shapes=[pltpu.SemaphoreType.DMA((n,)), pltpu.SemaphoreType.DMA((n,))],
                          name="join_halves")(*halves)


def _allreduce_small(v):
    rows, n = v.shape

    def body(v_ref, o_ref, buf, send, recv):
        x, y, c, _ = _place()
        me = 4 * x + 2 * y + c
        buf[me] = v_ref[...]
        copies = []
        for d in range(1, 8):
            peer = (x ^ (d >> 2), y ^ ((d >> 1) & 1), c ^ (d & 1))
            cp = pltpu.make_async_remote_copy(src_ref=v_ref, dst_ref=buf.at[me], send_sem=send.at[d], recv_sem=recv.at[d],
                                              device_id=peer, device_id_type=MESH)
            cp.start()
            copies.append(cp)
        for d in range(1, 8):
            peer = 4 * (x ^ (d >> 2)) + 2 * (y ^ ((d >> 1) & 1)) + (c ^ (d & 1))
            pltpu.make_async_remote_copy(src_ref=v_ref, dst_ref=buf.at[peer], send_sem=send.at[d], recv_sem=recv.at[d],
                                         device_id=(x, y, c), device_id_type=MESH).wait_recv()
        for cp in copies:
            cp.wait_send()
        acc = buf[0]
        for d in range(1, 8):
            acc = acc + buf[d]
        o_ref[...] = acc

    vm = pl.BlockSpec(memory_space=pltpu.VMEM)
    return pl.pallas_call(body, in_specs=[vm], out_specs=vm, out_shape=jax.ShapeDtypeStruct((rows, n), F32),
                          scratch_shapes=[pltpu.VMEM((8, rows, n), F32), pltpu.SemaphoreType.DMA((8,)),
                                          pltpu.SemaphoreType.DMA((8,))], name="allreduce_small")(v)


def _add_call(name, parts, tm, out_dtype=F32):
    def fn(i, *vals):
        acc = vals[0].astype(F32)
        for v in vals[1:]:
            acc = acc + v.astype(F32)
        return (acc,), ()
    return _rows_call(name, fn, list(parts), [], [(parts[0].shape[1], out_dtype)], [], tm)[0]


def _adamw(name, w, g, m, v):
    c1 = 1.0 / (1.0 - ADAM_B1 ** ADAM_STEP)
    c2 = 1.0 / (1.0 - ADAM_B2 ** ADAM_STEP)

    def fn(i, wv, gv, mv, vv):
        m2 = ADAM_B1 * mv + (1.0 - ADAM_B1) * gv
        v2 = ADAM_B2 * vv + (1.0 - ADAM_B2) * jnp.square(gv)
        delta = -ADAM_LR * ((m2 * c1) / (jnp.sqrt(v2 * c2) + ADAM_EPS) + ADAM_WD * wv)
        return (delta, m2, v2), ()
    cols = w.shape[1]
    tm = _tile(w.shape[0], (256, 176, 128, 64, 8))
    return _rows_call(name, fn, [w, g, m, v], [], [(cols, F32)] * 3, [], tm)


def _canon(a, transposed):
    return a[0].T if transposed else a[0]


def _pack_sharded(w_up, a_up, g_up, conv_dw):
    parts = [w_up[0].T, a_up[0].T, g_up[0].T, conv_dw[0].T]
    used = sum(p.shape[1] for p in parts)
    return jnp.concatenate(parts + [jnp.zeros((parts[0].shape[0], PACK_W - used), F32)], axis=1)


def _unpack_sharded(a):
    return [a[:, 0:64].T[None], a[:, 64:128].T[None], a[:, 128:256].T[None], a[:, 256:256 + CW].T[None]]


def _pack_small(vals):
    flat = jnp.concatenate([v.reshape(-1) for v in vals] + [jnp.zeros((SMALL_PAD - SMALL_N,), F32)])
    return flat.reshape(8, SMALL_PAD // 8)


def _unpack_small(a, like):
    flat, out, off = a.reshape(-1), [], 0
    for (_, n), ref in zip(SMALL, like):
        out.append(flat[off:off + n].reshape(ref.shape))
        off += n
    return out


def kernel(x, ffn1_norm_pre, ffn1_norm_post, ffn1_w_gu, ffn1_w_down, mix_norm_pre, mix_norm_post, w_in, shift_mu, w_up, w0, a_up, a0, g_up, k_k, k_a, r_k, gn_w, gn_b, conv_dw, conv_b, conv_ln_w, conv_ln_b, w_out, ffn2_norm_pre, ffn2_norm_post, ffn2_w_gu, ffn2_w_down, loss_target, m_ffn1_norm_pre, m_ffn1_norm_post, m_ffn1_w_gu, m_ffn1_w_down, m_mix_norm_pre, m_mix_norm_post, m_w_in, m_shift_mu, m_w_up, m_w0, m_a_up, m_a0, m_g_up, m_k_k, m_k_a, m_r_k, m_gn_w, m_gn_b, m_conv_dw, m_conv_b, m_conv_ln_w, m_conv_ln_b, m_w_out, m_ffn2_norm_pre, m_ffn2_norm_post, m_ffn2_w_gu, m_ffn2_w_down, v_ffn1_norm_pre, v_ffn1_norm_post, v_ffn1_w_gu, v_ffn1_w_down, v_mix_norm_pre, v_mix_norm_post, v_w_in, v_shift_mu, v_w_up, v_w0, v_a_up, v_a0, v_g_up, v_k_k, v_k_a, v_r_k, v_gn_w, v_gn_b, v_conv_dw, v_conv_b, v_conv_ln_w, v_conv_ln_b, v_w_out, v_ffn2_norm_pre, v_ffn2_norm_post, v_ffn2_w_gu, v_ffn2_w_down):
    w = dict(ffn1_norm_pre=ffn1_norm_pre, ffn1_norm_post=ffn1_norm_post, ffn1_w_gu=ffn1_w_gu, ffn1_w_down=ffn1_w_down, mix_norm_pre=mix_norm_pre, mix_norm_post=mix_norm_post, w_in=w_in, shift_mu=shift_mu, w_up=w_up, w0=w0, a_up=a_up, a0=a0, g_up=g_up, k_k=k_k, k_a=k_a, r_k=r_k, gn_w=gn_w, gn_b=gn_b, conv_dw=conv_dw, conv_b=conv_b, conv_ln_w=conv_ln_w, conv_ln_b=conv_ln_b, w_out=w_out, ffn2_norm_pre=ffn2_norm_pre, ffn2_norm_post=ffn2_norm_post, ffn2_w_gu=ffn2_w_gu, ffn2_w_down=ffn2_w_down)
    mom = dict(ffn1_norm_pre=m_ffn1_norm_pre, ffn1_norm_post=m_ffn1_norm_post, ffn1_w_gu=m_ffn1_w_gu, ffn1_w_down=m_ffn1_w_down, mix_norm_pre=m_mix_norm_pre, mix_norm_post=m_mix_norm_post, w_in=m_w_in, shift_mu=m_shift_mu, w_up=m_w_up, w0=m_w0, a_up=m_a_up, a0=m_a0, g_up=m_g_up, k_k=m_k_k, k_a=m_k_a, r_k=m_r_k, gn_w=m_gn_w, gn_b=m_gn_b, conv_dw=m_conv_dw, conv_b=m_conv_b, conv_ln_w=m_conv_ln_w, conv_ln_b=m_conv_ln_b, w_out=m_w_out, ffn2_norm_pre=m_ffn2_norm_pre, ffn2_norm_post=m_ffn2_norm_post, ffn2_w_gu=m_ffn2_w_gu, ffn2_w_down=m_ffn2_w_down)
    var = dict(ffn1_norm_pre=v_ffn1_norm_pre, ffn1_norm_post=v_ffn1_norm_post, ffn1_w_gu=v_ffn1_w_gu, ffn1_w_down=v_ffn1_w_down, mix_norm_pre=v_mix_norm_pre, mix_norm_post=v_mix_norm_post, w_in=v_w_in, shift_mu=v_shift_mu, w_up=v_w_up, w0=v_w0, a_up=v_a_up, a0=v_a0, g_up=v_g_up, k_k=v_k_k, k_a=v_k_a, r_k=v_r_k, gn_w=v_gn_w, gn_b=v_gn_b, conv_dw=v_conv_dw, conv_b=v_conv_b, conv_ln_w=v_conv_ln_w, conv_ln_b=v_conv_ln_b, w_out=v_w_out, ffn2_norm_pre=v_ffn2_norm_pre, ffn2_norm_post=v_ffn2_norm_post, ffn2_w_gu=v_ffn2_w_gu, ffn2_w_down=v_ffn2_w_down)
    names = list(w)

    x0 = x[0]
    tgt = loss_target[0]
    t_len = x0.shape[0]
    tm = _tile(t_len, (512, 256, 128))

    half = {n: rows // 8 for n, rows, _ in BIG}
    pack_local = _pack_sharded(w["w_up"], w["a_up"], w["g_up"], w["conv_dw"])
    shard = {n: _canon(w[n], tr).astype(BF16).reshape(2, half[n], D) for n, _, tr in BIG}
    rows_of = {n: rows for n, rows, _ in BIG}
    full = {"ffn1_w_gu": _gather_weights([shard["ffn1_w_gu"]])[0].reshape(2 * DFF, D)}

    h1, gu1, act1, f1, gathered = _ffn_fwd("ffn1", x0, w["ffn1_norm_pre"], full["ffn1_w_gu"], None, tm,
                                           gather=[shard[n] for n in SECOND] + [pack_local.reshape(2, 64, PACK_W)])
    full.update({n: g.reshape(rows_of[n], D) for n, g in zip(SECOND, gathered)})
    pack = gathered[-1].reshape(DR, PACK_W)
    p01, p2 = pack[:, 0:128], pack[:, 128:256]
    dw32 = jnp.concatenate([pack[:, 256:256 + CW].T, jnp.zeros((1, DR), F32)], axis=0)
    def resid_norm(scale):
        def fn(i, xv, fv, g_post, g_pre):
            xn = xv + scale * _rms(fv, g_post)
            return (xn, _rms(xn, g_pre)), ()
        return fn
    x1, hm = _rows_call("ffn1_resid", resid_norm(0.5), [x0, f1], [w["ffn1_norm_post"], w["mix_norm_pre"]],
                        [(D, F32), (D, BF16)], [], tm)
    p = _matmul("mix_in", hm, full["w_in"], "nt")
    ps = _shift_fwd(p, w["shift_mu"], tm)
    rkc = w["r_k"].reshape(1, DR)
    prep_consts = [w["w0"], w["a0"], w["k_k"], w["k_a"], p01, p2]
    r_, dec, k2, v_, z_, b_, g_ = _rows_call(
        "prep", lambda i, psv, *cs: (_prep(psv, *cs), ()), [ps], prep_consts, [(DR, F32)] * 7, [], tm)
    y, ck, late = _rec_fwd(r_, dec, k2, z_, b_, v_, [shard[n] for n in LATE])
    full.update({n: g.reshape(rows_of[n], D) for n, g in zip(LATE, late)})
    glu, cpre, ob = _conv_fwd(p, dw32, w["conv_b"], w["conv_ln_w"], w["conv_ln_b"], tm)
    post_consts = [w["gn_w"], w["gn_b"], rkc]
    (o,) = _rows_call(
        "post", lambda i, yv, rv, kv, vv, gv, obv, *cs: ((jnp.concatenate([_post(yv, rv, kv, vv, gv, *cs).astype(BF16), obv], axis=1),), ()),
        [y, r_, k2, v_, g_, ob], post_consts, [(D, BF16)], [], tm)
    mo = _matmul("mix_out", o, full["w_out"], "nn")
    x2, h2 = _rows_call("mix_resid", resid_norm(1.0), [x1, mo], [w["mix_norm_post"], w["ffn2_norm_pre"]],
                        [(D, F32), (D, BF16)], [], tm)
    h2, gu2, act2, f2, _ = _ffn_fwd("ffn2", x2, w["ffn2_norm_pre"], full["ffn2_w_gu"], full["ffn2_w_down"], tm, h=h2)

    g_small = {}

    def loss_fn(i, xv, fv, tv, g):
        branch, vjp = jax.vjp(lambda a, b: 0.5 * _rms(a, b), fv, g)
        err = xv + branch - tv
        part = 0.5 * jnp.sum(jnp.mean(err * err, axis=-1, keepdims=True), axis=0, keepdims=True)
        dx = err * (1.0 / D)
        df, dg = vjp(dx)
        return (dx, df), (jnp.broadcast_to(part, (8, 128)), dg)
    dx3, df2, loss_part, g_small["ffn2_norm_post"] = _rows_call(
        "loss", loss_fn, [x2, f2, tgt], [w["ffn2_norm_post"]], [(D, F32), (D, BF16)], [(8, 128), (1, D)], tm)
    loss = lax.psum(loss_part[0, 0], ("x", "y", "c"))

    (dx2, dmo, g_small["ffn2_norm_pre"], g_small["mix_norm_post"]), dgu2_t, dwd2, _ = _ffn_bwd(
        "ffn2", x2, w["ffn2_norm_pre"], h2, gu2, act2, dx3, df2, full["ffn2_w_gu"], full["ffn2_w_down"], tm,
        below=(mo, w["mix_norm_post"], 1.0))
    do = _matmul("mix_do", dmo, full["w_out"], "nt")
    dw_out = _matmul("mix_dwout", o, dmo, "tn")

    def post_b(i, yv, rv, kv, vv, gv, dov, *cs):
        _, vjp = jax.vjp(_post, yv, rv, kv, vv, gv, *cs)
        dy, dr, dk, dv, dg, dgw, dgb, drk = vjp(dov[:, :DR])
        return (dy, dr, dk, dv, dg), (_colsum(dgw), _colsum(dgb), _colsum(drk))
    dy, dr1, dk1, dv1, dg, g_small["gn_w"], g_small["gn_b"], g_small["r_k"] = _rows_call(
        "post_b", post_b, [y, r_, k2, v_, g_, do], post_consts, [(DR, F32)] * 5, [(1, DR)] * 3, tm)
    my_c = lax.axis_index("c")
    my_chip = 2 * lax.axis_index("x") + lax.axis_index("y")
    g_big = dict(w_out=dw_out, ffn2_w_gu=dgu2_t, ffn2_w_down=dwd2)

    def halves_of(group, which):
        return jnp.concatenate([lax.dynamic_index_in_dim(g_big[n].reshape(4, 2, half[n], D), which, 1, keepdims=False)
                                for n in group], axis=1)

    def pair_sums(tag, group, extra_mine=(), extra_give=()):
        mine = halves_of(group, my_c)
        got, *got_x = _swap_halves("swap_halves_" + tag, [halves_of(group, 1 - my_c).astype(BF16)] + list(extra_give))
        rows = mine.shape[1]
        tile = _tile(rows, (352, 592, 16))
        travels = _add_call("add_pair_" + tag, [mine.reshape(4 * rows, D), got.reshape(4 * rows, D)], tile, BF16)
        return mine, got, travels.reshape(4, rows, D), got_x

    def owner_sum(tag, mine, got, parts):
        own = [lax.dynamic_index_in_dim(a, my_chip, 0, keepdims=False) for a in (mine, got)]
        return _add_call("add_chips_" + tag, own + [parts[m] for m in range(3)], _tile(mine.shape[1], (352, 592, 16)))

    mine_l, got_l, sum_l, _ = pair_sums("late", LATE)
    (dr2, ddec, dk2, dz, db, dv2), (parts_l,) = _rec_bwd(r_, dec, k2, z_, b_, v_, dy, ck, [sum_l])

    def prep_b(i, psv, a1, a2, c1, c2, e1, e2, dwv, dzv, dbv, dgv, *cs):
        _, vjp = jax.vjp(_prep, psv, *cs)
        dps, dw0, da0, dkk, dka, dp01, dp2 = vjp((a1 + a2, dwv, c1 + c2, e1 + e2, dzv, dbv, dgv))
        return (dps,), (_colsum(dw0), _colsum(da0), _colsum(dkk), _colsum(dka), dp01, dp2)
    dps, g_small["w0"], g_small["a0"], g_small["k_k"], g_small["k_a"], dp01, dp2 = _rows_call(
        "prep_b", prep_b, [ps, dr1, dr2, dk1, dk2, dv1, dv2, ddec, dz, db, dg], prep_consts, [(D_SHIFT, F32)],
        [(1, DR)] * 4 + [(DR, 128)] * 2, tm)

    def convln_b(i, cv, dov, lw, lb):
        _, vjp = jax.vjp(_ln_silu, cv, lw, lb)
        dc, dlw, dlb = vjp(dov[:, DR:])
        return (dc,), (_colsum(dc), _colsum(dlw), _colsum(dlb))
    dc, g_small["conv_b"], g_small["conv_ln_w"], g_small["conv_ln_b"] = _rows_call(
        "convln_b", convln_b, [cpre, do], [w["conv_ln_w"], w["conv_ln_b"]], [(DR, F32)], [(1, DR)] * 3, tm)
    dpc, ddw32 = _conv_bwd(dc, glu, p, dw32, tm)
    dp, g_small["shift_mu"] = _shift_bwd(dps, p, dpc, w["shift_mu"], tm)
    dhm = _matmul("mix_dh", dp, full["w_in"], "nn")
    dw_in_t = _matmul("mix_dwin", dp, hm, "tn")

    dx1, df1, g_small["mix_norm_pre"], g_small["ffn1_norm_post"] = _norm_resid_b(
        "mix_norm_b", x1, dhm, dx2, w["mix_norm_pre"], tm, below=(f1, w["ffn1_norm_post"], 0.5))
    kept = {}

    def second_sums(dwd1):
        g_big.update(ffn1_w_down=dwd1, w_in=dw_in_t)
        pack_grads = jnp.concatenate([dp01, dp2, ddw32.T, jnp.zeros((DR, PACK_W - 288), F32)], axis=1).reshape(4, 2, 64, PACK_W)
        mine_p, give_p = (lax.dynamic_index_in_dim(pack_grads, which, 1, keepdims=False) for which in (my_c, 1 - my_c))
        mine, got, travels, (got_p,) = pair_sums("second", SECOND, extra_give=[give_p])
        sum_p = _add_call("add_pair_pack", [mine_p.reshape(256, PACK_W), got_p.reshape(256, PACK_W)], 256).reshape(4, 64, PACK_W)
        kept.update(second=(mine, got), sum_p=sum_p)
        return [travels, sum_p]

    def first_sums(dgu1_t):
        g_big.update(ffn1_w_gu=dgu1_t)
        mine, got, travels, _ = pair_sums("first", ("ffn1_w_gu",))
        kept.update(first=(mine, got))
        return [travels]

    (grad_x, g_small["ffn1_norm_pre"]), _, _, ((parts_s, parts_p), (parts_f,)) = _ffn_bwd(
        "ffn1", x0, w["ffn1_norm_pre"], h1, gu1, act1, dx1, df1, full["ffn1_w_gu"], full["ffn1_w_down"], tm,
        after_down=second_sums, after_gu=first_sums)

    sum_p = kept["sum_p"]
    fin_s = owner_sum("second", *kept["second"], parts_s)
    fin_f = owner_sum("first", *kept["first"], parts_f)
    fin_l = owner_sum("late", mine_l, got_l, parts_l)
    fin_p = _add_call("add_chips_pack", [lax.dynamic_index_in_dim(sum_p, my_chip, 0, keepdims=False)] + [parts_p[m] for m in range(3)], 64)
    fins = [fin_f, fin_s, fin_l, fin_p]
    red_f, red_s, red_l, red_p = [jnp.where(my_c == 0, jnp.stack([f, s]), jnp.stack([s, f])) for f, s in zip(fins, _join_halves(fins))]
    small_sum = _allreduce_small(_pack_small([g_small[n] for n, _ in SMALL]))

    grads, delta, new_m, new_v = {}, {}, {}, {}
    reduced = {}
    for group, red in ((("ffn1_w_gu",), red_f), (SECOND, red_s), (LATE, red_l)):
        off = 0
        for n in group:
            reduced[n] = red[:, off:off + half[n], :].reshape(rows_of[n] // 4, D)
            off += half[n]
    for n, rows, tr in BIG:
        g = reduced[n]
        g = (g.T if tr else g)[None]
        grads[n] = g
        d_, m_, v2_ = _adamw("adamw_" + n, w[n][0], g[0], mom[n][0], var[n][0])
        delta[n], new_m[n], new_v[n] = d_[None], m_[None], v2_[None]
    sh = ("w_up", "a_up", "g_up", "conv_dw")
    g_pack = red_p.reshape(128, PACK_W)
    d_, m_, v2_ = _adamw("adamw_pack", pack_local, g_pack, _pack_sharded(*[mom[n] for n in sh]), _pack_sharded(*[var[n] for n in sh]))
    for dst, src in ((grads, g_pack), (delta, d_), (new_m, m_), (new_v, v2_)):
        for n, a in zip(sh, _unpack_sharded(src)):
            dst[n] = a
    sm = [n for n, _ in SMALL]
    d_, m_, v2_ = _adamw("adamw_small", _pack_small([w[n] for n in sm]), small_sum, _pack_small([mom[n] for n in sm]),
                         _pack_small([var[n] for n in sm]))
    like = [w[n] for n in sm]
    for dst, src in ((grads, small_sum), (delta, d_), (new_m, m_), (new_v, v2_)):
        for n, a in zip(sm, _unpack_small(src, like)):
            dst[n] = a

    wn = names
    return (loss, grad_x[None], *[grads[n] for n in wn], *[delta[n] for n in wn], *[new_m[n] for n in wn],
            *[new_v[n] for n in wn])
```

```python
import functools
import math

import jax
import jax.numpy as jnp
from jax import lax
from jax.experimental import pallas as pl
from jax.experimental.pallas import tpu as pltpu

F32 = jnp.float32
BF16 = jnp.bfloat16

D = 1024
DFF = 2816
HID_TILE = DFF // 2
DR = 512
HS = 64
D_SHIFT = 1792
D_IN = 2816
CW = 31
RMS_EPS = 1e-6
GN_EPS = 64e-5
LN_EPS = 1e-5
DECAY_SCALE = math.exp(-0.5)
ADAM_LR, ADAM_B1, ADAM_B2, ADAM_EPS, ADAM_WD, ADAM_STEP = 0.001, 0.9, 0.999, 1e-8, 0.01, 10

REC_TILE = 128
VMEM_LIMIT = 56 * 1024 * 1024
MESH = pl.DeviceIdType.MESH
ANY = pl.BlockSpec(memory_space=pl.ANY)

BIG = (("ffn1_w_gu", 2 * DFF, True), ("ffn1_w_down", DFF, False), ("w_in", D_IN, True),
       ("w_out", D, False), ("ffn2_w_gu", 2 * DFF, True), ("ffn2_w_down", DFF, False))
SECOND = ("ffn1_w_down", "w_in")
LATE = ("w_out", "ffn2_w_gu", "ffn2_w_down")
PACK_W = 384
SMALL = (("ffn1_norm_pre", D), ("ffn1_norm_post", D), ("mix_norm_pre", D), ("mix_norm_post", D),
         ("shift_mu", D_SHIFT), ("w0", DR), ("a0", DR), ("k_k", DR), ("k_a", DR), ("r_k", DR),
         ("gn_w", DR), ("gn_b", DR), ("conv_b", DR), ("conv_ln_w", DR), ("conv_ln_b", DR),
         ("ffn2_norm_pre", D), ("ffn2_norm_post", D))
SMALL_N = sum(n for _, n in SMALL)
SMALL_PAD = 8 * 1664


def _params(sem):
    return pltpu.CompilerParams(dimension_semantics=sem, vmem_limit_bytes=VMEM_LIMIT)


def _tile(n, prefs):
    for p in prefs:
        if n % p == 0:
            return p
    return n


def _rows_call(name, fn, rows, consts, row_outs, acc_outs, tm):
    specs, arrs = [], []
    for r in rows:
        if isinstance(r, tuple):
            a, bs, im = r
            specs.append(pl.BlockSpec(bs, im))
        else:
            a = r
            specs.append(pl.BlockSpec((tm, a.shape[1]), lambda i: (i, 0)))
        arrs.append(a)
    t_rows = arrs[0].shape[0]
    for c in consts:
        specs.append(pl.BlockSpec(c.shape, functools.partial(lambda i, n: (0,) * n, n=c.ndim)))
        arrs.append(c)
    n_in, n_o, n_a = len(arrs), len(row_outs), len(acc_outs)

    def kern(*refs):
        i = pl.program_id(0)
        vals = [r[...] for r in refs[:n_in]]
        ro, ao = fn(i, *vals)
        outs = refs[n_in:]
        for k in range(n_o):
            outs[k][...] = ro[k].astype(outs[k].dtype)
        if n_a:
            @pl.when(i == 0)
            def _():
                for k in range(n_a):
                    outs[n_o + k][...] = jnp.zeros(outs[n_o + k].shape, F32)
            for k in range(n_a):
                outs[n_o + k][...] += ao[k]

    out_specs = [pl.BlockSpec((tm, w), lambda i: (i, 0)) for (w, _) in row_outs]
    out_specs += [pl.BlockSpec(s, functools.partial(lambda i, n: (0,) * n, n=len(s))) for s in acc_outs]
    out_shape = [jax.ShapeDtypeStruct((t_rows, w), dt) for (w, dt) in row_outs]
    out_shape += [jax.ShapeDtypeStruct(s, F32) for s in acc_outs]
    return pl.pallas_call(kern, grid=(t_rows // tm,), in_specs=specs, out_specs=out_specs, out_shape=out_shape,
                          name=name, compiler_params=_params(("arbitrary",)))(*arrs)


_DIMS = {"nn": (((1,), (0,)), ((), ())), "nt": (((1,), (1,)), ((), ())), "tn": (((0,), (0,)), ((), ()))}
_LANE_TILES = (1408, 1024, 512, 384, 256, 128)


def _matmul(name, a, b, mode, out_dtype=F32, owners=None, gather=None):
    if mode == "nn":
        (m, k), (_, n) = a.shape, b.shape
    elif mode == "nt":
        (m, k), (n, _) = a.shape, b.shape
    else:
        (k, m), (_, n) = a.shape, b.shape
    if mode == "tn":
        tm, tk = _tile(m, _LANE_TILES), _tile(k, (512, 256, 128))
    else:
        tm, tk = _tile(m, (1024, 512, 256, 128)), _tile(k, _LANE_TILES)
    tn = _tile(n, _LANE_TILES)
    nk = k // tk
    assert out_dtype == F32 or nk == 1

    owners, gather = list(owners or ()), list(gather or ())
    assert not (owners and gather)
    riders = owners + gather
    ns = len(riders)
    grid = (m // tm, n // tn, nk)
    steps = grid[0] * grid[1] * nk

    def kern(a_ref, b_ref, *rest):
        o_ref = rest[ns]
        if ns:
            step = (pl.program_id(0) * grid[1] + pl.program_id(1)) * nk + pl.program_id(2)
            if owners:
                start, finish = _owners_plan(rest[:ns], rest[ns + 1:2 * ns + 1], *rest[2 * ns + 1:])
                pl.when(step == 0)(start)
            else:
                start, middle, finish = _gather_plan(rest[:ns], rest[ns + 1:2 * ns + 1], *rest[2 * ns + 1:])
                pl.when(step == 0)(start)
                pl.when(step == steps // 2)(middle)

        def part():
            return lax.dot_general(a_ref[...].astype(BF16), b_ref[...].astype(BF16), _DIMS[mode], preferred_element_type=F32)

        if nk == 1:
            o_ref[...] = part().astype(o_ref.dtype)
        else:
            @pl.when(pl.program_id(2) == 0)
            def _():
                o_ref[...] = jnp.zeros(o_ref.shape, F32)

            o_ref[...] += part()
        if ns:
            pl.when(step == steps - 1)(finish)

    a_spec = pl.BlockSpec((tk, tm), lambda i, j, q: (q, i)) if mode == "tn" else pl.BlockSpec((tm, tk), lambda i, j, q: (i, q))
    b_spec = pl.BlockSpec((tn, tk), lambda i, j, q: (j, q)) if mode == "nt" else pl.BlockSpec((tk, tn), lambda i, j, q: (q, j))
    extra_shapes = _owner_shapes(owners) if owners else _gathered_shapes(gather)
    extra_sems = [] if not ns else _owner_sems(ns) if owners else _gather_sems(ns)
    out, *arrived = pl.pallas_call(
        kern, grid=grid, in_specs=[a_spec, b_spec] + [ANY] * ns,
        out_specs=[pl.BlockSpec((tm, tn), lambda i, j, q: (i, j))] + [ANY] * ns,
        out_shape=[jax.ShapeDtypeStruct((m, n), out_dtype)] + extra_shapes,
        scratch_shapes=extra_sems, name=name,
        compiler_params=_params(("arbitrary", "arbitrary", "arbitrary")))(a, b, *riders)
    if gather:
        arrived = _fill_own(arrived, gather)
    return (out, arrived) if ns else out


def _rms(x, g):
    return x * lax.rsqrt(jnp.mean(x * x, axis=-1, keepdims=True) + RMS_EPS) * g


def _silu(x):
    return x * jax.nn.sigmoid(x)


def _bones(n):
    r = lax.broadcasted_iota(jnp.int32, (n, n), 0) // HS
    c = lax.broadcasted_iota(jnp.int32, (n, n), 1) // HS
    return (r == c).astype(BF16)


@jax.custom_vjp
def _segsum(x):
    bones = _bones(x.shape[1])
    hi = x.astype(BF16)
    lo = (x - hi.astype(F32)).astype(BF16)
    return jnp.dot(hi, bones, preferred_element_type=F32) + jnp.dot(lo, bones, preferred_element_type=F32)


_segsum.defvjp(lambda x: (_segsum(x), None), lambda _, ct: (_segsum(ct),))


@jax.custom_vjp
def _dot_nt(x, w):
    return lax.dot_general(x.astype(BF16), w.astype(BF16), _DIMS["nt"], preferred_element_type=F32)


def _dot_nt_bwd(res, ct):
    x, w = res
    ctb = ct.astype(BF16)
    dx = lax.dot_general(ctb, w.astype(BF16), _DIMS["nn"], preferred_element_type=F32)
    dw = lax.dot_general(ctb, x.astype(BF16), _DIMS["tn"], preferred_element_type=F32)
    return dx, dw


_dot_nt.defvjp(lambda x, w: (_dot_nt(x, w), (x, w)), _dot_nt_bwd)


def _prep(ps, w0, a0, k_k, k_a, p01, p2):
    r, k, v = ps[:, :DR], ps[:, DR:2 * DR], ps[:, 2 * DR:3 * DR]
    wa, xg = ps[:, 3 * DR:3 * DR + 128], ps[:, 3 * DR + 128:]
    first = lax.broadcasted_iota(jnp.int32, (1, 128), 1) < 64
    d = w0 + _dot_nt(jnp.where(first, jnp.tanh(wa), 0.0), p01)
    decay = jnp.exp(-DECAY_SCALE * jax.nn.sigmoid(d))
    a = jax.nn.sigmoid(a0 + _dot_nt(jnp.where(first, 0.0, wa), p01))
    g = _dot_nt(jax.nn.sigmoid(xg), p2)
    kk = k * k_k
    kk = kk * lax.rsqrt(jnp.maximum(_segsum(kk * kk), 1e-12))
    k2 = k * (1.0 + (a - 1.0) * k_a)
    return r, decay, k2, v, -kk, kk * a, g


def _post(y, r, k, v, g, gn_w, gn_b, r_k):
    mu = _segsum(y) * (1.0 / HS)
    yc = y - mu
    var = _segsum(yc * yc) * (1.0 / HS)
    yn = yc * lax.rsqrt(var + GN_EPS) * gn_w + gn_b
    return (yn + _segsum(r * k * r_k) * v) * g


def _ln_silu(c, w, b):
    mu = jnp.mean(c, axis=-1, keepdims=True)
    var = jnp.mean(jnp.square(c - mu), axis=-1, keepdims=True)
    return _silu((c - mu) * lax.rsqrt(var + LN_EPS) * w + b)


def _colsum(x):
    return jnp.sum(x, axis=0, keepdims=True)


def _gu_swiglu(name, h, w_gu_t, gather=None):
    m = h.shape[0]
    tm, tn = _tile(m, (512, 256, 128)), HID_TILE
    nj = DFF // tn
    gather = list(gather or ())
    ns = len(gather)
    steps = (m // tm) * nj

    def kern(a_ref, bg_ref, bu_ref, *rest):
        gate_ref, up_ref, act_ref = rest[ns:ns + 3]
        if ns:
            step = pl.program_id(0) * nj + pl.program_id(1)
            start, middle, finish = _gather_plan(rest[:ns], rest[ns + 3:2 * ns + 3], *rest[2 * ns + 3:])
            pl.when(step == 0)(start)
            pl.when(step == steps * 3 // 4)(middle)
        a = a_ref[...]
        gate = lax.dot_general(a, bg_ref[...], _DIMS["nt"], preferred_element_type=F32)
        up = lax.dot_general(a, bu_ref[...], _DIMS["nt"], preferred_element_type=F32)
        gate_ref[...] = gate
        up_ref[...] = up
        act_ref[...] = (_silu(gate) * up).astype(BF16)
        if ns:
            pl.when(step == steps - 1)(finish)

    tile = pl.BlockSpec((tm, tn), lambda i, j: (i, j))
    gate, up, act, *got = pl.pallas_call(
        kern, grid=(m // tm, nj),
        in_specs=[pl.BlockSpec((tm, D), lambda i, j: (i, 0)), pl.BlockSpec((tn, D), lambda i, j: (j, 0)),
                  pl.BlockSpec((tn, D), lambda i, j: (j + nj, 0))] + [ANY] * ns,
        out_specs=[tile] * 3 + [ANY] * ns,
        out_shape=[jax.ShapeDtypeStruct((m, DFF), F32)] * 2 + [jax.ShapeDtypeStruct((m, DFF), BF16)] + _gathered_shapes(gather),
        scratch_shapes=_gather_sems(ns) if ns else [], name=name,
        compiler_params=_params(("arbitrary", "arbitrary")))(h, w_gu_t, w_gu_t, *gather)
    return gate, up, act, _fill_own(got, gather)


def _dact_swiglu_b(name, df, w_down, gate, up):
    m = df.shape[0]
    tm, tn = _tile(m, (512, 256, 128)), HID_TILE
    nj = DFF // tn

    def kern(df_ref, wd_ref, gate_ref, up_ref, o_ref):
        dact = lax.dot_general(df_ref[...], wd_ref[...], _DIMS["nt"], preferred_element_type=F32)
        _, vjp = jax.vjp(lambda a, b: _silu(a) * b, gate_ref[...], up_ref[...])
        dgate, dup = vjp(dact)
        for j in range(nj):
            @pl.when(pl.program_id(1) == j)
            def _(j=j):
                o_ref[:, tn * j:tn * (j + 1)] = dgate.astype(BF16)
                o_ref[:, DFF + tn * j:DFF + tn * (j + 1)] = dup.astype(BF16)

    tile = pl.BlockSpec((tm, tn), lambda i, j: (i, j))
    return pl.pallas_call(
        kern, grid=(m // tm, nj),
        in_specs=[pl.BlockSpec((tm, D), lambda i, j: (i, 0)), pl.BlockSpec((tn, D), lambda i, j: (j, 0)), tile, tile],
        out_specs=pl.BlockSpec((tm, 2 * DFF), lambda i, j: (i, 0)),
        out_shape=jax.ShapeDtypeStruct((m, 2 * DFF), BF16), name=name,
        compiler_params=_params(("arbitrary", "arbitrary")))(df, w_down, gate, up)


def _ffn_fwd(tag, x, pre, w_gu_t, w_down, tm, gather=None, h=None):
    if h is None:
        (h,) = _rows_call(tag + "_norm", lambda i, xv, g: ((_rms(xv, g),), ()), [x], [pre], [(D, BF16)], [], tm)
    gate, up, act, gathered = _gu_swiglu(tag + "_gu", h, w_gu_t, gather)
    if gather:
        w_down = gathered[0].reshape(DFF, D)
    f = _matmul(tag + "_down", act, w_down, "nn")
    return h, (gate, up), act, f, gathered


def _norm_resid_b(name, x, dh, dxo, pre, tm, below=None):
    def fn(i, *vals):
        xv, dhv, dv = vals[:3]
        _, vjp = jax.vjp(_rms, xv, vals[-1] if below is None else vals[-2])
        dx, dg = vjp(dhv)
        dx = dx + dv
        if below is None:
            return (dx,), (dg,)
        _, vjp = jax.vjp(lambda a, b: below[2] * _rms(a, b), vals[3], vals[-1])
        df, dgp = vjp(dx)
        return (dx, df), (dg, dgp)
    if below is None:
        return _rows_call(name, fn, [x, dh, dxo], [pre], [(D, F32)], [(1, D)], tm)
    return _rows_call(name, fn, [x, dh, dxo, below[0]], [pre, below[1]], [(D, F32), (D, BF16)], [(1, D)] * 2, tm)


def _ffn_bwd(tag, x, pre, h, gu, act, dxo, df, w_gu_t, w_down, tm, after_down=None, after_gu=None, below=None):
    dgu = _dact_swiglu_b(tag + "_dact", df, w_down, *gu)
    dw_down = _matmul(tag + "_dwdown", act, df, "tn")
    sums = after_down(dw_down) if after_down else []
    dw_gu_t = _matmul(tag + "_dwgu", dgu, h, "tn", owners=sums)
    dw_gu_t, parts_down = dw_gu_t if sums else (dw_gu_t, [])
    sums = after_gu(dw_gu_t) if after_gu else []
    dh = _matmul(tag + "_dh", dgu, w_gu_t, "nn", owners=sums)
    dh, parts_gu = dh if sums else (dh, [])
    parts = (parts_down, parts_gu)
    return _norm_resid_b(tag + "_norm_b", x, dh, dxo, pre, tm, below), dw_gu_t, dw_down, parts


def _pair_bcast(cols, first):
    return jnp.concatenate([jnp.where(first, cols[2 * p], cols[2 * p + 1]) for p in range(4)], axis=1)


def _head_sums(x, first):
    cols = []
    for p in range(4):
        xp = x[:, 128 * p:128 * (p + 1)]
        cols.append(jnp.sum(jnp.where(first, xp, 0.0), axis=1, keepdims=True))
        cols.append(jnp.sum(jnp.where(first, 0.0, xp), axis=1, keepdims=True))
    return cols


def _split16(x8):
    hi = x8.astype(BF16).astype(F32)
    return jnp.concatenate([hi, x8 - hi], axis=0).astype(BF16)


def _cols8(x8, e16):
    return lax.dot_general(_split16(x8), e16, _DIMS["tn"], preferred_element_type=F32)


def _pair_rows(c, first):
    return jnp.concatenate([jnp.where(first, c[128 * p:128 * p + HS], c[128 * p + HS:128 * (p + 1)]) for p in range(4)], axis=1)


def _rows8(prod, bones, dmask):
    x = prod.astype(BF16)
    full = jnp.concatenate([jnp.dot(x[:, 256 * q:256 * (q + 1)], bones, preferred_element_type=F32) for q in range(2)], axis=1)
    return jnp.concatenate([_colsum(full[HS * j:HS * (j + 1)] * dmask) for j in range(8)], axis=0)


def _rec_step(s, wr, zr, br, kr, vc, first):
    u = _pair_bcast(_head_sums(s * zr, first), first)
    return s * wr + u * br + vc * kr, u


def _rec_consts():
    e16 = (lax.broadcasted_iota(jnp.int32, (16, 1024), 0) % 8 == lax.broadcasted_iota(jnp.int32, (16, 1024), 1) // 128)
    dmask = lax.broadcasted_iota(jnp.int32, (HS, DR), 0) == lax.broadcasted_iota(jnp.int32, (HS, DR), 1) % HS
    return e16.astype(BF16), _bones(256), dmask.astype(F32)


def _const_spec(a):
    return pl.BlockSpec(a.shape, functools.partial(lambda i, n: (0,) * n, n=a.ndim))


def _rec_fwd(r, w, k, z, b, v, shards):
    t_len = r.shape[0]
    nt = t_len // REC_TILE
    ns = len(shards)
    consts = _rec_consts()

    def kern(r_ref, w_ref, k_ref, z_ref, b_ref, v_ref, e_ref, bones_ref, dm_ref, *rest):
        y_ref, ck_ref, states, u_ref = rest[ns:ns + 4]
        s_ref, prod = rest[2 * ns + 4:2 * ns + 6]
        start, middle, finish = _gather_plan(rest[:ns], rest[ns + 4:2 * ns + 4], *rest[2 * ns + 6:])
        i = pl.program_id(0)

        @pl.when(i == 0)
        def _():
            s_ref[...] = jnp.zeros(s_ref.shape, F32)
            start()

        pl.when(i == nt // 2)(middle)
        ck_ref[0] = s_ref[...]
        first = lax.broadcasted_iota(jnp.int32, (1, 128), 1) < HS

        def group(g8, s):
            base = pl.multiple_of(g8 * 8, 8)
            r8, w8, k8, z8, b8, v8 = (ref[pl.ds(base, 8), :] for ref in (r_ref, w_ref, k_ref, z_ref, b_ref, v_ref))
            vcols = _cols8(v8, e_ref[...])
            urows = []
            for j in range(8):
                vc = _pair_rows(vcols[:, 128 * j:128 * (j + 1)], first)
                s, u = _rec_step(s, w8[j:j + 1], z8[j:j + 1], b8[j:j + 1], k8[j:j + 1], vc, first)
                states[base + j] = s
                urows.append(_colsum(u * dm_ref[...]))
                prod[HS * j:HS * (j + 1), :] = s * r8[j:j + 1]
            y_ref[pl.ds(base, 8), :] = _rows8(prod[...], bones_ref[...], dm_ref[...])
            u_ref[pl.ds(base, 8), :] = jnp.concatenate(urows, axis=0)
            return s

        s_ref[...] = lax.fori_loop(0, REC_TILE // 8, group, s_ref[...])
        pl.when(i == nt - 1)(finish)

    row = pl.BlockSpec((REC_TILE, DR), lambda i: (i, 0))
    y, ck, states, u, *got = pl.pallas_call(
        kern, grid=(nt,), in_specs=[row] * 6 + [_const_spec(c) for c in consts] + [ANY] * ns,
        out_specs=[row, pl.BlockSpec((1, HS, DR), lambda i: (i, 0, 0)), pl.BlockSpec((REC_TILE, HS, DR), lambda i: (i, 0, 0)), row]
        + [ANY] * ns,
        out_shape=[jax.ShapeDtypeStruct((t_len, DR), F32), jax.ShapeDtypeStruct((nt, HS, DR), F32),
                   jax.ShapeDtypeStruct((t_len, HS, DR), F32), jax.ShapeDtypeStruct((t_len, DR), F32)] + _gathered_shapes(shards),
        scratch_shapes=[pltpu.VMEM((HS, DR), F32), pltpu.VMEM((8 * HS, DR), F32)] + _gather_sems(ns), name="rec_fwd",
        compiler_params=_params(("arbitrary",)))(r, w, k, z, b, v, *consts, *shards)
    return y, (ck, states, u), _fill_own(got, shards)


def _rec_bwd(r, w, k, z, b, v, dy, saved, sums):
    t_len = r.shape[0]
    nt = t_len // REC_TILE
    ns = len(sums)
    consts = _rec_consts()

    def kern(r_ref, w_ref, k_ref, z_ref, b_ref, v_ref, dy_ref, u_ref, ck_ref, states, e_ref, bones_ref, dm_ref, *rest):
        dr_ref, dw_ref, dk_ref, dz_ref, db_ref, dv_ref = rest[ns:ns + 6]
        ds_ref, prod = rest[2 * ns + 6:2 * ns + 8]
        start, finish = _owners_plan(rest[:ns], rest[ns + 6:2 * ns + 6], *rest[2 * ns + 8:])
        i = pl.program_id(0)

        @pl.when(i == 0)
        def _():
            ds_ref[...] = jnp.zeros(ds_ref.shape, F32)
            start()

        first = lax.broadcasted_iota(jnp.int32, (1, 128), 1) < HS

        def bgroup(gg, ds):
            base = pl.multiple_of((REC_TILE // 8 - 1 - gg) * 8, 8)
            r8, w8, k8, z8, b8, v8, dy8, u8 = (ref[pl.ds(base, 8), :]
                                               for ref in (r_ref, w_ref, k_ref, z_ref, b_ref, v_ref, dy_ref, u_ref))
            vcols = _cols8(v8, e_ref[...])
            dycols = _cols8(dy8, e_ref[...])
            ucols = _cols8(u8, e_ref[...])
            before = jnp.where(base == 0, ck_ref[0], states[jnp.maximum(base - 1, 0)])
            rows = {n: [None] * 8 for n in ("dr", "dw", "dk", "dz", "db")}
            for j in range(7, -1, -1):
                t = base + j
                rr, wr, kr, zr, br = (x[j:j + 1] for x in (r8, w8, k8, z8, b8))
                s_prev, s_t = (states[t - 1] if j else before), states[t]
                dyc = _pair_rows(dycols[:, 128 * j:128 * (j + 1)], first)
                vc = _pair_rows(vcols[:, 128 * j:128 * (j + 1)], first)
                ds = ds + dyc * rr
                rows["dr"][j] = _colsum(s_t * dyc)
                rows["dw"][j] = _colsum(ds * s_prev)
                du = _pair_bcast(_head_sums(ds * br, first), first)
                rows["db"][j] = _colsum(ds * _pair_rows(ucols[:, 128 * j:128 * (j + 1)], first))
                rows["dk"][j] = _colsum(ds * vc)
                prod[HS * j:HS * (j + 1), :] = ds * kr
                rows["dz"][j] = _colsum(s_prev * du)
                ds = ds * wr + du * zr
            for n, ref in (("dr", dr_ref), ("dw", dw_ref), ("dk", dk_ref), ("dz", dz_ref), ("db", db_ref)):
                ref[pl.ds(base, 8), :] = jnp.concatenate(rows[n], axis=0)
            dv_ref[pl.ds(base, 8), :] = _rows8(prod[...], bones_ref[...], dm_ref[...])
            return ds

        ds_ref[...] = lax.fori_loop(0, REC_TILE // 8, bgroup, ds_ref[...])
        pl.when(i == nt - 1)(finish)

    ck, states, u = saved
    row = pl.BlockSpec((REC_TILE, DR), lambda i: (nt - 1 - i, 0))
    outs = pl.pallas_call(
        kern, grid=(nt,),
        in_specs=[row] * 8 + [pl.BlockSpec((1, HS, DR), lambda i: (nt - 1 - i, 0, 0)),
                              pl.BlockSpec((REC_TILE, HS, DR), lambda i: (nt - 1 - i, 0, 0))]
        + [_const_spec(c) for c in consts] + [ANY] * ns,
        out_specs=[row] * 6 + [ANY] * ns, out_shape=[jax.ShapeDtypeStruct((t_len, DR), F32)] * 6 + _owner_shapes(sums),
        scratch_shapes=[pltpu.VMEM((HS, DR), F32), pltpu.VMEM((8 * HS, DR), F32)] + _owner_sems(ns), name="rec_bwd",
        compiler_params=_params(("arbitrary",)))(r, w, k, z, b, v, dy, u, ck, states, *consts, *sums)
    return outs[:6], outs[6:]


def _prev_rows(a, tm, n):
    return (a, (n, a.shape[1]), lambda i: (jnp.maximum(i * (tm // n) - 1, 0), 0))


def _next_rows(a, tm, n):
    last = a.shape[0] // n - 1
    return (a, (n, a.shape[1]), lambda i: (jnp.minimum((i + 1) * (tm // n), last), 0))


def _shifted(x, prev8, i):
    rowid = lax.broadcasted_iota(jnp.int32, x.shape, 0)
    before = jnp.where(i == 0, 0.0, prev8[7:8, :])
    return jnp.where(rowid == 0, before, pltpu.roll(x, 1, 0))


def _shift_fwd(p, mu, tm):
    def fn(i, pv, prev8, muv):
        x = pv[:, :D_SHIFT]
        return (x + (_shifted(x, prev8[:, :D_SHIFT], i) - x) * muv,), ()
    return _rows_call("shift", fn, [p, _prev_rows(p, tm, 8)], [mu], [(D_SHIFT, F32)], [], tm)[0]


def _shift_bwd(dps, p, dpc, mu, tm):
    n_tiles = p.shape[0] // tm

    def fn(i, dv, next8, pv, prev8, dpcv, muv):
        x = pv[:, :D_SHIFT]
        xs = _shifted(x, prev8[:, :D_SHIFT], i)
        rowid = lax.broadcasted_iota(jnp.int32, dv.shape, 0)
        after = jnp.where(i == n_tiles - 1, 0.0, next8[0:1, :])
        dnext = jnp.where(rowid == tm - 1, after, pltpu.roll(dv, tm - 1, 0))
        dp_s = dv * (1.0 - muv) + dnext * muv
        return (jnp.concatenate([dp_s.astype(BF16), dpcv], axis=1),), (_colsum(dv * (xs - x)),)
    return _rows_call("shift_b", fn, [dps, _next_rows(dps, tm, 8), p, _prev_rows(p, tm, 8), dpc], [mu],
                      [(D_IN, BF16)], [(1, D_SHIFT)], tm)


def _glu(pc):
    return pc[:, :DR] * jax.nn.sigmoid(pc[:, DR:])


def _shift_copies(ext, shifted, tm):
    for s in range(1, 8):
        shifted[s - 1] = ext[s:s + tm + 24, :]


def _window(ext, shifted, off, tm):
    if off % 8 == 0:
        return ext[off:off + tm, :]
    return shifted[off % 8 - 1, off // 8 * 8:off // 8 * 8 + tm, :]


def _conv_fwd(p, dw32, cb, lw, lb, tm):
    t_len = p.shape[0]

    def kern(p_ref, ph_ref, dw_ref, cb_ref, lw_ref, lb_ref, glu_ref, c_ref, ob_ref, ext, shifted):
        i = pl.program_id(0)
        glu = _glu(p_ref[:, D_SHIFT:])
        ext[0:32, :] = jnp.where(i == 0, 0.0, _glu(ph_ref[:, D_SHIFT:]))
        ext[32:, :] = glu
        _shift_copies(ext, shifted, tm)
        acc = jnp.zeros((tm, DR), F32)
        for j in range(CW):
            acc = acc + _window(ext, shifted, 2 + j, tm) * dw_ref[j:j + 1, :]
        c = acc + cb_ref[...]
        glu_ref[...] = glu
        c_ref[...] = c
        ob_ref[...] = _ln_silu(c, lw_ref[...], lb_ref[...]).astype(BF16)

    tile = lambda w: pl.BlockSpec((tm, w), lambda i: (i, 0))
    const = lambda a: pl.BlockSpec(a.shape, lambda i: (0, 0))
    halo = pl.BlockSpec((32, D_IN), lambda i: (jnp.maximum(i * (tm // 32) - 1, 0), 0))
    return pl.pallas_call(
        kern, grid=(t_len // tm,), in_specs=[tile(D_IN), halo, const(dw32), const(cb), const(lw), const(lb)],
        out_specs=[tile(DR)] * 3,
        out_shape=[jax.ShapeDtypeStruct((t_len, DR), F32)] * 2 + [jax.ShapeDtypeStruct((t_len, DR), BF16)],
        scratch_shapes=[pltpu.VMEM((tm + 32, DR), F32), pltpu.VMEM((7, tm + 24, DR), F32)], name="conv_fwd",
        compiler_params=_params(("arbitrary",)))(p, p, dw32, cb, lw, lb)


def _conv_bwd(dc, glu, p, dw32, tm):
    t_len = p.shape[0]
    n_tiles = t_len // tm

    def kern(dc_ref, dcn_ref, glu_ref, gluh_ref, p_ref, dw_ref, dpc_ref, ddw_ref, ext_d, ext_g, shifted_d, shifted_g):
        i = pl.program_id(0)

        @pl.when(i == 0)
        def _():
            ddw_ref[...] = jnp.zeros(ddw_ref.shape, F32)

        dcv = dc_ref[...]
        ext_d[0:tm, :] = dcv
        ext_d[tm:, :] = jnp.where(i == n_tiles - 1, 0.0, dcn_ref[...])
        ext_g[0:32, :] = jnp.where(i == 0, 0.0, gluh_ref[...])
        ext_g[32:, :] = glu_ref[...]
        _shift_copies(ext_d, shifted_d, tm)
        _shift_copies(ext_g, shifted_g, tm)
        dglu = jnp.zeros((tm, DR), F32)
        for j in range(CW):
            dglu = dglu + _window(ext_d, shifted_d, 30 - j, tm) * dw_ref[j:j + 1, :]
            ddw_ref[j:j + 1, :] += _colsum(dcv * _window(ext_g, shifted_g, 2 + j, tm))
        pc = p_ref[:, D_SHIFT:]
        sg = jax.nn.sigmoid(pc[:, DR:])
        dpc_ref[...] = jnp.concatenate([dglu * sg, dglu * pc[:, :DR] * sg * (1.0 - sg)], axis=1).astype(BF16)

    tile = lambda w: pl.BlockSpec((tm, w), lambda i: (i, 0))
    nxt = pl.BlockSpec((32, DR), lambda i: (jnp.minimum((i + 1) * (tm // 32), t_len // 32 - 1), 0))
    prv = pl.BlockSpec((32, DR), lambda i: (jnp.maximum(i * (tm // 32) - 1, 0), 0))
    return pl.pallas_call(
        kern, grid=(n_tiles,), in_specs=[tile(DR), nxt, tile(DR), prv, tile(D_IN), pl.BlockSpec((32, DR), lambda i: (0, 0))],
        out_specs=[tile(D), pl.BlockSpec((32, DR), lambda i: (0, 0))],
        out_shape=[jax.ShapeDtypeStruct((t_len, D), BF16), jax.ShapeDtypeStruct((32, DR), F32)],
        scratch_shapes=[pltpu.VMEM((tm + 32, DR), F32)] * 2 + [pltpu.VMEM((7, tm + 24, DR), F32)] * 2, name="conv_bwd",
        compiler_params=_params(("arbitrary",)))(dc, dc, glu, glu, p, dw32)


def _place():
    x, y, c = lax.axis_index("x"), lax.axis_index("y"), lax.axis_index("c")
    chips = [(1 - x, y), (x, 1 - y), (1 - x, 1 - y)]
    return x, y, c, chips


def _gather_weights(shards):
    n = len(shards)

    def body(*refs):
        start, middle, finish = _gather_plan(refs[:n], refs[n:2 * n], *refs[2 * n:])
        start()
        middle()
        finish()

    got = pl.pallas_call(body, in_specs=[ANY] * n, out_specs=[ANY] * n, out_shape=_gathered_shapes(shards),
                         scratch_shapes=_gather_sems(n), name="gather_weights")(*shards)
    return _fill_own(got, shards)


def _gathered_shapes(shards):
    return [jax.ShapeDtypeStruct((4,) + s.shape, s.dtype) for s in shards]


def _gather_sems(n):
    return [pltpu.SemaphoreType.DMA((6 * n,)), pltpu.SemaphoreType.DMA((6 * n,))]


def _fill_own(got, shards):
    me = 2 * lax.axis_index("x") + lax.axis_index("y")
    return [lax.dynamic_update_slice(g, s[None], (me, 0, 0, 0)) for g, s in zip(got, shards)]


def _gather_plan(src, dst, send, recv):
    n = len(src)
    x, y, c, chips = _place()
    me, sib = 2 * x + y, (x, y, 1 - c)

    def rcopy(k, sem, s_ref, d_ref, to):
        return pltpu.make_async_remote_copy(src_ref=s_ref, dst_ref=d_ref, send_sem=send.at[6 * k + sem],
                                            recv_sem=recv.at[6 * k + sem], device_id=to, device_id_type=MESH)

    def landed(k, m, half):
        return dst[k].at[2 * chips[m][0] + chips[m][1], half]

    first = [rcopy(k, m, src[k].at[c], dst[k].at[me, c], (*chips[m], c)) for k in range(n) for m in range(3)]
    passed = [rcopy(k, 3 + m, landed(k, m, c), landed(k, m, c), sib) for k in range(n) for m in range(3)]

    def start():
        for cp in first:
            cp.start()

    def middle():
        for k in range(n):
            for m in range(3):
                rcopy(k, m, landed(k, m, c), landed(k, m, c), sib).wait_recv()
                passed[3 * k + m].start()

    def finish():
        for k in range(n):
            for m in range(3):
                rcopy(k, 3 + m, landed(k, m, 1 - c), landed(k, m, 1 - c), sib).wait_recv()
        for cp in first + passed:
            cp.wait_send()

    return start, middle, finish


def _swap_halves(name, give):
    n = len(give)

    def body(*refs):
        src, got = refs[:n], refs[n:2 * n]
        send, recv = refs[2 * n:]
        x, y, c, _ = _place()
        copies = []
        for k in range(n):
            for j in range(4):
                copies.append(pltpu.make_async_remote_copy(
                    src_ref=src[k].at[j], dst_ref=got[k].at[j], send_sem=send.at[4 * k + j], recv_sem=recv.at[4 * k + j],
                    device_id=(x, y, 1 - c), device_id_type=MESH))
                copies[-1].start()
        for cp in copies:
            cp.wait()

    out_shape = [jax.ShapeDtypeStruct(g.shape, g.dtype) for g in give]
    return pl.pallas_call(body, in_specs=[ANY] * n, out_specs=[ANY] * n, out_shape=out_shape,
                          scratch_shapes=[pltpu.SemaphoreType.DMA((4 * n,)), pltpu.SemaphoreType.DMA((4 * n,))],
                          name=name)(*give)


def _owner_shapes(sums):
    return [jax.ShapeDtypeStruct((3,) + s.shape[1:], s.dtype) for s in sums]


def _owner_sems(n):
    return [pltpu.SemaphoreType.DMA((3 * n,)), pltpu.SemaphoreType.DMA((3 * n,))]


def _owners_plan(src, dst, send, recv):
    x, y, c, chips = _place()
    copies = [pltpu.make_async_remote_copy(
        src_ref=src[k].at[2 * chip[0] + chip[1]], dst_ref=dst[k].at[m], send_sem=send.at[3 * k + m],
        recv_sem=recv.at[3 * k + m], device_id=(*chip, c), device_id_type=MESH)
        for k in range(len(src)) for m, chip in enumerate(chips)]

    def start():
        for cp in copies:
            cp.start()

    def finish():
        for cp in copies:
            cp.wait()

    return start, finish


def _join_halves(halves):
    n = len(halves)

    def body(*refs):
        src, dst = refs[:n], refs[n:2 * n]
        send, recv = refs[2 * n:]
        x, y, c, _ = _place()
        copies = []
        for k in range(n):
            copies.append(pltpu.make_async_remote_copy(src_ref=src[k], dst_ref=dst[k], send_sem=send.at[k],
                                                       recv_sem=recv.at[k], device_id=(x, y, 1 - c), device_id_type=MESH))
            copies[-1].start()
        for cp in copies:
            cp.wait()

    out_shape = [jax.ShapeDtypeStruct(h.shape, h.dtype) for h in halves]
    return pl.pallas_call(body, in_specs=[ANY] * n, out_specs=[ANY] * n, out_shape=out_shape,
                          scratch_shapes=[pltpu.SemaphoreType.DMA((n,)), pltpu.SemaphoreType.DMA((n,))],
                          name="join_halves")(*halves)


def _allreduce_small(v):
    rows, n = v.shape

    def body(v_ref, o_ref, buf, send, recv):
        x, y, c, _ = _place()
        me = 4 * x + 2 * y + c
        buf[me] = v_ref[...]
        copies = []
        for d in range(1, 8):
            peer = (x ^ (d >> 2), y ^ ((d >> 1) & 1), c ^ (d & 1))
            cp = pltpu.make_async_remote_copy(src_ref=v_ref, dst_ref=buf.at[me], send_sem=send.at[d], recv_sem=recv.at[d],
                                              device_id=peer, device_id_type=MESH)
            cp.start()
            copies.append(cp)
        for d in range(1, 8):
            peer = 4 * (x ^ (d >> 2)) + 2 * (y ^ ((d >> 1) & 1)) + (c ^ (d & 1))
            pltpu.make_async_remote_copy(src_ref=v_ref, dst_ref=buf.at[peer], send_sem=send.at[d], recv_sem=recv.at[d],
                                         device_id=(x, y, c), device_id_type=MESH).wait_recv()
        for cp in copies:
            cp.wait_send()
        acc = buf[0]
        for d in range(1, 8):
            acc = acc + buf[d]
        o_ref[...] = acc

    vm = pl.BlockSpec(memory_space=pltpu.VMEM)
    return pl.pallas_call(body, in_specs=[vm], out_specs=vm, out_shape=jax.ShapeDtypeStruct((rows, n), F32),
                          scratch_shapes=[pltpu.VMEM((8, rows, n), F32), pltpu.SemaphoreType.DMA((8,)),
                                          pltpu.SemaphoreType.DMA((8,))], name="allreduce_small")(v)


def _add_call(name, parts, tm, out_dtype=F32):
    def fn(i, *vals):
        acc = vals[0].astype(F32)
        for v in vals[1:]:
            acc = acc + v.astype(F32)
        return (acc,), ()
    return _rows_call(name, fn, list(parts), [], [(parts[0].shape[1], out_dtype)], [], tm)[0]


def _adamw(name, w, g, m, v):
    c1 = 1.0 / (1.0 - ADAM_B1 ** ADAM_STEP)
    c2 = 1.0 / (1.0 - ADAM_B2 ** ADAM_STEP)

    def fn(i, wv, gv, mv, vv):
        m2 = ADAM_B1 * mv + (1.0 - ADAM_B1) * gv
        v2 = ADAM_B2 * vv + (1.0 - ADAM_B2) * jnp.square(gv)
        delta = -ADAM_LR * ((m2 * c1) / (jnp.sqrt(v2 * c2) + ADAM_EPS) + ADAM_WD * wv)
        return (delta, m2, v2), ()
    cols = w.shape[1]
    tm = _tile(w.shape[0], (256, 176, 128, 64, 8))
    return _rows_call(name, fn, [w, g, m, v], [], [(cols, F32)] * 3, [], tm)


def _canon(a, transposed):
    return a[0].T if transposed else a[0]


def _pack_sharded(w_up, a_up, g_up, conv_dw):
    parts = [w_up[0].T, a_up[0].T, g_up[0].T, conv_dw[0].T]
    used = sum(p.shape[1] for p in parts)
    return jnp.concatenate(parts + [jnp.zeros((parts[0].shape[0], PACK_W - used), F32)], axis=1)


def _unpack_sharded(a):
    return [a[:, 0:64].T[None], a[:, 64:128].T[None], a[:, 128:256].T[None], a[:, 256:256 + CW].T[None]]


def _pack_small(vals):
    flat = jnp.concatenate([v.reshape(-1) for v in vals] + [jnp.zeros((SMALL_PAD - SMALL_N,), F32)])
    return flat.reshape(8, SMALL_PAD // 8)


def _unpack_small(a, like):
    flat, out, off = a.reshape(-1), [], 0
    for (_, n), ref in zip(SMALL, like):
        out.append(flat[off:off + n].reshape(ref.shape))
        off += n
    return out


def kernel(x, ffn1_norm_pre, ffn1_norm_post, ffn1_w_gu, ffn1_w_down, mix_norm_pre, mix_norm_post, w_in, shift_mu, w_up, w0, a_up, a0, g_up, k_k, k_a, r_k, gn_w, gn_b, conv_dw, conv_b, conv_ln_w, conv_ln_b, w_out, ffn2_norm_pre, ffn2_norm_post, ffn2_w_gu, ffn2_w_down, loss_target, m_ffn1_norm_pre, m_ffn1_norm_post, m_ffn1_w_gu, m_ffn1_w_down, m_mix_norm_pre, m_mix_norm_post, m_w_in, m_shift_mu, m_w_up, m_w0, m_a_up, m_a0, m_g_up, m_k_k, m_k_a, m_r_k, m_gn_w, m_gn_b, m_conv_dw, m_conv_b, m_conv_ln_w, m_conv_ln_b, m_w_out, m_ffn2_norm_pre, m_ffn2_norm_post, m_ffn2_w_gu, m_ffn2_w_down, v_ffn1_norm_pre, v_ffn1_norm_post, v_ffn1_w_gu, v_ffn1_w_down, v_mix_norm_pre, v_mix_norm_post, v_w_in, v_shift_mu, v_w_up, v_w0, v_a_up, v_a0, v_g_up, v_k_k, v_k_a, v_r_k, v_gn_w, v_gn_b, v_conv_dw, v_conv_b, v_conv_ln_w, v_conv_ln_b, v_w_out, v_ffn2_norm_pre, v_ffn2_norm_post, v_ffn2_w_gu, v_ffn2_w_down):
    w = dict(ffn1_norm_pre=ffn1_norm_pre, ffn1_norm_post=ffn1_norm_post, ffn1_w_gu=ffn1_w_gu, ffn1_w_down=ffn1_w_down, mix_norm_pre=mix_norm_pre, mix_norm_post=mix_norm_post, w_in=w_in, shift_mu=shift_mu, w_up=w_up, w0=w0, a_up=a_up, a0=a0, g_up=g_up, k_k=k_k, k_a=k_a, r_k=r_k, gn_w=gn_w, gn_b=gn_b, conv_dw=conv_dw, conv_b=conv_b, conv_ln_w=conv_ln_w, conv_ln_b=conv_ln_b, w_out=w_out, ffn2_norm_pre=ffn2_norm_pre, ffn2_norm_post=ffn2_norm_post, ffn2_w_gu=ffn2_w_gu, ffn2_w_down=ffn2_w_down)
    mom = dict(ffn1_norm_pre=m_ffn1_norm_pre, ffn1_norm_post=m_ffn1_norm_post, ffn1_w_gu=m_ffn1_w_gu, ffn1_w_down=m_ffn1_w_down, mix_norm_pre=m_mix_norm_pre, mix_norm_post=m_mix_norm_post, w_in=m_w_in, shift_mu=m_shift_mu, w_up=m_w_up, w0=m_w0, a_up=m_a_up, a0=m_a0, g_up=m_g_up, k_k=m_k_k, k_a=m_k_a, r_k=m_r_k, gn_w=m_gn_w, gn_b=m_gn_b, conv_dw=m_conv_dw, conv_b=m_conv_b, conv_ln_w=m_conv_ln_w, conv_ln_b=m_conv_ln_b, w_out=m_w_out, ffn2_norm_pre=m_ffn2_norm_pre, ffn2_norm_post=m_ffn2_norm_post, ffn2_w_gu=m_ffn2_w_gu, ffn2_w_down=m_ffn2_w_down)
    var = dict(ffn1_norm_pre=v_ffn1_norm_pre, ffn1_norm_post=v_ffn1_norm_post, ffn1_w_gu=v_ffn1_w_gu, ffn1_w_down=v_ffn1_w_down, mix_norm_pre=v_mix_norm_pre, mix_norm_post=v_mix_norm_post, w_in=v_w_in, shift_mu=v_shift_mu, w_up=v_w_up, w0=v_w0, a_up=v_a_up, a0=v_a0, g_up=v_g_up, k_k=v_k_k, k_a=v_k_a, r_k=v_r_k, gn_w=v_gn_w, gn_b=v_gn_b, conv_dw=v_conv_dw, conv_b=v_conv_b, conv_ln_w=v_conv_ln_w, conv_ln_b=v_conv_ln_b, w_out=v_w_out, ffn2_norm_pre=v_ffn2_norm_pre, ffn2_norm_post=v_ffn2_norm_post, ffn2_w_gu=v_ffn2_w_gu, ffn2_w_down=v_ffn2_w_down)
    names = list(w)

    x0 = x[0]
    tgt = loss_target[0]
    t_len = x0.shape[0]
    tm = _tile(t_len, (512, 256, 128))

    half = {n: rows // 8 for n, rows, _ in BIG}
    pack_local = _pack_sharded(w["w_up"], w["a_up"], w["g_up"], w["conv_dw"])
    shard = {n: _canon(w[n], tr).astype(BF16).reshape(2, half[n], D) for n, _, tr in BIG}
    rows_of = {n: rows for n, rows, _ in BIG}
    full = {"ffn1_w_gu": _gather_weights([shard["ffn1_w_gu"]])[0].reshape(2 * DFF, D)}

    h1, gu1, act1, f1, gathered = _ffn_fwd("ffn1", x0, w["ffn1_norm_pre"], full["ffn1_w_gu"], None, tm,
                                           gather=[shard[n] for n in SECOND] + [pack_local.reshape(2, 64, PACK_W)])
    full.update({n: g.reshape(rows_of[n], D) for n, g in zip(SECOND, gathered)})
    pack = gathered[-1].reshape(DR, PACK_W)
    p01, p2 = pack[:, 0:128], pack[:, 128:256]
    dw32 = jnp.concatenate([pack[:, 256:256 + CW].T, jnp.zeros((1, DR), F32)], axis=0)
    def resid_norm(scale):
        def fn(i, xv, fv, g_post, g_pre):
            xn = xv + scale * _rms(fv, g_post)
            return (xn, _rms(xn, g_pre)), ()
        return fn
    x1, hm = _rows_call("ffn1_resid", resid_norm(0.5), [x0, f1], [w["ffn1_norm_post"], w["mix_norm_pre"]],
                        [(D, F32), (D, BF16)], [], tm)
    p = _matmul("mix_in", hm, full["w_in"], "nt")
    ps = _shift_fwd(p, w["shift_mu"], tm)
    rkc = w["r_k"].reshape(1, DR)
    prep_consts = [w["w0"], w["a0"], w["k_k"], w["k_a"], p01, p2]
    r_, dec, k2, v_, z_, b_, g_ = _rows_call(
        "prep", lambda i, psv, *cs: (_prep(psv, *cs), ()), [ps], prep_consts, [(DR, F32)] * 7, [], tm)
    y, ck, late = _rec_fwd(r_, dec, k2, z_, b_, v_, [shard[n] for n in LATE])
    full.update({n: g.reshape(rows_of[n], D) for n, g in zip(LATE, late)})
    glu, cpre, ob = _conv_fwd(p, dw32, w["conv_b"], w["conv_ln_w"], w["conv_ln_b"], tm)
    post_consts = [w["gn_w"], w["gn_b"], rkc]
    (o,) = _rows_call(
        "post", lambda i, yv, rv, kv, vv, gv, obv, *cs: ((jnp.concatenate([_post(yv, rv, kv, vv, gv, *cs).astype(BF16), obv], axis=1),), ()),
        [y, r_, k2, v_, g_, ob], post_consts, [(D, BF16)], [], tm)
    mo = _matmul("mix_out", o, full["w_out"], "nn")
    x2, h2 = _rows_call("mix_resid", resid_norm(1.0), [x1, mo], [w["mix_norm_post"], w["ffn2_norm_pre"]],
                        [(D, F32), (D, BF16)], [], tm)
    h2, gu2, act2, f2, _ = _ffn_fwd("ffn2", x2, w["ffn2_norm_pre"], full["ffn2_w_gu"], full["ffn2_w_down"], tm, h=h2)

    g_small = {}

    def loss_fn(i, xv, fv, tv, g):
        branch, vjp = jax.vjp(lambda a, b: 0.5 * _rms(a, b), fv, g)
        err = xv + branch - tv
        part = 0.5 * jnp.sum(jnp.mean(err * err, axis=-1, keepdims=True), axis=0, keepdims=True)
        dx = err * (1.0 / D)
        df, dg = vjp(dx)
        return (dx, df), (jnp.broadcast_to(part, (8, 128)), dg)
    dx3, df2, loss_part, g_small["ffn2_norm_post"] = _rows_call(
        "loss", loss_fn, [x2, f2, tgt], [w["ffn2_norm_post"]], [(D, F32), (D, BF16)], [(8, 128), (1, D)], tm)
    loss = lax.psum(loss_part[0, 0], ("x", "y", "c"))

    (dx2, dmo, g_small["ffn2_norm_pre"], g_small["mix_norm_post"]), dgu2_t, dwd2, _ = _ffn_bwd(
        "ffn2", x2, w["ffn2_norm_pre"], h2, gu2, act2, dx3, df2, full["ffn2_w_gu"], full["ffn2_w_down"], tm,
        below=(mo, w["mix_norm_post"], 1.0))
    do = _matmul("mix_do", dmo, full["w_out"], "nt")
    dw_out = _matmul("mix_dwout", o, dmo, "tn")

    def post_b(i, yv, rv, kv, vv, gv, dov, *cs):
        _, vjp = jax.vjp(_post, yv, rv, kv, vv, gv, *cs)
        dy, dr, dk, dv, dg, dgw, dgb, drk = vjp(dov[:, :DR])
        return (dy, dr, dk, dv, dg), (_colsum(dgw), _colsum(dgb), _colsum(drk))
    dy, dr1, dk1, dv1, dg, g_small["gn_w"], g_small["gn_b"], g_small["r_k"] = _rows_call(
        "post_b", post_b, [y, r_, k2, v_, g_, do], post_consts, [(DR, F32)] * 5, [(1, DR)] * 3, tm)
    my_c = lax.axis_index("c")
    my_chip = 2 * lax.axis_index("x") + lax.axis_index("y")
    g_big = dict(w_out=dw_out, ffn2_w_gu=dgu2_t, ffn2_w_down=dwd2)

    def halves_of(group, which):
        return jnp.concatenate([lax.dynamic_index_in_dim(g_big[n].reshape(4, 2, half[n], D), which, 1, keepdims=False)
                                for n in group], axis=1)

    def pair_sums(tag, group, extra_mine=(), extra_give=()):
        mine = halves_of(group, my_c)
        got, *got_x = _swap_halves("swap_halves_" + tag, [halves_of(group, 1 - my_c).astype(BF16)] + list(extra_give))
        rows = mine.shape[1]
        tile = _tile(rows, (352, 592, 16))
        travels = _add_call("add_pair_" + tag, [mine.reshape(4 * rows, D), got.reshape(4 * rows, D)], tile, BF16)
        return mine, got, travels.reshape(4, rows, D), got_x

    def owner_sum(tag, mine, got, parts):
        own = [lax.dynamic_index_in_dim(a, my_chip, 0, keepdims=False) for a in (mine, got)]
        return _add_call("add_chips_" + tag, own + [parts[m] for m in range(3)], _tile(mine.shape[1], (352, 592, 16)))

    mine_l, got_l, sum_l, _ = pair_sums("late", LATE)
    (dr2, ddec, dk2, dz, db, dv2), (parts_l,) = _rec_bwd(r_, dec, k2, z_, b_, v_, dy, ck, [sum_l])

    def prep_b(i, psv, a1, a2, c1, c2, e1, e2, dwv, dzv, dbv, dgv, *cs):
        _, vjp = jax.vjp(_prep, psv, *cs)
        dps, dw0, da0, dkk, dka, dp01, dp2 = vjp((a1 + a2, dwv, c1 + c2, e1 + e2, dzv, dbv, dgv))
        return (dps,), (_colsum(dw0), _colsum(da0), _colsum(dkk), _colsum(dka), dp01, dp2)
    dps, g_small["w0"], g_small["a0"], g_small["k_k"], g_small["k_a"], dp01, dp2 = _rows_call(
        "prep_b", prep_b, [ps, dr1, dr2, dk1, dk2, dv1, dv2, ddec, dz, db, dg], prep_consts, [(D_SHIFT, F32)],
        [(1, DR)] * 4 + [(DR, 128)] * 2, tm)

    def convln_b(i, cv, dov, lw, lb):
        _, vjp = jax.vjp(_ln_silu, cv, lw, lb)
        dc, dlw, dlb = vjp(dov[:, DR:])
        return (dc,), (_colsum(dc), _colsum(dlw), _colsum(dlb))
    dc, g_small["conv_b"], g_small["conv_ln_w"], g_small["conv_ln_b"] = _rows_call(
        "convln_b", convln_b, [cpre, do], [w["conv_ln_w"], w["conv_ln_b"]], [(DR, F32)], [(1, DR)] * 3, tm)
    dpc, ddw32 = _conv_bwd(dc, glu, p, dw32, tm)
    dp, g_small["shift_mu"] = _shift_bwd(dps, p, dpc, w["shift_mu"], tm)
    dhm = _matmul("mix_dh", dp, full["w_in"], "nn")
    dw_in_t = _matmul("mix_dwin", dp, hm, "tn")

    dx1, df1, g_small["mix_norm_pre"], g_small["ffn1_norm_post"] = _norm_resid_b(
        "mix_norm_b", x1, dhm, dx2, w["mix_norm_pre"], tm, below=(f1, w["ffn1_norm_post"], 0.5))
    kept = {}

    def second_sums(dwd1):
        g_big.update(ffn1_w_down=dwd1, w_in=dw_in_t)
        pack_grads = jnp.concatenate([dp01, dp2, ddw32.T, jnp.zeros((DR, PACK_W - 288), F32)], axis=1).reshape(4, 2, 64, PACK_W)
        mine_p, give_p = (lax.dynamic_index_in_dim(pack_grads, which, 1, keepdims=False) for which in (my_c, 1 - my_c))
        mine, got, travels, (got_p,) = pair_sums("second", SECOND, extra_give=[give_p])
        sum_p = _add_call("add_pair_pack", [mine_p.reshape(256, PACK_W), got_p.reshape(256, PACK_W)], 256).reshape(4, 64, PACK_W)
        kept.update(second=(mine, got), sum_p=sum_p)
        return [travels, sum_p]

    def first_sums(dgu1_t):
        g_big.update(ffn1_w_gu=dgu1_t)
        mine, got, travels, _ = pair_sums("first", ("ffn1_w_gu",))
        kept.update(first=(mine, got))
        return [travels]

    (grad_x, g_small["ffn1_norm_pre"]), _, _, ((parts_s, parts_p), (parts_f,)) = _ffn_bwd(
        "ffn1", x0, w["ffn1_norm_pre"], h1, gu1, act1, dx1, df1, full["ffn1_w_gu"], full["ffn1_w_down"], tm,
        after_down=second_sums, after_gu=first_sums)

    sum_p = kept["sum_p"]
    fin_s = owner_sum("second", *kept["second"], parts_s)
    fin_f = owner_sum("first", *kept["first"], parts_f)
    fin_l = owner_sum("late", mine_l, got_l, parts_l)
    fin_p = _add_call("add_chips_pack", [lax.dynamic_index_in_dim(sum_p, my_chip, 0, keepdims=False)] + [parts_p[m] for m in range(3)], 64)
    fins = [fin_f, fin_s, fin_l, fin_p]
    red_f, red_s, red_l, red_p = [jnp.where(my_c == 0, jnp.stack([f, s]), jnp.stack([s, f])) for f, s in zip(fins, _join_halves(fins))]
    small_sum = _allreduce_small(_pack_small([g_small[n] for n, _ in SMALL]))

    grads, delta, new_m, new_v = {}, {}, {}, {}
    reduced = {}
    for group, red in ((("ffn1_w_gu",), red_f), (SECOND, red_s), (LATE, red_l)):
        off = 0
        for n in group:
            reduced[n] = red[:, off:off + half[n], :].reshape(rows_of[n] // 4, D)
            off += half[n]
    for n, rows, tr in BIG:
        g = reduced[n]
        g = (g.T if tr else g)[None]
        grads[n] = g
        d_, m_, v2_ = _adamw("adamw_" + n, w[n][0], g[0], mom[n][0], var[n][0])
        delta[n], new_m[n], new_v[n] = d_[None], m_[None], v2_[None]
    sh = ("w_up", "a_up", "g_up", "conv_dw")
    g_pack = red_p.reshape(128, PACK_W)
    d_, m_, v2_ = _adamw("adamw_pack", pack_local, g_pack, _pack_sharded(*[mom[n] for n in sh]), _pack_sharded(*[var[n] for n in sh]))
    for dst, src in ((grads, g_pack), (delta, d_), (new_m, m_), (new_v, v2_)):
        for n, a in zip(sh, _unpack_sharded(src)):
            dst[n] = a
    sm = [n for n, _ in SMALL]
    d_, m_, v2_ = _adamw("adamw_small", _pack_small([w[n] for n in sm]), small_sum, _pack_small([mom[n] for n in sm]),
                         _pack_small([var[n] for n in sm]))
    like = [w[n] for n in sm]
    for dst, src in ((grads, small_sum), (delta, d_), (new_m, m_), (new_v, v2_)):
        for n, a in zip(sm, _unpack_small(src, like)):
            dst[n] = a

    wn = names
    return (loss, grad_x[None], *[grads[n] for n in wn], *[delta[n] for n in wn], *[new_m[n] for n in wn],
            *[new_v[n] for n in wn])
```
